```python
import math, functools
import jax, jax.numpy as jnp
from jax import lax
import numpy as np

D_MODEL = 1024
BATCH = 8
SEQ = 2048
DEPTH = 1
DEC_BATCH = 32
DEC_SEQ = 1
PAST_LEN = 16384
PAGE_SIZE = 128

MIX_WIDTH = D_MODEL
HG_WIDTH = MIX_WIDTH // 2
HG_HEADS = 4
HG_DK = HG_WIDTH // HG_HEADS
HG_DV = HG_WIDTH // HG_HEADS
HG_CHUNK = 64
DA_WIDTH = MIX_WIDTH - HG_WIDTH
DA_HEADS = 4
DA_DV = DA_WIDTH // DA_HEADS
DA_DH = DA_DV // 2
DA_DK = 2 * DA_DH
Q_BLOCK = 128
ROPE_THETA = 10000.0
N_MEM = 256
MEM_HEADS = 4
MEM_DH = D_MODEL // MEM_HEADS
D_FF = -(-8 * D_MODEL // (3 * 256)) * 256
IN_COLS = 4 * HG_WIDTH + 3 * DA_WIDTH
EPS = 1e-6

kernel_name = "hymba_hgrn2_diffattn_decode_step"

F32 = jnp.float32


def rmsnorm(x, g):
    x32 = x.astype(F32)
    y = x32 * lax.rsqrt(jnp.mean(x32 * x32, axis=-1, keepdims=True) + EPS)
    return (y * g.astype(F32)).astype(x.dtype)


def rope(x, pos):
    d = x.shape[-1]
    inv = ROPE_THETA ** (-jnp.arange(0, d, 2, dtype=F32) / d)
    ang = pos.astype(F32)[:, None] * inv[None, :]
    cos = jnp.cos(ang)[:, None, :]
    sin = jnp.sin(ang)[:, None, :]
    x32 = x.astype(F32)
    x1, x2 = x32[..., : d // 2], x32[..., d // 2:]
    return jnp.concatenate([x1 * cos - x2 * sin, x1 * sin + x2 * cos], axis=-1).astype(x.dtype)


def mixer_inputs(h, pos, w_in_l, lb_l):
    B, L, _ = h.shape
    z = h @ w_in_l
    sizes = [HG_WIDTH] * 4 + [DA_WIDTH] * 3
    splits = [int(s) for s in np.cumsum(sizes)[:-1]]
    hq, hf, hi, hg, dq, dk, dv = jnp.split(z, splits, axis=-1)
    lb = lb_l.reshape(HG_HEADS, HG_DK)
    zf = hf.astype(F32).reshape(B, L, HG_HEADS, HG_DK)
    logf = jnp.log(lb + (1.0 - lb) * jax.nn.sigmoid(zf))
    k_hg = (1.0 - lb) * jax.nn.sigmoid(-zf)
    q_hg = hq.astype(F32).reshape(B, L, HG_HEADS, HG_DK)
    v_hg = hi.astype(F32).reshape(B, L, HG_HEADS, HG_DV)
    q_da = rope(dq.reshape(B, L, DA_HEADS * 2, DA_DH), pos).reshape(B, L, DA_HEADS, 2, DA_DH)
    k_da = rope(dk.reshape(B, L, DA_HEADS * 2, DA_DH), pos).reshape(B, L, DA_HEADS, 2, DA_DH)
    v_da = dv.reshape(B, L, DA_HEADS, DA_DV)
    return q_hg, k_hg, logf, v_hg, hg, q_da, k_da, v_da


def hgrn2_recurrence(q, k, logf, v, s0):
    B, L, H, DK = q.shape
    DV = v.shape[-1]
    C = HG_CHUNK if L % HG_CHUNK == 0 else L
    n = L // C

    def to_chunks(a):
        return a.reshape(B, n, C, H, a.shape[-1]).transpose(1, 0, 3, 2, 4)

    qc, kc, gc, vc = to_chunks(q), to_chunks(k), to_chunks(logf), to_chunks(v)
    causal = jnp.tril(jnp.ones((C, C), dtype=bool))

    def step(S, inp):
        qb, kb, gb, vb = inp
        b = jnp.cumsum(gb, axis=2)
        diff = b[:, :, :, None, :] - b[:, :, None, :, :]
        decay = jnp.exp(jnp.where(causal[:, :, None], diff, -jnp.inf))
        A = jnp.einsum('bhtk,bhsk,bhtsk->bhts', qb, kb, decay)
        o = jnp.einsum('bhts,bhsv->bhtv', A, vb) + jnp.einsum('bhtk,bhkv->bhtv', qb * jnp.exp(b), S)
        b_last = b[:, :, -1:, :]
        S_new = jnp.exp(b_last[:, :, 0, :])[..., None] * S + jnp.einsum(
            'bhsk,bhsv->bhkv', kb * jnp.exp(b_last - b), vb)
        return S_new, o

    S, o = lax.scan(step, s0.astype(F32), (qc, kc, gc, vc))
    o = o.transpose(1, 0, 3, 2, 4).reshape(B, L, H, DV)
    return o, S


def diff_weights(s, lam):
    p = jax.nn.softmax(s.astype(F32), axis=-1)
    return p[:, :, 0] - lam * p[:, :, 1]


def diff_attn_prompt(q, k, v, lam):
    B, S, H, _, DH = q.shape
    nb = S // Q_BLOCK
    scale = DH ** -0.5
    qb = q.reshape(B, nb, Q_BLOCK, H, 2, DH).transpose(1, 0, 2, 3, 4, 5)
    kpos = jnp.arange(S)

    def block(args):
        qi, i = args
        qpos = i * Q_BLOCK + jnp.arange(Q_BLOCK)
        s = jnp.einsum('bqhmd,bkhmd->bhmqk', qi, k).astype(F32) * scale
        s = jnp.where((kpos[None, :] <= qpos[:, None])[None, None, None], s, -jnp.inf)
        w = diff_weights(s, lam)
        return jnp.einsum('bhqk,bkhv->bqhv', w.astype(v.dtype), v)

    o = lax.map(block, (qb, jnp.arange(nb)))
    return o.transpose(1, 0, 2, 3, 4).reshape(B, S, H, v.shape[-1])


def diff_attn_sample(q, k, v, lam, k_past, v_past):
    T = q.shape[1]
    P = k_past.shape[1]
    scale = q.shape[-1] ** -0.5
    s_past = jnp.einsum('bqhmd,bkhmd->bhmqk', q, k_past).astype(F32) * scale
    s_new = jnp.einsum('bqhmd,bkhmd->bhmqk', q, k).astype(F32) * scale
    s_new = jnp.where(jnp.tril(jnp.ones((T, T), dtype=bool))[None, None, None], s_new, -jnp.inf)
    w = diff_weights(jnp.concatenate([s_past, s_new], axis=-1), lam).astype(v.dtype)
    return (jnp.einsum('bhqk,bkhv->bqhv', w[..., :P], v_past)
            + jnp.einsum('bhqk,bkhv->bqhv', w[..., P:], v))


def mem_kv(mem, g, wk, wv):
    B = mem.shape[0]
    m = rmsnorm(mem, g)
    k = (m @ wk).reshape(B, -1, MEM_HEADS, MEM_DH)
    v = (m @ wv).reshape(B, -1, MEM_HEADS, MEM_DH)
    return k, v


def cross_attn(h, mk, mv, wq, wo):
    B, L, _ = h.shape
    q = (h @ wq).reshape(B, L, MEM_HEADS, MEM_DH)
    s = jnp.einsum('blhd,bnhd->bhln', q, mk).astype(F32) * (MEM_DH ** -0.5)
    p = jax.nn.softmax(s, axis=-1).astype(mv.dtype)
    o = jnp.einsum('bhln,bnhd->blhd', p, mv).reshape(B, L, MEM_HEADS * MEM_DH)
    return o @ wo


def swiglu(h, wg, wu, wd):
    return (jax.nn.silu(h @ wg) * (h @ wu)) @ wd


def setup_inputs(seed: int = 0) -> dict:
    key = jax.random.key(seed)
    ks = jax.random.split(key, 32)
    n_pages = PAST_LEN // PAGE_SIZE
    n_used = DEC_BATCH * n_pages
    n_phys = n_used + n_used // 4
    nrm = lambda k, shape, s=1.0: jax.random.normal(k, shape, dtype=F32) * s
    gain = lambda k, shape: 1.0 + 0.02 * jax.random.normal(k, shape, dtype=F32)
    page_table = jax.random.permutation(ks[0], n_phys)[:n_used].reshape(DEC_BATCH, n_pages).astype(jnp.int32)
    return {
        "x_prompt": nrm(ks[1], (BATCH, SEQ, D_MODEL)),
        "x_sample": nrm(ks[2], (DEC_BATCH, DEC_SEQ, D_MODEL)),
        "mem_prompt": nrm(ks[3], (BATCH, N_MEM, D_MODEL)),
        "cache_k": nrm(ks[4], (DEPTH, n_phys, PAGE_SIZE, DA_HEADS, DA_DK)),
        "cache_v": nrm(ks[5], (DEPTH, n_phys, PAGE_SIZE, DA_HEADS, DA_DV)),
        "cache_mem_k": nrm(ks[6], (DEPTH, DEC_BATCH, N_MEM, MEM_HEADS, MEM_DH)),
        "cache_mem_v": nrm(ks[7], (DEPTH, DEC_BATCH, N_MEM, MEM_HEADS, MEM_DH)),
        "state_hgrn": nrm(ks[8], (DEPTH, DEC_BATCH, HG_HEADS, HG_DK, HG_DV), 0.5),
        "page_table": page_table,
        "norm_mix": gain(ks[9], (DEPTH, D_MODEL)),
        "w_in": nrm(ks[10], (DEPTH, D_MODEL, IN_COLS), D_MODEL ** -0.5),
        "hg_lb": nrm(ks[11], (DEPTH + 1, HG_WIDTH), 0.5),
        "hg_onorm": gain(ks[12], (DEPTH, HG_DV)),
        "da_lambda": nrm(ks[13], (DEPTH, 4, DA_DH), 0.1),
        "da_onorm": gain(ks[14], (DEPTH, DA_DV)),
        "w_out": nrm(ks[15], (DEPTH, MIX_WIDTH, D_MODEL), MIX_WIDTH ** -0.5),
        "norm_mem_q": gain(ks[16], (DEPTH, D_MODEL)),
        "norm_mem_kv": gain(ks[17], (DEPTH, D_MODEL)),
        "w_mq": nrm(ks[18], (DEPTH, D_MODEL, MEM_HEADS * MEM_DH), D_MODEL ** -0.5),
        "w_mk": nrm(ks[19], (DEPTH, D_MODEL, MEM_HEADS * MEM_DH), D_MODEL ** -0.5),
        "w_mv": nrm(ks[20], (DEPTH, D_MODEL, MEM_HEADS * MEM_DH), D_MODEL ** -0.5),
        "w_mo": nrm(ks[21], (DEPTH, MEM_HEADS * MEM_DH, D_MODEL), (MEM_HEADS * MEM_DH) ** -0.5),
        "norm_ffn": gain(ks[22], (DEPTH, D_MODEL)),
        "w_gate": nrm(ks[23], (DEPTH, D_MODEL, D_FF), D_MODEL ** -0.5),
        "w_up": nrm(ks[24], (DEPTH, D_MODEL, D_FF), D_MODEL ** -0.5),
        "w_down": nrm(ks[25], (DEPTH, D_FF, D_MODEL), D_FF ** -0.5),
        "norm_final": gain(ks[26], (D_MODEL,)),
    }


def reference(x_prompt, x_sample, mem_prompt, cache_k, cache_v, cache_mem_k, cache_mem_v, state_hgrn,
              page_table, norm_mix, w_in, hg_lb, hg_onorm, da_lambda, da_onorm, w_out, norm_mem_q,
              norm_mem_kv, w_mq, w_mk, w_mv, w_mo, norm_ffn, w_gate, w_up, w_down, norm_final):
    lb_all = jnp.cumsum(jax.nn.softmax(hg_lb.astype(F32), axis=0), axis=0)
    dec_b = x_sample.shape[0]
    past_len = page_table.shape[1] * cache_k.shape[2]
    pos_p = jnp.arange(x_prompt.shape[1])
    pos_s = past_len + jnp.arange(x_sample.shape[1])

    def run_layer(x, pos, hg_s0, attend, mk, mv, l):
        B, L, _ = x.shape
        h = rmsnorm(x, norm_mix[l])
        q_hg, k_hg, logf, v_hg, g_hg, q_da, k_da, v_da = mixer_inputs(h, pos, w_in[l], lb_all[l])
        o_hg, s_hg = hgrn2_recurrence(q_hg, k_hg, logf, v_hg, hg_s0)
        o_hg = rmsnorm(o_hg, hg_onorm[l]) * jax.nn.silu(g_hg.astype(F32)).reshape(B, L, HG_HEADS, HG_DV)
        lam_init = 0.8 - 0.6 * math.exp(-0.3 * l)
        lp = da_lambda[l].astype(F32)
        lam = jnp.exp(jnp.sum(lp[0] * lp[1])) - jnp.exp(jnp.sum(lp[2] * lp[3])) + lam_init
        o_da = attend(q_da, k_da, v_da, lam)
        o_da = rmsnorm(o_da, da_onorm[l]) * (1.0 - lam_init)
        mix = jnp.concatenate([o_hg.reshape(B, L, HG_WIDTH).astype(x.dtype),
                               o_da.reshape(B, L, DA_WIDTH).astype(x.dtype)], axis=-1)
        x = x + mix @ w_out[l]
        x = x + cross_attn(rmsnorm(x, norm_mem_q[l]), mk, mv, w_mq[l], w_mo[l])
        x = x + swiglu(rmsnorm(x, norm_ffn[l]), w_gate[l], w_up[l], w_down[l])
        return x, s_hg.astype(x.dtype), k_da.reshape(B, L, DA_HEADS, DA_DK), v_da

    xp, xs = x_prompt, x_sample
    hsp, kp_l, vp_l, mkp_l, mvp_l, hss, ks_l, vs_l = [], [], [], [], [], [], [], []
    for l in range(DEPTH):
        mk, mv = mem_kv(mem_prompt, norm_mem_kv[l], w_mk[l], w_mv[l])
        s0 = jnp.zeros((xp.shape[0], HG_HEADS, HG_DK, HG_DV), F32)
        xp, s_p, k_p, v_p = run_layer(xp, pos_p, s0, diff_attn_prompt, mk, mv, l)
        hsp.append(s_p); kp_l.append(k_p); vp_l.append(v_p); mkp_l.append(mk); mvp_l.append(mv)
        k_past = cache_k[l][page_table].reshape(dec_b, past_len, DA_HEADS, 2, DA_DH)
        v_past = cache_v[l][page_table].reshape(dec_b, past_len, DA_HEADS, DA_DV)
        attend_s = functools.partial(diff_attn_sample, k_past=k_past, v_past=v_past)
        xs, s_s, k_s, v_s = run_layer(xs, pos_s, state_hgrn[l], attend_s, cache_mem_k[l], cache_mem_v[l], l)
        hss.append(s_s); ks_l.append(k_s); vs_l.append(v_s)

    y_prompt = rmsnorm(xp, norm_final)
    y_sample = rmsnorm(xs, norm_final)
    return (y_prompt, y_sample, jnp.stack(hsp), jnp.stack(kp_l), jnp.stack(vp_l), jnp.stack(mkp_l),
            jnp.stack(mvp_l), jnp.stack(hss), jnp.stack(ks_l), jnp.stack(vs_l))
```

```python
import functools
import math

import jax
import jax.numpy as jnp
from jax import lax
from jax.experimental import pallas as pl
from jax.experimental.pallas import tpu as pltpu

F32 = jnp.float32
BF16 = jnp.bfloat16
EPS = 1e-6
ROPE_THETA = 10000.0

HG_HEADS = 4
DA_HEADS = 4
MEM_HEADS = 4
HG_CHUNK = 64
LANES = 128
VMEM_LIMIT_CAP = 56 << 20

_NT = (((1,), (1,)), ((), ()))
_TN = (((0,), (0,)), ((), ()))


def _dot(a, b):
    return jnp.dot(a, b, preferred_element_type=F32)


def _dot_nt(a, b):
    return lax.dot_general(a, b, _NT, preferred_element_type=F32)


def _dot_tn(a, b):
    return lax.dot_general(a, b, _TN, preferred_element_type=F32)


def _rms(x, g):
    ms = jnp.mean(x * x, axis=-1, keepdims=True)
    return x * lax.rsqrt(ms + EPS) * g


def _sigmoid(x):
    return 1.0 / (1.0 + jnp.exp(-x))


def _cparams(semantics, vmem_bytes):
    return pltpu.CompilerParams(
        dimension_semantics=semantics,
        vmem_limit_bytes=int(min(max(vmem_bytes, 16 << 20), VMEM_LIMIT_CAP)))


def _const_spec(shape):
    nd = len(shape)
    return pl.BlockSpec(shape, lambda *_: (0,) * nd, pipeline_mode=pl.Buffered(1))


def _lower_bound(lb_ref):
    a0 = lb_ref[0:1, :]
    a1 = lb_ref[1:2, :]
    m = jnp.maximum(a0, a1)
    e0 = jnp.exp(a0 - m)
    e1 = jnp.exp(a1 - m)
    return e0 / (e0 + e1)


def _rope(x, cos, sin_signed):
    n = x.shape[-1]
    lane = lax.broadcasted_iota(jnp.int32, x.shape, x.ndim - 1)
    swapped = jnp.where((lane & 63) < 32,
                        pltpu.roll(x, n - 32, x.ndim - 1),
                        pltpu.roll(x, 32, x.ndim - 1))
    return x * cos + swapped * sin_signed


def _lambda(lam_ref, lam_init):
    lp = lam_ref[...]
    s01 = jnp.sum(lp[0:1, :] * lp[1:2, :], axis=-1, keepdims=True)
    s23 = jnp.sum(lp[2:3, :] * lp[3:4, :], axis=-1, keepdims=True)
    return jnp.exp(s01) - jnp.exp(s23) + lam_init


def _mixer_sections(z, lb):
    w = z.shape[-1] // 7
    hq, zf, hi, hg, dq, dk, dv = (z[:, i * w:(i + 1) * w] for i in range(7))
    sig = _sigmoid(zf)
    logf = jnp.log(lb + (1.0 - lb) * sig)
    k_hg = (1.0 - lb) * (1.0 - sig)
    gate = hg * _sigmoid(hg)
    return hq, k_hg, logf, hi, gate, dq, dk, dv


def _head_rms(o, g):
    outs = []
    for h in range(o.shape[-1] // LANES):
        oh = o[:, h * LANES:(h + 1) * LANES]
        outs.append(_rms(oh, g))
    return jnp.concatenate(outs, axis=-1)


def _memkv_kernel(mem_ref, g_ref, wk_ref, wv_ref, k_ref, v_ref, kb_ref, vb_ref):
    m = _rms(mem_ref[0], g_ref[...]).astype(BF16)
    k = _dot(m, wk_ref[...])
    v = _dot(m, wv_ref[...])
    k_ref[0] = k
    v_ref[0] = v
    kb_ref[0] = k.astype(BF16)
    vb_ref[0] = v.astype(BF16)


def _mem_kv(mem, g, wk, wv):
    B, N, D = mem.shape
    W = wk.shape[1]
    blk = lambda d: pl.BlockSpec((1, N, d), lambda b: (b, 0, 0))
    return pl.pallas_call(
        _memkv_kernel,
        grid=(B,),
        in_specs=[blk(D), _const_spec((1, D)), _const_spec((D, W)), _const_spec((D, W))],
        out_specs=[blk(W)] * 4,
        out_shape=[jax.ShapeDtypeStruct((B, N, W), F32)] * 2
        + [jax.ShapeDtypeStruct((B, N, W), BF16)] * 2,
        compiler_params=_cparams(("arbitrary",), 40 << 20),
        name="mem_kv",
    )(mem, g, wk, wv)


def _hgrn_chunk(q, kk, lf, v, st_ref):
    C, W = q.shape
    t = lax.broadcasted_iota(jnp.int32, (C, W), 0)
    ti = lax.broadcasted_iota(jnp.int32, (C, C), 0)
    si = lax.broadcasted_iota(jnp.int32, (C, C), 1)
    n_lvl = int(math.log2(C))

    c = lf
    q_lvls, k_lvls = [], []
    for l in range(n_lvl):
        m = 1 << l
        upper = (t & m) != 0
        if m < 8:
            y = c
            for i in range(l):
                y = jnp.where((t & (1 << i)) == 0, pltpu.roll(y, C - (1 << i), 0), y)
            bc = jnp.where(upper, pltpu.roll(y, m, 0), y)
        else:
            pieces = []
            for j in range(C // (2 * m)):
                r = 2 * m * j + m - 1
                pieces.append(jnp.broadcast_to(c[r:r + 1, :], (2 * m, W)))
            bc = pieces[0] if len(pieces) == 1 else jnp.concatenate(pieces, axis=0)
        e = jnp.exp(jnp.where(upper, c, bc - c))
        q_lvls.append(jnp.where(upper, q * e, 0.0).astype(BF16))
        k_lvls.append(jnp.where(upper, 0.0, kk * e).astype(BF16))
        c = c + jnp.where(upper, bc, 0.0)
    b = c
    b_last = b[C - 1:C, :]
    q_bf = q.astype(BF16)
    k_bf = kk.astype(BF16)
    v_bf = v.astype(BF16)
    q_state = (q * jnp.exp(b)).astype(BF16)
    k_state = (kk * jnp.exp(b_last - b)).astype(BF16)
    decay = jnp.exp(b_last)

    outs = []
    for h in range(W // LANES):
        sl = slice(h * LANES, (h + 1) * LANES)
        a = jnp.where(ti == si, _dot_nt(q_bf[:, sl], k_bf[:, sl]), 0.0)
        for l in range(n_lvl):
            same_seg = (ti >> (l + 1)) == (si >> (l + 1))
            a = a + jnp.where(same_seg, _dot_nt(q_lvls[l][:, sl], k_lvls[l][:, sl]), 0.0)
        st = st_ref[h]
        o = _dot(a.astype(BF16), v_bf[:, sl]) + _dot_nt(q_state[:, sl], st.astype(BF16))
        st_ref[h] = st * decay[:, sl] + _dot_tn(v_bf[:, sl], k_state[:, sl])
        outs.append(o)
    return jnp.concatenate(outs, axis=-1)


def _prompt_mix_kernel(x_ref, g_ref, w_ref, lb_ref, cos_ref, sin_ref, onorm_ref,
                       qd_ref, k_ref, v_ref, kb_ref, vb_ref, mix_ref, state_ref,
                       q_s, kk_s, lf_s, vh_s, o_s, st_s):
    s = pl.program_id(1)

    @pl.when(s == 0)
    def _():
        st_s[...] = jnp.zeros_like(st_s)

    h = _rms(x_ref[0], g_ref[...]).astype(BF16)
    z = _dot(h, w_ref[...])
    lb = _lower_bound(lb_ref)
    hq, k_hg, logf, hi, gate, dq, dk, dv = _mixer_sections(z, lb)

    cos = cos_ref[...]
    sin = sin_ref[...]
    q_da = _rope(dq, cos, sin)
    k_da = _rope(dk, cos, sin)
    qd_ref[0] = (q_da * (float(LANES // 2) ** -0.5)).astype(BF16)
    k_ref[0] = k_da
    kb_ref[0] = k_da.astype(BF16)
    v_ref[0] = dv
    vb_ref[0] = dv.astype(BF16)

    q_s[...] = hq
    kk_s[...] = k_hg
    lf_s[...] = logf
    vh_s[...] = hi

    def chunk(ci, carry):
        rows = pl.ds(pl.multiple_of(ci * HG_CHUNK, HG_CHUNK), HG_CHUNK)
        o_s[rows, :] = _hgrn_chunk(q_s[rows, :], kk_s[rows, :], lf_s[rows, :],
                                   vh_s[rows, :], st_s)
        return carry

    lax.fori_loop(0, x_ref.shape[1] // HG_CHUNK, chunk, 0)

    mix_ref[0] = (_head_rms(o_s[...], onorm_ref[...]) * gate).astype(BF16)

    @pl.when(s == pl.num_programs(1) - 1)
    def _():
        for hh in range(HG_HEADS):
            state_ref[0, hh] = st_s[hh].T


def _prompt_mix(x, g, w_in, hg_lb, cos, sin, onorm, ts):
    B, S, D = x.shape
    W = w_in.shape[1] // 7
    tok = lambda d: pl.BlockSpec((1, ts, d), lambda b, s: (b, s, 0))
    tab = pl.BlockSpec((ts, W), lambda b, s: (s, 0))
    state = pl.BlockSpec((1, HG_HEADS, LANES, LANES), lambda b, s: (b, 0, 0, 0))
    sds = jax.ShapeDtypeStruct
    return pl.pallas_call(
        _prompt_mix_kernel,
        grid=(B, S // ts),
        in_specs=[tok(D), _const_spec((1, D)), _const_spec(w_in.shape),
                  _const_spec(hg_lb.shape), tab, tab, _const_spec((1, LANES))],
        out_specs=[tok(W)] * 6 + [state],
        out_shape=[sds((B, S, W), BF16), sds((B, S, W), F32), sds((B, S, W), F32),
                   sds((B, S, W), BF16), sds((B, S, W), BF16), sds((B, S, W), BF16),
                   sds((B, HG_HEADS, LANES, LANES), F32)],
        scratch_shapes=[pltpu.VMEM((ts, W), F32)] * 5
        + [pltpu.VMEM((HG_HEADS, LANES, LANES), F32)],
        compiler_params=_cparams(("arbitrary", "arbitrary"), 48 << 20),
        name="prompt_mix",
    )(x, g, w_in, hg_lb, cos, sin, onorm)


def _diff_attn_kernel(q_ref, k_ref, v_ref, lam_ref, onorm_ref, o_ref, *, lam_init, tk):
    i = pl.program_id(2)
    q = q_ref[0]
    tq = q.shape[0]
    lane = lax.broadcasted_iota(jnp.int32, q.shape, 1)
    zero = jnp.zeros_like(q)
    qm = (jnp.where(lane < LANES // 2, q, zero), jnp.where(lane >= LANES // 2, q, zero))
    row = lax.broadcasted_iota(jnp.int32, (tq, tk), 0)
    col = lax.broadcasted_iota(jnp.int32, (tq, tk), 1)

    def step(j, carry, masked):
        rows = pl.ds(pl.multiple_of(j * tk, tk), tk)
        k = k_ref[0, rows, :]
        v = v_ref[0, rows, :]
        new = []
        for mp in range(2):
            m_old, l_old, acc = carry[mp]
            s = _dot_nt(qm[mp], k)
            if masked:
                s = jnp.where(col <= row, s, -jnp.inf)
            m_new = jnp.maximum(m_old, jnp.max(s, axis=-1, keepdims=True))
            alpha = jnp.exp(m_old - m_new)
            p = jnp.exp(s - m_new)
            l_new = alpha * l_old + jnp.sum(p, axis=-1, keepdims=True)
            acc = alpha * acc + _dot(p.astype(BF16), v)
            new.append((m_new, l_new, acc))
        return tuple(new)

    init = tuple((jnp.full((tq, 1), -jnp.inf, F32), jnp.zeros((tq, 1), F32),
                  jnp.zeros((tq, LANES), F32)) for _ in range(2))
    carry = lax.fori_loop(0, i, lambda j, c: step(j, c, False), init)
    (_, l0, a0), (_, l1, a1) = step(i, carry, True)
    lam = _lambda(lam_ref, lam_init)
    o = a0 / l0 - lam * (a1 / l1)
    o_ref[0] = (_rms(o, onorm_ref[...]) * (1.0 - lam_init)).astype(BF16)


def _diff_attn_prompt(qd, kb, vb, da_lambda, onorm, lam_init, tq):
    B, S, W = qd.shape
    H = W // LANES
    qspec = pl.BlockSpec((1, tq, LANES), lambda b, h, i: (b, i, h))
    kvspec = pl.BlockSpec((1, S, LANES), lambda b, h, i: (b, 0, h))
    return pl.pallas_call(
        functools.partial(_diff_attn_kernel, lam_init=lam_init, tk=tq),
        grid=(B, H, S // tq),
        in_specs=[qspec, kvspec, kvspec, _const_spec(da_lambda.shape), _const_spec((1, LANES))],
        out_specs=qspec,
        out_shape=jax.ShapeDtypeStruct((B, S, W), BF16),
        compiler_params=_cparams(("arbitrary",) * 3, 32 << 20),
        name="diff_attn_prompt",
    )(qd, kb, vb, da_lambda, onorm)


def _cross_attn_heads(q, mk, mv):
    dh = q.shape[-1] // MEM_HEADS
    outs = []
    for h in range(MEM_HEADS):
        sl = slice(h * dh, (h + 1) * dh)
        s = _dot_nt(q[:, sl], mk[:, sl])
        m = jnp.max(s, axis=-1, keepdims=True)
        p = jnp.exp(s - m)
        l = jnp.sum(p, axis=-1, keepdims=True)
        outs.append(_dot((p / l).astype(BF16), mv[:, sl]))
    return jnp.concatenate(outs, axis=-1)


def _out_cross_kernel(x_ref, mhg_ref, mda_ref, wo_ref, gq_ref, wq_ref, mk_ref, mv_ref,
                      wmo_ref, x2_ref):
    w = mhg_ref.shape[-1]
    x1 = x_ref[0] + _dot(mhg_ref[0], wo_ref[0:w, :]) + _dot(mda_ref[0], wo_ref[w:2 * w, :])
    hq = _rms(x1, gq_ref[...]).astype(BF16)
    dh = wq_ref.shape[1] // MEM_HEADS
    q = (_dot(hq, wq_ref[...]) * (float(dh) ** -0.5)).astype(BF16)
    o = _cross_attn_heads(q, mk_ref[0], mv_ref[0])
    x2_ref[0] = x1 + _dot(o.astype(BF16), wmo_ref[...])


def _out_cross_prompt(x, mhg, mda, w_out, gq, w_mq, mkb, mvb, w_mo, ts):
    B, S, D = x.shape
    W = mhg.shape[-1]
    N, MW = mkb.shape[1], mkb.shape[2]
    tok = lambda d: pl.BlockSpec((1, ts, d), lambda b, s: (b, s, 0))
    mem = pl.BlockSpec((1, N, MW), lambda b, s: (b, 0, 0))
    return pl.pallas_call(
        _out_cross_kernel,
        grid=(B, S // ts),
        in_specs=[tok(D), tok(W), tok(W), _const_spec(w_out.shape), _const_spec((1, D)),
                  _const_spec(w_mq.shape), mem, mem, _const_spec(w_mo.shape)],
        out_specs=tok(D),
        out_shape=jax.ShapeDtypeStruct((B, S, D), F32),
        compiler_params=_cparams(("arbitrary", "arbitrary"), 48 << 20),
        name="out_cross_prompt",
    )(x, mhg, mda, w_out, gq, w_mq, mkb, mvb, w_mo)


def _out_cross_sample_kernel(x_ref, mhg_ref, mda_ref, wo_ref, gq_ref, wq_ref, mk_ref, mv_ref,
                             wmo_ref, x2_ref, x1_s, q_s, o_s):
    b = pl.program_id(0)
    w = mhg_ref.shape[-1]

    @pl.when(b == 0)
    def _():
        x1 = x_ref[...] + _dot(mhg_ref[...], wo_ref[0:w, :]) + _dot(mda_ref[...], wo_ref[w:2 * w, :])
        x1_s[...] = x1
        hq = _rms(x1, gq_ref[...]).astype(BF16)
        dh = wq_ref.shape[1] // MEM_HEADS
        q_s[...] = _dot(hq, wq_ref[...]) * (float(dh) ** -0.5)

    q = q_s[pl.ds(b, 1), :].astype(BF16)
    o_s[pl.ds(b, 1), :] = _cross_attn_heads(q, mk_ref[0].astype(BF16), mv_ref[0].astype(BF16))

    @pl.when(b == pl.num_programs(0) - 1)
    def _():
        x2_ref[...] = x1_s[...] + _dot(o_s[...].astype(BF16), wmo_ref[...])


def _out_cross_sample(x, mhg, mda, w_out, gq, w_mq, mem_k, mem_v, w_mo):
    T, D = x.shape
    N, MW = mem_k.shape[1], mem_k.shape[2]
    mem = pl.BlockSpec((1, N, MW), lambda b: (b, 0, 0))
    return pl.pallas_call(
        _out_cross_sample_kernel,
        grid=(T,),
        in_specs=[_const_spec(x.shape), _const_spec(mhg.shape), _const_spec(mda.shape),
                  _const_spec(w_out.shape), _const_spec((1, D)), _const_spec(w_mq.shape),
                  mem, mem, _const_spec(w_mo.shape)],
        out_specs=pl.BlockSpec((T, D), lambda b: (0, 0)),
        out_shape=jax.ShapeDtypeStruct((T, D), F32),
        scratch_shapes=[pltpu.VMEM((T, D), F32), pltpu.VMEM((T, MW), F32), pltpu.VMEM((T, MW), F32)],
        compiler_params=_cparams(("arbitrary",), 32 << 20),
        name="out_cross_sample",
    )(x, mhg, mda, w_out, gq, w_mq, mem_k, mem_v, w_mo)


def _ffn_kernel(x_ref, gf_ref, wg_ref, wu_ref, wd_ref, gfin_ref, y_ref):
    x = x_ref[...]
    h = _rms(x, gf_ref[...]).astype(BF16)
    g = _dot(h, wg_ref[...])
    u = _dot(h, wu_ref[...])
    a = (g * _sigmoid(g) * u).astype(BF16)
    x3 = x + _dot(a, wd_ref[...])
    y_ref[...] = _rms(x3, gfin_ref[...])


def _ffn(x, gf, wg, wu, wd, gfin, ts):
    T, D = x.shape
    tok = pl.BlockSpec((ts, D), lambda i: (i, 0))
    return pl.pallas_call(
        _ffn_kernel,
        grid=(T // ts,),
        in_specs=[tok, _const_spec((1, D)), _const_spec(wg.shape), _const_spec(wu.shape),
                  _const_spec(wd.shape), _const_spec((1, D))],
        out_specs=tok,
        out_shape=jax.ShapeDtypeStruct((T, D), F32),
        compiler_params=_cparams(("arbitrary",), VMEM_LIMIT_CAP),
        name="ffn",
    )(x, gf, wg, wu, wd, gfin)


def _sample_mix_kernel(x_ref, g_ref, w_ref, lb_ref, cos_ref, sin_ref,
                       qd_ref, k_ref, v_ref, hq_ref, kk_ref, f_ref, vh_ref, gate_ref):
    h = _rms(x_ref[...], g_ref[...]).astype(BF16)
    z = _dot(h, w_ref[...])
    lb = _lower_bound(lb_ref)
    hq, k_hg, logf, hi, gate, dq, dk, dv = _mixer_sections(z, lb)
    cos = cos_ref[...]
    sin = sin_ref[...]
    qd_ref[...] = (_rope(dq, cos, sin) * (float(LANES // 2) ** -0.5)).astype(BF16)
    k_ref[...] = _rope(dk, cos, sin)
    v_ref[...] = dv
    hq_ref[...] = hq
    kk_ref[...] = k_hg
    f_ref[...] = jnp.exp(logf)
    vh_ref[...] = hi
    gate_ref[...] = gate


def _sample_mix(x, g, w_in, hg_lb, cos, sin):
    T, D = x.shape
    W = w_in.shape[1] // 7
    sds = jax.ShapeDtypeStruct
    return pl.pallas_call(
        _sample_mix_kernel,
        out_shape=[sds((T, W), BF16)] + [sds((T, W), F32)] * 7,
        compiler_params=_cparams(None, 32 << 20),
        name="sample_mix",
    )(x, g, w_in, hg_lb, cos, sin)


def _sample_hgrn_kernel(hq_ref, kk_ref, f_ref, vh_ref, gate_ref, onorm_ref, s0_ref,
                        mix_ref, s1_ref):
    nb = s0_ref.shape[0]
    g0 = pl.program_id(0) * nb
    W = hq_ref.shape[-1]

    def columns(ref, h):
        rows = ref[pl.ds(pl.multiple_of(g0, nb), nb), h * LANES:(h + 1) * LANES]
        pad = jnp.zeros((LANES - nb, LANES), F32)
        return jnp.concatenate([rows, pad], axis=0).T

    o_rows = []
    for h in range(HG_HEADS):
        q_t, k_t, f_t = columns(hq_ref, h), columns(kk_ref, h), columns(f_ref, h)
        v_rows = vh_ref[pl.ds(pl.multiple_of(g0, nb), nb), h * LANES:(h + 1) * LANES]
        o_h = []
        for j in range(nb):
            bcast = lambda tile: jnp.broadcast_to(tile[:, j:j + 1], (LANES, LANES))
            s_new = bcast(f_t) * s0_ref[j, h] + bcast(k_t) * v_rows[j:j + 1, :]
            s1_ref[j, h] = s_new
            o_h.append(jnp.sum(bcast(q_t) * s_new, axis=0, keepdims=True))
        o_rows.append(jnp.concatenate(o_h, axis=0))
    o = jnp.concatenate(o_rows, axis=-1)
    gate = gate_ref[pl.ds(pl.multiple_of(g0, nb), nb), :]
    mix_ref[...] = (_head_rms(o, onorm_ref[...]) * gate).astype(BF16)


def _sample_hgrn(hq, kk, f, vh, gate, onorm, s0, nb=8):
    T, W = hq.shape
    st = pl.BlockSpec((nb, HG_HEADS, LANES, LANES), lambda i: (i, 0, 0, 0))
    full = _const_spec((T, W))
    return pl.pallas_call(
        _sample_hgrn_kernel,
        grid=(T // nb,),
        in_specs=[full] * 5 + [_const_spec((1, LANES)), st],
        out_specs=[pl.BlockSpec((nb, W), lambda i: (i, 0)), st],
        out_shape=[jax.ShapeDtypeStruct((T, W), BF16), jax.ShapeDtypeStruct(s0.shape, F32)],
        compiler_params=_cparams(("arbitrary",), 32 << 20),
        name="sample_hgrn",
    )(hq, kk, f, vh, gate, onorm, s0)


def _paged_attn_kernel(pt_ref, q_ref, kn_ref, vn_ref, lam_ref, onorm_ref, *refs,
                       lam_init, n_pg):
    k_refs = refs[:n_pg]
    v_refs = refs[n_pg:2 * n_pg]
    o_ref = refs[2 * n_pg]
    wq_s, m_s, l_s, acc_s = refs[2 * n_pg + 1:]
    j = pl.program_id(1)
    half = LANES // 2

    @pl.when(j == 0)
    def _():
        row = lax.broadcasted_iota(jnp.int32, (8, LANES), 0)
        lane = lax.broadcasted_iota(jnp.int32, (8, LANES), 1)
        crow = lax.broadcasted_iota(jnp.int32, (8, 2 * LANES), 0)
        ccol = lax.broadcasted_iota(jnp.int32, (8, 2 * LANES), 1)
        sel = ((crow == 0) & (ccol < LANES)) | ((crow == 1) & (ccol >= LANES))
        e = jnp.where(sel, 1.0, 0.0).astype(BF16)
        for h in range(DA_HEADS):
            qh = jnp.broadcast_to(q_ref[0, :, h * LANES:(h + 1) * LANES].astype(F32), (8, LANES))
            in_map = ((row == 0) & (lane < half)) | ((row == 1) & (lane >= half))
            qmat = jnp.where(in_map, qh, 0.0).astype(BF16)
            wq = _dot_tn(qmat, e).astype(BF16)
            wq_s[h] = wq
            kn = kn_ref[0, :, h * LANES:(h + 1) * LANES].astype(BF16)
            s_self = _dot(jnp.broadcast_to(kn, (8, LANES)), wq)
            vn = jnp.broadcast_to(vn_ref[0, :, h * LANES:(h + 1) * LANES], (8, LANES))
            first = lax.broadcasted_iota(jnp.int32, (8, LANES), 0) == 0
            for mp in range(2):
                m_s[h, mp] = s_self[:, mp * LANES:(mp + 1) * LANES]
                l_s[h, mp] = jnp.where(first, 1.0, 0.0)
                acc_s[h, mp] = jnp.where(first, vn, 0.0)

    for h in range(DA_HEADS):
        sl = slice(h * LANES, (h + 1) * LANES)
        wq = wq_s[h]
        scores = [_dot(k_refs[p][0, :, sl].astype(BF16), wq) for p in range(n_pg)]
        for mp in range(2):
            cs = slice(mp * LANES, (mp + 1) * LANES)
            m_old = m_s[h, mp]
            blk_max = scores[0][:, cs]
            for p in range(1, n_pg):
                blk_max = jnp.maximum(blk_max, scores[p][:, cs])
            m_new = jnp.maximum(m_old, jnp.max(blk_max, axis=0, keepdims=True))
            alpha = jnp.exp(m_old - m_new)
            l_acc = alpha * l_s[h, mp]
            acc = alpha * acc_s[h, mp]
            m_row = m_new[0:1, :]
            for p in range(n_pg):
                pr = jnp.exp(scores[p][:, cs] - m_row)
                pv = pr * v_refs[p][0, :, sl]
                n8 = pr.shape[0] // 8
                l_acc = l_acc + jnp.sum(pr.reshape(n8, 8, LANES), axis=0)
                acc = acc + jnp.sum(pv.reshape(n8, 8, LANES), axis=0)
            m_s[h, mp] = m_new
            l_s[h, mp] = l_acc
            acc_s[h, mp] = acc

    @pl.when(j == pl.num_programs(1) - 1)
    def _():
        lam = _lambda(lam_ref, lam_init)
        outs = []
        for h in range(DA_HEADS):
            o = []
            for mp in range(2):
                l = jnp.sum(l_s[h, mp], axis=0, keepdims=True)
                a = jnp.sum(acc_s[h, mp], axis=0, keepdims=True)
                o.append(a / l)
            outs.append(_rms(o[0] - lam * o[1], onorm_ref[...]) * (1.0 - lam_init))
        o_ref[0] = jnp.concatenate(outs, axis=-1).astype(BF16)


def _paged_attn(page_table, qd, k_new, v_new, da_lambda, onorm, cache_k, cache_v, lam_init, n_pg):
    T, n_pages = page_table.shape
    n_phys, page, W = cache_k.shape
    row = pl.BlockSpec((1, 1, W), lambda b, j, pt: (b, 0, 0))

    def page_spec(p):
        return pl.BlockSpec((1, page, W), lambda b, j, pt: (pt[b, j * n_pg + p], 0, 0))

    const = lambda shape: pl.BlockSpec(shape, lambda b, j, pt: (0,) * len(shape))
    grid_spec = pltpu.PrefetchScalarGridSpec(
        num_scalar_prefetch=1,
        grid=(T, n_pages // n_pg),
        in_specs=[row, row, row, const(da_lambda.shape), const((1, LANES))]
        + [page_spec(p) for p in range(n_pg)] * 2,
        out_specs=row,
        scratch_shapes=[pltpu.VMEM((DA_HEADS, LANES, 2 * LANES), BF16),
                        pltpu.VMEM((DA_HEADS, 2, 8, LANES), F32),
                        pltpu.VMEM((DA_HEADS, 2, 8, LANES), F32),
                        pltpu.VMEM((DA_HEADS, 2, 8, LANES), F32)],
    )
    out = pl.pallas_call(
        functools.partial(_paged_attn_kernel, lam_init=lam_init, n_pg=n_pg),
        grid_spec=grid_spec,
        out_shape=jax.ShapeDtypeStruct((T, 1, W), BF16),
        compiler_params=_cparams(("arbitrary", "arbitrary"), 40 << 20),
        name="paged_diff_attn",
    )(page_table, qd.reshape(T, 1, W), k_new.reshape(T, 1, W), v_new.reshape(T, 1, W),
      da_lambda, onorm, *([cache_k] * n_pg), *([cache_v] * n_pg))
    return out.reshape(T, W)


def _rope_tables(pos, dh, n_maps):
    inv = ROPE_THETA ** (-jnp.arange(0, dh, 2, dtype=F32) / dh)
    ang = pos.astype(F32)[:, None] * inv[None, :]
    cos = jnp.cos(ang)
    sin = jnp.sin(ang)
    cos = jnp.tile(jnp.concatenate([cos, cos], axis=-1), (1, n_maps))
    sin = jnp.tile(jnp.concatenate([-sin, sin], axis=-1), (1, n_maps))
    return cos, sin


def _pick_tile(n, pref):
    t = min(n, pref)
    while n % t:
        t //= 2
    return t


def kernel(x_prompt, x_sample, mem_prompt, cache_k, cache_v, cache_mem_k, cache_mem_v, state_hgrn, page_table, norm_mix, w_in, hg_lb, hg_onorm, da_lambda, da_onorm, w_out, norm_mem_q, norm_mem_kv, w_mq, w_mk, w_mv, w_mo, norm_ffn, w_gate, w_up, w_down, norm_final):
    B, S, D = x_prompt.shape
    T = x_sample.shape[0]
    depth = w_in.shape[0]
    assert depth == 1 and x_sample.shape[1] == 1
    l = 0
    lam_init = 0.8 - 0.6 * math.exp(-0.3 * l)
    n_phys, page = cache_k.shape[1], cache_k.shape[2]
    past_len = page_table.shape[1] * page
    W = w_in.shape[2] // 7

    bf = lambda w: w.astype(BF16)
    w_in_b, w_out_b = bf(w_in[l]), bf(w_out[l])
    w_mq_b, w_mk_b, w_mv_b, w_mo_b = bf(w_mq[l]), bf(w_mk[l]), bf(w_mv[l]), bf(w_mo[l])
    w_gate_b, w_up_b, w_down_b = bf(w_gate[l]), bf(w_up[l]), bf(w_down[l])
    row = lambda g: g.reshape(1, -1)
    lb2 = hg_lb[l:l + 2]
    lam_p = da_lambda[l]

    cos_p, sin_p = _rope_tables(jnp.arange(S), LANES // 2, 2 * DA_HEADS)
    cos_s, sin_s = _rope_tables(past_len + jnp.arange(1), LANES // 2, 2 * DA_HEADS)
    cos_s = jnp.broadcast_to(cos_s, (T, W))
    sin_s = jnp.broadcast_to(sin_s, (T, W))

    mk, mv, mkb, mvb = _mem_kv(mem_prompt, row(norm_mem_kv[l]), w_mk_b, w_mv_b)
    qd, k_p, v_p, kb, vb, mix_hg, hs_p = _prompt_mix(
        x_prompt, row(norm_mix[l]), w_in_b, lb2, cos_p, sin_p, row(hg_onorm[l]),
        ts=_pick_tile(S, 256))
    mix_da = _diff_attn_prompt(qd, kb, vb, lam_p, row(da_onorm[l]), lam_init,
                               tq=_pick_tile(S, 256))
    x2 = _out_cross_prompt(x_prompt, mix_hg, mix_da, w_out_b, row(norm_mem_q[l]), w_mq_b,
                           mkb, mvb, w_mo_b, ts=_pick_tile(S, 512))
    y_p = _ffn(x2.reshape(B * S, D), row(norm_ffn[l]), w_gate_b, w_up_b, w_down_b,
               row(norm_final), ts=_pick_tile(B * S, 512)).reshape(B, S, D)

    xs = x_sample.reshape(T, D)
    qd_s, k_s, v_s, hq_s, kk_s, f_s, vh_s, gate_s = _sample_mix(
        xs, row(norm_mix[l]), w_in_b, lb2, cos_s, sin_s)
    mix_hg_s, hs_s = _sample_hgrn(hq_s, kk_s, f_s, vh_s, gate_s, row(hg_onorm[l]), state_hgrn[l])
    mix_da_s = _paged_attn(page_table, qd_s, k_s, v_s, lam_p, row(da_onorm[l]),
                           cache_k[l].reshape(n_phys, page, W), cache_v[l].reshape(n_phys, page, W),
                           lam_init, n_pg=_pick_tile(page_table.shape[1], 8))
    MW = w_mk.shape[2]
    x2_s = _out_cross_sample(xs, mix_hg_s, mix_da_s, w_out_b, row(norm_mem_q[l]), w_mq_b,
                             cache_mem_k[l].reshape(T, -1, MW), cache_mem_v[l].reshape(T, -1, MW),
                             w_mo_b)
    y_s = _ffn(x2_s, row(norm_ffn[l]), w_gate_b, w_up_b, w_down_b, row(norm_final), ts=T)

    dk = LANES
    return (y_p, y_s.reshape(T, 1, D),
            hs_p[None],
            k_p.reshape(1, B, S, DA_HEADS, dk), v_p.reshape(1, B, S, DA_HEADS, dk),
            mk.reshape(1, B, -1, MEM_HEADS, MW // MEM_HEADS),
            mv.reshape(1, B, -1, MEM_HEADS, MW // MEM_HEADS),
            hs_s[None],
            k_s.reshape(1, T, 1, DA_HEADS, dk), v_s.reshape(1, T, 1, DA_HEADS, dk))
```

```python
import functools
import math

import jax
import jax.numpy as jnp
from jax import lax
from jax.experimental import pallas as pl
from jax.experimental.pallas import tpu as pltpu

F32 = jnp.float32
BF16 = jnp.bfloat16
EPS = 1e-6
ROPE_THETA = 10000.0

HG_HEADS = 4
DA_HEADS = 4
MEM_HEADS = 4
HG_CHUNK = 64
LANES = 128
VMEM_LIMIT_CAP = 56 << 20

_NT = (((1,), (1,)), ((), ()))
_TN = (((0,), (0,)), ((), ()))


def _dot(a, b):
    return jnp.dot(a, b, preferred_element_type=F32)


def _dot_nt(a, b):
    return lax.dot_general(a, b, _NT, preferred_element_type=F32)


def _dot_tn(a, b):
    return lax.dot_general(a, b, _TN, preferred_element_type=F32)


def _rms(x, g):
    ms = jnp.mean(x * x, axis=-1, keepdims=True)
    return x * lax.rsqrt(ms + EPS) * g


def _sigmoid(x):
    return 1.0 / (1.0 + jnp.exp(-x))


def _cparams(semantics, vmem_bytes):
    return pltpu.CompilerParams(
        dimension_semantics=semantics,
        vmem_limit_bytes=int(min(max(vmem_bytes, 16 << 20), VMEM_LIMIT_CAP)))


def _const_spec(shape):
    nd = len(shape)
    return pl.BlockSpec(shape, lambda *_: (0,) * nd, pipeline_mode=pl.Buffered(1))


def _lower_bound(lb_ref):
    a0 = lb_ref[0:1, :]
    a1 = lb_ref[1:2, :]
    m = jnp.maximum(a0, a1)
    e0 = jnp.exp(a0 - m)
    e1 = jnp.exp(a1 - m)
    return e0 / (e0 + e1)


def _rope(x, cos, sin_signed):
    n = x.shape[-1]
    lane = lax.broadcasted_iota(jnp.int32, x.shape, x.ndim - 1)
    swapped = jnp.where((lane & 63) < 32,
                        pltpu.roll(x, n - 32, x.ndim - 1),
                        pltpu.roll(x, 32, x.ndim - 1))
    return x * cos + swapped * sin_signed


def _lambda(lam_ref, lam_init):
    lp = lam_ref[...]
    s01 = jnp.sum(lp[0:1, :] * lp[1:2, :], axis=-1, keepdims=True)
    s23 = jnp.sum(lp[2:3, :] * lp[3:4, :], axis=-1, keepdims=True)
    return jnp.exp(s01) - jnp.exp(s23) + lam_init


def _mixer_sections(z, lb):
    w = z.shape[-1] // 7
    hq, zf, hi, hg, dq, dk, dv = (z[:, i * w:(i + 1) * w] for i in range(7))
    sig = _sigmoid(zf)
    logf = jnp.log(lb + (1.0 - lb) * sig)
    k_hg = (1.0 - lb) * (1.0 - sig)
    gate = hg * _sigmoid(hg)
    return hq, k_hg, logf, hi, gate, dq, dk, dv


def _head_rms(o, g):
    outs = []
    for h in range(o.shape[-1] // LANES):
        oh = o[:, h * LANES:(h + 1) * LANES]
        outs.append(_rms(oh, g))
    return jnp.concatenate(outs, axis=-1)


def _memkv_kernel(mem_ref, g_ref, wk_ref, wv_ref, k_ref, v_ref, kb_ref, vb_ref):
    m = _rms(mem_ref[0], g_ref[...]).astype(BF16)
    k = _dot(m, wk_ref[...])
    v = _dot(m, wv_ref[...])
    k_ref[0] = k
    v_ref[0] = v
    kb_ref[0] = k.astype(BF16)
    vb_ref[0] = v.astype(BF16)


def _mem_kv(mem, g, wk, wv):
    B, N, D = mem.shape
    W = wk.shape[1]
    blk = lambda d: pl.BlockSpec((1, N, d), lambda b: (b, 0, 0))
    return pl.pallas_call(
        _memkv_kernel,
        grid=(B,),
        in_specs=[blk(D), _const_spec((1, D)), _const_spec((D, W)), _const_spec((D, W))],
        out_specs=[blk(W)] * 4,
        out_shape=[jax.ShapeDtypeStruct((B, N, W), F32)] * 2
        + [jax.ShapeDtypeStruct((B, N, W), BF16)] * 2,
        compiler_params=_cparams(("arbitrary",), 40 << 20),
        name="mem_kv",
    )(mem, g, wk, wv)


def _hgrn_chunk(q, kk, lf, v, st_ref):
    C, W = q.shape
    t = lax.broadcasted_iota(jnp.int32, (C, W), 0)
    ti = lax.broadcasted_iota(jnp.int32, (C, C), 0)
    si = lax.broadcasted_iota(jnp.int32, (C, C), 1)
    n_lvl = int(math.log2(C))

    c = lf
    q_lvls, k_lvls = [], []
    for l in range(n_lvl):
        m = 1 << l
        upper = (t & m) != 0
        if m < 8:
            y = c
            for i in range(l):
                y = jnp.where((t & (1 << i)) == 0, pltpu.roll(y, C - (1 << i), 0), y)
            bc = jnp.where(upper, pltpu.roll(y, m, 0), y)
        else:
            pieces = []
            for j in range(C // (2 * m)):
                r = 2 * m * j + m - 1
                pieces.append(jnp.broadcast_to(c[r:r + 1, :], (2 * m, W)))
            bc = pieces[0] if len(pieces) == 1 else jnp.concatenate(pieces, axis=0)
        e = jnp.exp(jnp.where(upper, c, bc - c))
        q_lvls.append(jnp.where(upper, q * e, 0.0).astype(BF16))
        k_lvls.append(jnp.where(upper, 0.0, kk * e).astype(BF16))
        c = c + jnp.where(upper, bc, 0.0)
    b = c
    b_last = b[C - 1:C, :]
    q_bf = q.astype(BF16)
    k_bf = kk.astype(BF16)
    v_bf = v.astype(BF16)
    q_state = (q * jnp.exp(b)).astype(BF16)
    k_state = (kk * jnp.exp(b_last - b)).astype(BF16)
    decay = jnp.exp(b_last)

    outs = []
    for h in range(W // LANES):
        sl = slice(h * LANES, (h + 1) * LANES)
        a = jnp.where(ti == si, _dot_nt(q_bf[:, sl], k_bf[:, sl]), 0.0)
        for l in range(n_lvl):
            same_seg = (ti >> (l + 1)) == (si >> (l + 1))
            a = a + jnp.where(same_seg, _dot_nt(q_lvls[l][:, sl], k_lvls[l][:, sl]), 0.0)
        st = st_ref[h]
        o = _dot(a.astype(BF16), v_bf[:, sl]) + _dot_nt(q_state[:, sl], st.astype(BF16))
        st_ref[h] = st * decay[:, sl] + _dot_tn(v_bf[:, sl], k_state[:, sl])
        outs.append(o)
    return jnp.concatenate(outs, axis=-1)


def _prompt_mix_kernel(x_ref, g_ref, w_ref, lb_ref, cos_ref, sin_ref, onorm_ref,
                       qd_ref, k_ref, v_ref, kb_ref, mix_ref, state_ref,
                       q_s, kk_s, lf_s, vh_s, o_s, st_s):
    s = pl.program_id(1)

    @pl.when(s == 0)
    def _():
        st_s[...] = jnp.zeros_like(st_s)

    h = _rms(x_ref[0], g_ref[...]).astype(BF16)
    z = _dot(h, w_ref[...])
    lb = _lower_bound(lb_ref)
    hq, k_hg, logf, hi, gate, dq, dk, dv = _mixer_sections(z, lb)

    cos = cos_ref[...]
    sin = sin_ref[...]
    q_da = _rope(dq, cos, sin)
    k_da = _rope(dk, cos, sin)
    qd_ref[0] = (q_da * (float(LANES // 2) ** -0.5)).astype(BF16)
    k_ref[0] = k_da
    kb_ref[0] = k_da.astype(BF16)
    v_ref[0] = dv

    q_s[...] = hq
    kk_s[...] = k_hg
    lf_s[...] = logf
    vh_s[...] = hi

    def chunk(ci, carry):
        rows = pl.ds(pl.multiple_of(ci * HG_CHUNK, HG_CHUNK), HG_CHUNK)
        o_s[rows, :] = _hgrn_chunk(q_s[rows, :], kk_s[rows, :], lf_s[rows, :],
                                   vh_s[rows, :], st_s)
        return carry

    lax.fori_loop(0, x_ref.shape[1] // HG_CHUNK, chunk, 0)

    mix_ref[0] = (_head_rms(o_s[...], onorm_ref[...]) * gate).astype(BF16)

    @pl.when(s == pl.num_programs(1) - 1)
    def _():
        for hh in range(HG_HEADS):
            state_ref[0, hh] = st_s[hh].T


def _prompt_mix(x, g, w_in, hg_lb, cos, sin, onorm, ts):
    B, S, D = x.shape
    W = w_in.shape[1] // 7
    tok = lambda d: pl.BlockSpec((1, ts, d), lambda b, s: (b, s, 0))
    tab = pl.BlockSpec((ts, W), lambda b, s: (s, 0))
    state = pl.BlockSpec((1, HG_HEADS, LANES, LANES), lambda b, s: (b, 0, 0, 0))
    sds = jax.ShapeDtypeStruct
    return pl.pallas_call(
        _prompt_mix_kernel,
        grid=(B, S // ts),
        in_specs=[tok(D), _const_spec((1, D)), _const_spec(w_in.shape),
                  _const_spec(hg_lb.shape), tab, tab, _const_spec((1, LANES))],
        out_specs=[tok(W)] * 5 + [state],
        out_shape=[sds((B, S, W), BF16), sds((B, S, W), F32), sds((B, S, W), F32),
                   sds((B, S, W), BF16), sds((B, S, W), BF16),
                   sds((B, HG_HEADS, LANES, LANES), F32)],
        scratch_shapes=[pltpu.VMEM((ts, W), F32)] * 5
        + [pltpu.VMEM((HG_HEADS, LANES, LANES), F32)],
        compiler_params=_cparams(("arbitrary", "arbitrary"), 48 << 20),
        name="prompt_mix",
    )(x, g, w_in, hg_lb, cos, sin, onorm)


def _diff_attn_kernel(q_ref, k_ref, v_ref, lam_ref, onorm_ref, o_ref,
                      qq_s, vt_s, m_s, l_s, acc_s, *, lam_init, tk):
    i = pl.program_id(2)
    tq = q_ref.shape[1]
    n_ck = tq // tk
    n_kv = v_ref.shape[1] // tk

    @pl.when(i == 0)
    def _():
        for jj in range(n_kv):
            vt_s[jj] = v_ref[0, jj * tk:(jj + 1) * tk, :].T.astype(BF16)

    q = q_ref[0]
    lane = lax.broadcasted_iota(jnp.int32, q.shape, 1)
    zero = jnp.zeros_like(q)
    qq_s[0:tq, :] = jnp.where(lane < LANES // 2, q, zero)
    qq_s[tq:2 * tq, :] = jnp.where(lane >= LANES // 2, q, zero)
    m_s[...] = jnp.full(m_s.shape, -jnp.inf, F32)
    l_s[...] = jnp.zeros(l_s.shape, F32)
    acc_s[...] = jnp.zeros(acc_s.shape, F32)

    def scores(j, c):
        k = k_ref[0, pl.ds(pl.multiple_of(j * tk, tk), tk), :]
        return _dot_nt(k, qq_s[c * tk:(c + 1) * tk, :])

    def update(j, c, tri, s):
        cols = slice(c * tk, (c + 1) * tk)
        if tri:
            key = lax.broadcasted_iota(jnp.int32, s.shape, 0)
            qry = lax.broadcasted_iota(jnp.int32, s.shape, 1)
            s = jnp.where(key <= qry, s, -jnp.inf)
        m_old = m_s[:, cols]
        m_new = jnp.maximum(m_old, jnp.max(s, axis=0, keepdims=True))
        alpha = jnp.exp(m_old - m_new)
        p = jnp.exp(s - m_new)
        l_s[:, cols] = alpha * l_s[:, cols] + jnp.sum(p, axis=0, keepdims=True)
        acc_s[:, cols] = alpha * acc_s[:, cols] + _dot(vt_s[j], p.astype(BF16))
        m_s[:, cols] = m_new

    def run(work):
        s = scores(*work[0][:2])
        for n, (j, c, tri) in enumerate(work):
            s_next = scores(*work[n + 1][:2]) if n + 1 < len(work) else None
            update(j, c, tri, s)
            s = s_next

    def full_block(j, carry):
        run([(j, c, False) for c in range(2 * n_ck)])
        return carry

    lax.fori_loop(0, i * n_ck, full_block, 0)
    run([(i * n_ck + d, mp * n_ck + cq, cq == d)
         for d in range(n_ck) for mp in range(2) for cq in range(d, n_ck)])

    lam = _lambda(lam_ref, lam_init)
    a = acc_s[...] * (1.0 / l_s[...])
    o_t = a[:, 0:tq] - lam * a[:, tq:2 * tq]
    ms = jnp.mean(o_t * o_t, axis=0, keepdims=True)
    y = (o_t * lax.rsqrt(ms + EPS)).T
    o_ref[0] = (y * onorm_ref[...] * (1.0 - lam_init)).astype(BF16)


def _diff_attn_prompt(qd, kb, v, da_lambda, onorm, lam_init, tq, tk):
    B, S, W = qd.shape
    H = W // LANES
    qspec = pl.BlockSpec((1, tq, LANES), lambda b, h, i: (b, i, h))
    kvspec = pl.BlockSpec((1, S, LANES), lambda b, h, i: (b, 0, h))
    return pl.pallas_call(
        functools.partial(_diff_attn_kernel, lam_init=lam_init, tk=tk),
        grid=(B, H, S // tq),
        in_specs=[qspec, kvspec, kvspec, _const_spec(da_lambda.shape), _const_spec((1, LANES))],
        out_specs=qspec,
        out_shape=jax.ShapeDtypeStruct((B, S, W), BF16),
        scratch_shapes=[pltpu.VMEM((2 * tq, LANES), BF16), pltpu.VMEM((S // tk, LANES, tk), BF16),
                        pltpu.VMEM((1, 2 * tq), F32), pltpu.VMEM((1, 2 * tq), F32),
                        pltpu.VMEM((LANES, 2 * tq), F32)],
        compiler_params=_cparams(("arbitrary",) * 3, 32 << 20),
        name="diff_attn_prompt",
    )(qd, kb, v, da_lambda, onorm)


def _cross_attn_heads(q, mk, mv):
    dh = q.shape[-1] // MEM_HEADS
    outs = []
    for h in range(MEM_HEADS):
        sl = slice(h * dh, (h + 1) * dh)
        s = _dot_nt(q[:, sl], mk[:, sl])
        m = jnp.max(s, axis=-1, keepdims=True)
        p = jnp.exp(s - m)
        l = jnp.sum(p, axis=-1, keepdims=True)
        outs.append(_dot((p / l).astype(BF16), mv[:, sl]))
    return jnp.concatenate(outs, axis=-1)


def _out_cross_kernel(x_ref, mhg_ref, mda_ref, wo_ref, gq_ref, wq_ref, mk_ref, mv_ref,
                      wmo_ref, x2_ref):
    w = mhg_ref.shape[-1]
    x1 = x_ref[0] + _dot(mhg_ref[0], wo_ref[0:w, :]) + _dot(mda_ref[0], wo_ref[w:2 * w, :])
    hq = _rms(x1, gq_ref[...]).astype(BF16)
    dh = wq_ref.shape[1] // MEM_HEADS
    q = (_dot(hq, wq_ref[...]) * (float(dh) ** -0.5)).astype(BF16)
    o = _cross_attn_heads(q, mk_ref[0], mv_ref[0])
    x2_ref[0] = x1 + _dot(o.astype(BF16), wmo_ref[...])


def _out_cross_prompt(x, mhg, mda, w_out, gq, w_mq, mkb, mvb, w_mo, ts):
    B, S, D = x.shape
    W = mhg.shape[-1]
    N, MW = mkb.shape[1], mkb.shape[2]
    tok = lambda d: pl.BlockSpec((1, ts, d), lambda b, s: (b, s, 0))
    mem = pl.BlockSpec((1, N, MW), lambda b, s: (b, 0, 0))
    return pl.pallas_call(
        _out_cross_kernel,
        grid=(B, S // ts),
        in_specs=[tok(D), tok(W), tok(W), _const_spec(w_out.shape), _const_spec((1, D)),
                  _const_spec(w_mq.shape), mem, mem, _const_spec(w_mo.shape)],
        out_specs=tok(D),
        out_shape=jax.ShapeDtypeStruct((B, S, D), F32),
        compiler_params=_cparams(("arbitrary", "arbitrary"), 48 << 20),
        name="out_cross_prompt",
    )(x, mhg, mda, w_out, gq, w_mq, mkb, mvb, w_mo)


def _out_cross_sample_kernel(x_ref, mhg_ref, mda_ref, wo_ref, gq_ref, wq_ref, mk_ref, mv_ref,
                             wmo_ref, x2_ref, x1_s, q_s, o_s):
    b = pl.program_id(0)
    w = mhg_ref.shape[-1]

    @pl.when(b == 0)
    def _():
        x1 = x_ref[...] + _dot(mhg_ref[...], wo_ref[0:w, :]) + _dot(mda_ref[...], wo_ref[w:2 * w, :])
        x1_s[...] = x1
        hq = _rms(x1, gq_ref[...]).astype(BF16)
        dh = wq_ref.shape[1] // MEM_HEADS
        q_s[...] = _dot(hq, wq_ref[...]) * (float(dh) ** -0.5)

    q = q_s[pl.ds(b, 1), :].astype(BF16)
    o_s[pl.ds(b, 1), :] = _cross_attn_heads(q, mk_ref[0].astype(BF16), mv_ref[0].astype(BF16))

    @pl.when(b == pl.num_programs(0) - 1)
    def _():
        x2_ref[...] = x1_s[...] + _dot(o_s[...].astype(BF16), wmo_ref[...])


def _out_cross_sample(x, mhg, mda, w_out, gq, w_mq, mem_k, mem_v, w_mo):
    T, D = x.shape
    N, MW = mem_k.shape[1], mem_k.shape[2]
    mem = pl.BlockSpec((1, N, MW), lambda b: (b, 0, 0))
    return pl.pallas_call(
        _out_cross_sample_kernel,
        grid=(T,),
        in_specs=[_const_spec(x.shape), _const_spec(mhg.shape), _const_spec(mda.shape),
                  _const_spec(w_out.shape), _const_spec((1, D)), _const_spec(w_mq.shape),
                  mem, mem, _const_spec(w_mo.shape)],
        out_specs=pl.BlockSpec((T, D), lambda b: (0, 0)),
        out_shape=jax.ShapeDtypeStruct((T, D), F32),
        scratch_shapes=[pltpu.VMEM((T, D), F32), pltpu.VMEM((T, MW), F32), pltpu.VMEM((T, MW), F32)],
        compiler_params=_cparams(("arbitrary",), 32 << 20),
        name="out_cross_sample",
    )(x, mhg, mda, w_out, gq, w_mq, mem_k, mem_v, w_mo)


def _ffn_kernel(x_ref, gf_ref, wg_ref, wu_ref, wd_ref, gfin_ref, y_ref):
    x = x_ref[...]
    h = _rms(x, gf_ref[...]).astype(BF16)
    g = _dot(h, wg_ref[...])
    u = _dot(h, wu_ref[...])
    a = (g * _sigmoid(g) * u).astype(BF16)
    x3 = x + _dot(a, wd_ref[...])
    y_ref[...] = _rms(x3, gfin_ref[...])


def _ffn(x, gf, wg, wu, wd, gfin, ts):
    T, D = x.shape
    tok = pl.BlockSpec((ts, D), lambda i: (i, 0))
    return pl.pallas_call(
        _ffn_kernel,
        grid=(T // ts,),
        in_specs=[tok, _const_spec((1, D)), _const_spec(wg.shape), _const_spec(wu.shape),
                  _const_spec(wd.shape), _const_spec((1, D))],
        out_specs=tok,
        out_shape=jax.ShapeDtypeStruct((T, D), F32),
        compiler_params=_cparams(("arbitrary",), VMEM_LIMIT_CAP),
        name="ffn",
    )(x, gf, wg, wu, wd, gfin)


def _sample_mix_kernel(x_ref, g_ref, w_ref, lb_ref, cos_ref, sin_ref,
                       qd_ref, k_ref, v_ref, hq_ref, kk_ref, f_ref, vh_ref, gate_ref):
    h = _rms(x_ref[...], g_ref[...]).astype(BF16)
    z = _dot(h, w_ref[...])
    lb = _lower_bound(lb_ref)
    hq, k_hg, logf, hi, gate, dq, dk, dv = _mixer_sections(z, lb)
    cos = cos_ref[...]
    sin = sin_ref[...]
    qd_ref[...] = _rope(dq, cos, sin) * (float(LANES // 2) ** -0.5)
    k_ref[...] = _rope(dk, cos, sin)
    v_ref[...] = dv
    hq_ref[...] = hq
    kk_ref[...] = k_hg
    f_ref[...] = jnp.exp(logf)
    vh_ref[...] = hi
    gate_ref[...] = gate


def _sample_mix(x, g, w_in, hg_lb, cos, sin):
    T, D = x.shape
    W = w_in.shape[1] // 7
    sds = jax.ShapeDtypeStruct
    return pl.pallas_call(
        _sample_mix_kernel,
        out_shape=[sds((T, W), F32)] * 8,
        compiler_params=_cparams(None, 32 << 20),
        name="sample_mix",
    )(x, g, w_in, hg_lb, cos, sin)


def _sample_hgrn_kernel(hq_ref, kk_ref, f_ref, vh_ref, gate_ref, onorm_ref, s0_ref,
                        mix_ref, s1_ref):
    nb = s0_ref.shape[0]
    g0 = pl.program_id(0) * nb
    W = hq_ref.shape[-1]

    def columns(ref, h):
        rows = ref[pl.ds(pl.multiple_of(g0, nb), nb), h * LANES:(h + 1) * LANES]
        pad = jnp.zeros((LANES - nb, LANES), F32)
        return jnp.concatenate([rows, pad], axis=0).T

    o_rows = []
    for h in range(HG_HEADS):
        q_t, k_t, f_t = columns(hq_ref, h), columns(kk_ref, h), columns(f_ref, h)
        v_rows = vh_ref[pl.ds(pl.multiple_of(g0, nb), nb), h * LANES:(h + 1) * LANES]
        o_h = []
        for j in range(nb):
            bcast = lambda tile: jnp.broadcast_to(tile[:, j:j + 1], (LANES, LANES))
            s_new = bcast(f_t) * s0_ref[j, h] + bcast(k_t) * v_rows[j:j + 1, :]
            s1_ref[j, h] = s_new
            o_h.append(jnp.sum(bcast(q_t) * s_new, axis=0, keepdims=True))
        o_rows.append(jnp.concatenate(o_h, axis=0))
    o = jnp.concatenate(o_rows, axis=-1)
    gate = gate_ref[pl.ds(pl.multiple_of(g0, nb), nb), :]
    mix_ref[...] = (_head_rms(o, onorm_ref[...]) * gate).astype(BF16)


def _sample_hgrn(hq, kk, f, vh, gate, onorm, s0, nb=8):
    T, W = hq.shape
    st = pl.BlockSpec((nb, HG_HEADS, LANES, LANES), lambda i: (i, 0, 0, 0))
    full = _const_spec((T, W))
    return pl.pallas_call(
        _sample_hgrn_kernel,
        grid=(T // nb,),
        in_specs=[full] * 5 + [_const_spec((1, LANES)), st],
        out_specs=[pl.BlockSpec((nb, W), lambda i: (i, 0)), st],
        out_shape=[jax.ShapeDtypeStruct((T, W), BF16), jax.ShapeDtypeStruct(s0.shape, F32)],
        compiler_params=_cparams(("arbitrary",), 32 << 20),
        name="sample_hgrn",
    )(hq, kk, f, vh, gate, onorm, s0)


def _paged_attn_kernel(pt_ref, q_ref, kn_ref, vn_ref, lam_ref, onorm_ref, *refs,
                       lam_init, n_pg):
    k_refs = refs[:n_pg]
    v_refs = refs[n_pg:2 * n_pg]
    o_ref = refs[2 * n_pg]
    q8_s, m_s, l_s, acc_s = refs[2 * n_pg + 1:]
    j = pl.program_id(1)
    half = LANES // 2
    H = DA_HEADS
    reps = 8 // H

    drow = lax.broadcasted_iota(jnp.int32, (LANES, 2 * LANES), 0)
    dcol = lax.broadcasted_iota(jnp.int32, (LANES, 2 * LANES), 1)
    ones2 = jnp.where((drow // half) == (dcol // LANES), 1.0, 0.0).astype(BF16)

    def head_rows(ref):
        rows = [ref[0, :, h * LANES:(h + 1) * LANES].astype(F32) for h in range(H)]
        return jnp.concatenate(rows * reps, axis=0)

    @pl.when(j == 0)
    def _():
        q8 = head_rows(q_ref)
        q8_s[...] = q8
        s_self = _dot((head_rows(kn_ref) * q8).astype(BF16), ones2)
        first = lax.broadcasted_iota(jnp.int32, (8, LANES), 0) < H
        vn = head_rows(vn_ref)
        for mp in range(2):
            m_s[mp] = s_self[:, mp * LANES:(mp + 1) * LANES]
            l_s[mp] = jnp.where(first, 1.0, 0.0)
            acc_s[mp] = jnp.where(first, vn, 0.0)

    q8 = q8_s[...]
    scores = []
    for p in range(n_pg):
        k = k_refs[p][0]
        n8 = k.shape[0] // 8
        kq = (k.reshape(n8, 8, LANES) * q8[None]).reshape(k.shape)
        scores.append(_dot(kq.astype(BF16), ones2))
    for mp in range(2):
        cs = slice(mp * LANES, (mp + 1) * LANES)
        m_old = m_s[mp]
        m_new = m_old
        for p in range(n_pg):
            m_new = jnp.maximum(m_new, jnp.max(scores[p][:, cs].reshape(n8, 8, LANES), axis=0))
        alpha = jnp.exp(m_old - m_new)
        l_acc = alpha * l_s[mp]
        acc = alpha * acc_s[mp]
        for p in range(n_pg):
            pr = jnp.exp(scores[p][:, cs].reshape(n8, 8, LANES) - m_new[None])
            l_acc = l_acc + jnp.sum(pr, axis=0)
            acc = acc + jnp.sum(pr * v_refs[p][0].reshape(n8, 8, LANES), axis=0)
        m_s[mp] = m_new
        l_s[mp] = l_acc
        acc_s[mp] = acc

    @pl.when(j == pl.num_programs(1) - 1)
    def _():
        lam = _lambda(lam_ref, lam_init)
        o = []
        for mp in range(2):
            m = m_s[mp]
            m_all = m
            for r in range(1, reps):
                m_all = jnp.maximum(m_all, pltpu.roll(m, r * H, 0))
            w = jnp.exp(m - m_all)
            lw = l_s[mp] * w
            aw = acc_s[mp] * w
            l_all, a_all = lw, aw
            for r in range(1, reps):
                l_all = l_all + pltpu.roll(lw, r * H, 0)
                a_all = a_all + pltpu.roll(aw, r * H, 0)
            o.append(a_all / l_all)
        y = _rms(o[0] - lam * o[1], onorm_ref[...]) * (1.0 - lam_init)
        o_ref[0] = jnp.concatenate([y[h:h + 1, :] for h in range(H)], axis=-1).astype(BF16)


def _paged_attn(page_table, qd, k_new, v_new, da_lambda, onorm, cache_k, cache_v, lam_init, n_pg):
    T, n_pages = page_table.shape
    n_phys, rows, dk = cache_k.shape
    W = qd.shape[-1]
    row = pl.BlockSpec((1, 1, W), lambda b, j, pt: (b, 0, 0))

    def page_spec(p):
        return pl.BlockSpec((1, rows, dk), lambda b, j, pt: (pt[b, j * n_pg + p], 0, 0))

    const = lambda shape: pl.BlockSpec(shape, lambda b, j, pt: (0,) * len(shape))
    grid_spec = pltpu.PrefetchScalarGridSpec(
        num_scalar_prefetch=1,
        grid=(T, n_pages // n_pg),
        in_specs=[row, row, row, const(da_lambda.shape), const((1, LANES))]
        + [page_spec(p) for p in range(n_pg)] * 2,
        out_specs=row,
        scratch_shapes=[pltpu.VMEM((8, LANES), F32),
                        pltpu.VMEM((2, 8, LANES), F32),
                        pltpu.VMEM((2, 8, LANES), F32),
                        pltpu.VMEM((2, 8, LANES), F32)],
    )
    out = pl.pallas_call(
        functools.partial(_paged_attn_kernel, lam_init=lam_init, n_pg=n_pg),
        grid_spec=grid_spec,
        out_shape=jax.ShapeDtypeStruct((T, 1, W), BF16),
        compiler_params=_cparams(("arbitrary", "arbitrary"), 48 << 20),
        name="paged_diff_attn",
    )(page_table, qd.reshape(T, 1, W), k_new.reshape(T, 1, W), v_new.reshape(T, 1, W),
      da_lambda, onorm, *([cache_k] * n_pg), *([cache_v] * n_pg))
    return out.reshape(T, W)


def _rope_tables(pos, dh, n_maps):
    inv = ROPE_THETA ** (-jnp.arange(0, dh, 2, dtype=F32) / dh)
    ang = pos.astype(F32)[:, None] * inv[None, :]
    cos = jnp.cos(ang)
    sin = jnp.sin(ang)
    cos = jnp.tile(jnp.concatenate([cos, cos], axis=-1), (1, n_maps))
    sin = jnp.tile(jnp.concatenate([-sin, sin], axis=-1), (1, n_maps))
    return cos, sin


def _pick_tile(n, pref):
    t = min(n, pref)
    while n % t:
        t //= 2
    return t


def kernel(x_prompt, x_sample, mem_prompt, cache_k, cache_v, cache_mem_k, cache_mem_v, state_hgrn, page_table, norm_mix, w_in, hg_lb, hg_onorm, da_lambda, da_onorm, w_out, norm_mem_q, norm_mem_kv, w_mq, w_mk, w_mv, w_mo, norm_ffn, w_gate, w_up, w_down, norm_final):
    B, S, D = x_prompt.shape
    T = x_sample.shape[0]
    depth = w_in.shape[0]
    assert depth == 1 and x_sample.shape[1] == 1
    l = 0
    lam_init = 0.8 - 0.6 * math.exp(-0.3 * l)
    n_phys, page = cache_k.shape[1], cache_k.shape[2]
    past_len = page_table.shape[1] * page
    W = w_in.shape[2] // 7

    bf = lambda w: w.astype(BF16)
    w_in_b, w_out_b = bf(w_in[l]), bf(w_out[l])
    w_mq_b, w_mk_b, w_mv_b, w_mo_b = bf(w_mq[l]), bf(w_mk[l]), bf(w_mv[l]), bf(w_mo[l])
    w_gate_b, w_up_b, w_down_b = bf(w_gate[l]), bf(w_up[l]), bf(w_down[l])
    row = lambda g: g.reshape(1, -1)
    lb2 = hg_lb[l:l + 2]
    lam_p = da_lambda[l]

    cos_p, sin_p = _rope_tables(jnp.arange(S), LANES // 2, 2 * DA_HEADS)
    cos_s, sin_s = _rope_tables(past_len + jnp.arange(1), LANES // 2, 2 * DA_HEADS)
    cos_s = jnp.broadcast_to(cos_s, (T, W))
    sin_s = jnp.broadcast_to(sin_s, (T, W))

    mk, mv, mkb, mvb = _mem_kv(mem_prompt, row(norm_mem_kv[l]), w_mk_b, w_mv_b)
    qd, k_p, v_p, kb, mix_hg, hs_p = _prompt_mix(
        x_prompt, row(norm_mix[l]), w_in_b, lb2, cos_p, sin_p, row(hg_onorm[l]),
        ts=_pick_tile(S, 256))
    mix_da = _diff_attn_prompt(qd, kb, v_p, lam_p, row(da_onorm[l]), lam_init,
                               tq=_pick_tile(S, 1024), tk=_pick_tile(S, 256))
    x2 = _out_cross_prompt(x_prompt, mix_hg, mix_da, w_out_b, row(norm_mem_q[l]), w_mq_b,
                           mkb, mvb, w_mo_b, ts=_pick_tile(S, 512))
    y_p = _ffn(x2.reshape(B * S, D), row(norm_ffn[l]), w_gate_b, w_up_b, w_down_b,
               row(norm_final), ts=_pick_tile(B * S, 512)).reshape(B, S, D)

    xs = x_sample.reshape(T, D)
    qd_s, k_s, v_s, hq_s, kk_s, f_s, vh_s, gate_s = _sample_mix(
        xs, row(norm_mix[l]), w_in_b, lb2, cos_s, sin_s)
    mix_hg_s, hs_s = _sample_hgrn(hq_s, kk_s, f_s, vh_s, gate_s, row(hg_onorm[l]), state_hgrn[l])
    mix_da_s = _paged_attn(page_table, qd_s, k_s, v_s, lam_p, row(da_onorm[l]),
                           cache_k.reshape(depth * n_phys, page * DA_HEADS, LANES),
                           cache_v.reshape(depth * n_phys, page * DA_HEADS, LANES),
                           lam_init, n_pg=_pick_tile(page_table.shape[1], 16))
    MW = w_mk.shape[2]
    x2_s = _out_cross_sample(xs, mix_hg_s, mix_da_s, w_out_b, row(norm_mem_q[l]), w_mq_b,
                             cache_mem_k[l].reshape(T, -1, MW), cache_mem_v[l].reshape(T, -1, MW),
                             w_mo_b)
    y_s = _ffn(x2_s, row(norm_ffn[l]), w_gate_b, w_up_b, w_down_b, row(norm_final), ts=T)

    dk = LANES
    return (y_p, y_s.reshape(T, 1, D),
            hs_p[None],
            k_p.reshape(1, B, S, DA_HEADS, dk), v_p.reshape(1, B, S, DA_HEADS, dk),
            mk.reshape(1, B, -1, MEM_HEADS, MW // MEM_HEADS),
            mv.reshape(1, B, -1, MEM_HEADS, MW // MEM_HEADS),
            hs_s[None],
            k_s.reshape(1, T, 1, DA_HEADS, dk), v_s.reshape(1, T, 1, DA_HEADS, dk))
```

```python
import functools
import math

import jax
import jax.numpy as jnp
from jax import lax
from jax.experimental import pallas as pl
from jax.experimental.pallas import tpu as pltpu

F32 = jnp.float32
BF16 = jnp.bfloat16
EPS = 1e-6
ROPE_THETA = 10000.0

HG_HEADS = 4
DA_HEADS = 4
MEM_HEADS = 4
HG_CHUNK = 64
LANES = 128
VMEM_LIMIT_CAP = 56 << 20

_NT = (((1,), (1,)), ((), ()))
_TN = (((0,), (0,)), ((), ()))


def _dot(a, b):
    return jnp.dot(a, b, preferred_element_type=F32)


def _dot_nt(a, b):
    return lax.dot_general(a, b, _NT, preferred_element_type=F32)


def _dot_tn(a, b):
    return lax.dot_general(a, b, _TN, preferred_element_type=F32)


def _rms(x, g):
    ms = jnp.mean(x * x, axis=-1, keepdims=True)
    return x * lax.rsqrt(ms + EPS) * g


def _sigmoid(x):
    return 1.0 / (1.0 + jnp.exp(-x))


def _cparams(semantics, vmem_bytes):
    return pltpu.CompilerParams(
        dimension_semantics=semantics,
        vmem_limit_bytes=int(min(max(vmem_bytes, 16 << 20), VMEM_LIMIT_CAP)))


def _const_spec(shape):
    nd = len(shape)
    return pl.BlockSpec(shape, lambda *_: (0,) * nd, pipeline_mode=pl.Buffered(1))


def _lower_bound(lb_ref):
    a0 = lb_ref[0:1, :]
    a1 = lb_ref[1:2, :]
    m = jnp.maximum(a0, a1)
    e0 = jnp.exp(a0 - m)
    e1 = jnp.exp(a1 - m)
    return e0 / (e0 + e1)


def _rope(x, cos, sin_signed):
    n = x.shape[-1]
    lane = lax.broadcasted_iota(jnp.int32, x.shape, x.ndim - 1)
    swapped = jnp.where((lane & 63) < 32,
                        pltpu.roll(x, n - 32, x.ndim - 1),
                        pltpu.roll(x, 32, x.ndim - 1))
    return x * cos + swapped * sin_signed


def _lambda(lam_ref, lam_init):
    lp = lam_ref[...]
    s01 = jnp.sum(lp[0:1, :] * lp[1:2, :], axis=-1, keepdims=True)
    s23 = jnp.sum(lp[2:3, :] * lp[3:4, :], axis=-1, keepdims=True)
    return jnp.exp(s01) - jnp.exp(s23) + lam_init


def _mixer_sections(z, lb):
    w = z.shape[-1] // 7
    hq, zf, hi, hg, dq, dk, dv = (z[:, i * w:(i + 1) * w] for i in range(7))
    sig = _sigmoid(zf)
    logf = jnp.log(lb + (1.0 - lb) * sig)
    k_hg = (1.0 - lb) * (1.0 - sig)
    gate = hg * _sigmoid(hg)
    return hq, k_hg, logf, hi, gate, dq, dk, dv


def _head_rms(o, g):
    outs = []
    for h in range(o.shape[-1] // LANES):
        oh = o[:, h * LANES:(h + 1) * LANES]
        outs.append(_rms(oh, g))
    return jnp.concatenate(outs, axis=-1)


def _memkv_kernel(mem_ref, g_ref, wk_ref, wv_ref, k_ref, v_ref, kb_ref, vb_ref):
    m = _rms(mem_ref[0], g_ref[...]).astype(BF16)
    k = _dot(m, wk_ref[...])
    v = _dot(m, wv_ref[...])
    k_ref[0] = k
    v_ref[0] = v
    kb_ref[0] = k.astype(BF16)
    vb_ref[0] = v.astype(BF16)


def _mem_kv(mem, g, wk, wv):
    B, N, D = mem.shape
    W = wk.shape[1]
    blk = lambda d: pl.BlockSpec((1, N, d), lambda b: (b, 0, 0))
    return pl.pallas_call(
        _memkv_kernel,
        grid=(B,),
        in_specs=[blk(D), _const_spec((1, D)), _const_spec((D, W)), _const_spec((D, W))],
        out_specs=[blk(W)] * 4,
        out_shape=[jax.ShapeDtypeStruct((B, N, W), F32)] * 2
        + [jax.ShapeDtypeStruct((B, N, W), BF16)] * 2,
        compiler_params=_cparams(("arbitrary",), 40 << 20),
        name="mem_kv",
    )(mem, g, wk, wv)


def _hgrn_chunk(q, kk, lf, v, st_ref):
    C, W = q.shape
    t = lax.broadcasted_iota(jnp.int32, (C, W), 0)
    ti = lax.broadcasted_iota(jnp.int32, (C, C), 0)
    si = lax.broadcasted_iota(jnp.int32, (C, C), 1)
    n_lvl = int(math.log2(C))

    c = lf
    q_lvls, k_lvls = [], []
    for l in range(n_lvl):
        m = 1 << l
        upper = (t & m) != 0
        if m < 8:
            y = c
            for i in range(l):
                y = jnp.where((t & (1 << i)) == 0, pltpu.roll(y, C - (1 << i), 0), y)
            bc = jnp.where(upper, pltpu.roll(y, m, 0), y)
        else:
            pieces = []
            for j in range(C // (2 * m)):
                r = 2 * m * j + m - 1
                pieces.append(jnp.broadcast_to(c[r:r + 1, :], (2 * m, W)))
            bc = pieces[0] if len(pieces) == 1 else jnp.concatenate(pieces, axis=0)
        e = jnp.exp(jnp.where(upper, c, bc - c))
        q_lvls.append(jnp.where(upper, q * e, 0.0).astype(BF16))
        k_lvls.append(jnp.where(upper, 0.0, kk * e).astype(BF16))
        c = c + jnp.where(upper, bc, 0.0)
    b = c
    b_last = b[C - 1:C, :]
    q_bf = q.astype(BF16)
    k_bf = kk.astype(BF16)
    v_bf = v.astype(BF16)
    q_state = (q * jnp.exp(b)).astype(BF16)
    k_state = (kk * jnp.exp(b_last - b)).astype(BF16)
    decay = jnp.exp(b_last)

    outs = []
    for h in range(W // LANES):
        sl = slice(h * LANES, (h + 1) * LANES)
        a = jnp.where(ti == si, _dot_nt(q_bf[:, sl], k_bf[:, sl]), 0.0)
        for l in range(n_lvl):
            same_seg = (ti >> (l + 1)) == (si >> (l + 1))
            a = a + jnp.where(same_seg, _dot_nt(q_lvls[l][:, sl], k_lvls[l][:, sl]), 0.0)
        st = st_ref[h]
        o = _dot(a.astype(BF16), v_bf[:, sl]) + _dot_nt(q_state[:, sl], st.astype(BF16))
        st_ref[h] = st * decay[:, sl] + _dot_tn(v_bf[:, sl], k_state[:, sl])
        outs.append(o)
    return jnp.concatenate(outs, axis=-1)


def _prompt_mix_kernel(x_ref, g_ref, w_ref, lb_ref, cos_ref, sin_ref, onorm_ref,
                       qd_ref, k_ref, v_ref, kb_ref, mix_ref, state_ref,
                       q_s, kk_s, lf_s, vh_s, o_s, st_s):
    s = pl.program_id(1)

    @pl.when(s == 0)
    def _():
        st_s[...] = jnp.zeros_like(st_s)

    h = _rms(x_ref[0], g_ref[...]).astype(BF16)
    z = _dot(h, w_ref[...])
    lb = _lower_bound(lb_ref)
    hq, k_hg, logf, hi, gate, dq, dk, dv = _mixer_sections(z, lb)

    cos = cos_ref[...]
    sin = sin_ref[...]
    q_da = _rope(dq, cos, sin)
    k_da = _rope(dk, cos, sin)
    qd_ref[0] = (q_da * (float(LANES // 2) ** -0.5)).astype(BF16)
    k_ref[0] = k_da
    kb_ref[0] = k_da.astype(BF16)
    v_ref[0] = dv

    q_s[...] = hq
    kk_s[...] = k_hg
    lf_s[...] = logf
    vh_s[...] = hi

    def chunk(ci, carry):
        rows = pl.ds(pl.multiple_of(ci * HG_CHUNK, HG_CHUNK), HG_CHUNK)
        o_s[rows, :] = _hgrn_chunk(q_s[rows, :], kk_s[rows, :], lf_s[rows, :],
                                   vh_s[rows, :], st_s)
        return carry

    lax.fori_loop(0, x_ref.shape[1] // HG_CHUNK, chunk, 0)

    mix_ref[0] = (_head_rms(o_s[...], onorm_ref[...]) * gate).astype(BF16)

    @pl.when(s == pl.num_programs(1) - 1)
    def _():
        for hh in range(HG_HEADS):
            state_ref[0, hh] = st_s[hh].T


def _prompt_mix(x, g, w_in, hg_lb, cos, sin, onorm, ts):
    B, S, D = x.shape
    W = w_in.shape[1] // 7
    tok = lambda d: pl.BlockSpec((1, ts, d), lambda b, s: (b, s, 0))
    tab = pl.BlockSpec((ts, W), lambda b, s: (s, 0))
    state = pl.BlockSpec((1, HG_HEADS, LANES, LANES), lambda b, s: (b, 0, 0, 0))
    sds = jax.ShapeDtypeStruct
    return pl.pallas_call(
        _prompt_mix_kernel,
        grid=(B, S // ts),
        in_specs=[tok(D), _const_spec((1, D)), _const_spec(w_in.shape),
                  _const_spec(hg_lb.shape), tab, tab, _const_spec((1, LANES))],
        out_specs=[tok(W)] * 5 + [state],
        out_shape=[sds((B, S, W), BF16), sds((B, S, W), F32), sds((B, S, W), F32),
                   sds((B, S, W), BF16), sds((B, S, W), BF16),
                   sds((B, HG_HEADS, LANES, LANES), F32)],
        scratch_shapes=[pltpu.VMEM((ts, W), F32)] * 5
        + [pltpu.VMEM((HG_HEADS, LANES, LANES), F32)],
        compiler_params=_cparams(("arbitrary", "arbitrary"), 48 << 20),
        name="prompt_mix",
    )(x, g, w_in, hg_lb, cos, sin, onorm)


def _diff_attn_kernel(q_ref, k_ref, v_ref, lam_ref, onorm_ref, o_ref,
                      qq_s, vt_s, m_s, l_s, acc_s, *, lam_init, tk):
    i = pl.program_id(2)
    tq = q_ref.shape[1]
    n_ck = tq // tk
    n_kv = v_ref.shape[1] // tk

    @pl.when(i == 0)
    def _():
        for jj in range(n_kv):
            vt_s[jj] = v_ref[0, jj * tk:(jj + 1) * tk, :].T.astype(BF16)

    q = q_ref[0]
    lane = lax.broadcasted_iota(jnp.int32, q.shape, 1)
    zero = jnp.zeros_like(q)
    qq_s[0:tq, :] = jnp.where(lane < LANES // 2, q, zero)
    qq_s[tq:2 * tq, :] = jnp.where(lane >= LANES // 2, q, zero)
    m_s[...] = jnp.full(m_s.shape, -jnp.inf, F32)
    l_s[...] = jnp.zeros(l_s.shape, F32)
    acc_s[...] = jnp.zeros(acc_s.shape, F32)

    def scores(j, c):
        k = k_ref[0, pl.ds(pl.multiple_of(j * tk, tk), tk), :]
        return _dot_nt(k, qq_s[c * tk:(c + 1) * tk, :])

    def update(j, c, tri, s):
        cols = slice(c * tk, (c + 1) * tk)
        if tri:
            key = lax.broadcasted_iota(jnp.int32, s.shape, 0)
            qry = lax.broadcasted_iota(jnp.int32, s.shape, 1)
            s = jnp.where(key <= qry, s, -jnp.inf)
        m_old = m_s[:, cols]
        m_new = jnp.maximum(m_old, jnp.max(s, axis=0, keepdims=True))
        alpha = jnp.exp(m_old - m_new)
        p = jnp.exp(s - m_new)
        l_s[:, cols] = alpha * l_s[:, cols] + jnp.sum(p, axis=0, keepdims=True)
        acc_s[:, cols] = alpha * acc_s[:, cols] + _dot(vt_s[j], p.astype(BF16))
        m_s[:, cols] = m_new

    def run(work):
        s = scores(*work[0][:2])
        for n, (j, c, tri) in enumerate(work):
            s_next = scores(*work[n + 1][:2]) if n + 1 < len(work) else None
            update(j, c, tri, s)
            s = s_next

    def full_block(j, carry):
        run([(j, c, False) for c in range(2 * n_ck)])
        return carry

    lax.fori_loop(0, i * n_ck, full_block, 0)
    run([(i * n_ck + d, mp * n_ck + cq, cq == d)
         for d in range(n_ck) for mp in range(2) for cq in range(d, n_ck)])

    lam = _lambda(lam_ref, lam_init)
    a = acc_s[...] * (1.0 / l_s[...])
    o_t = a[:, 0:tq] - lam * a[:, tq:2 * tq]
    ms = jnp.mean(o_t * o_t, axis=0, keepdims=True)
    y = (o_t * lax.rsqrt(ms + EPS)).T
    o_ref[0] = (y * onorm_ref[...] * (1.0 - lam_init)).astype(BF16)


def _diff_attn_prompt(qd, kb, v, da_lambda, onorm, lam_init, tq, tk):
    B, S, W = qd.shape
    H = W // LANES
    qspec = pl.BlockSpec((1, tq, LANES), lambda b, h, i: (b, i, h))
    kvspec = pl.BlockSpec((1, S, LANES), lambda b, h, i: (b, 0, h))
    return pl.pallas_call(
        functools.partial(_diff_attn_kernel, lam_init=lam_init, tk=tk),
        grid=(B, H, S // tq),
        in_specs=[qspec, kvspec, kvspec, _const_spec(da_lambda.shape), _const_spec((1, LANES))],
        out_specs=qspec,
        out_shape=jax.ShapeDtypeStruct((B, S, W), BF16),
        scratch_shapes=[pltpu.VMEM((2 * tq, LANES), BF16), pltpu.VMEM((S // tk, LANES, tk), BF16),
                        pltpu.VMEM((1, 2 * tq), F32), pltpu.VMEM((1, 2 * tq), F32),
                        pltpu.VMEM((LANES, 2 * tq), F32)],
        compiler_params=_cparams(("arbitrary",) * 3, 32 << 20),
        name="diff_attn_prompt",
    )(qd, kb, v, da_lambda, onorm)


def _cross_attn_heads(q, mk, mv):
    dh = q.shape[-1] // MEM_HEADS
    outs = []
    for h in range(MEM_HEADS):
        sl = slice(h * dh, (h + 1) * dh)
        s = _dot_nt(q[:, sl], mk[:, sl])
        m = jnp.max(s, axis=-1, keepdims=True)
        p = jnp.exp(s - m)
        l = jnp.sum(p, axis=-1, keepdims=True)
        outs.append(_dot((p / l).astype(BF16), mv[:, sl]))
    return jnp.concatenate(outs, axis=-1)


def _out_cross_kernel(x_ref, mhg_ref, mda_ref, wo_ref, gq_ref, wq_ref, mk_ref, mv_ref,
                      wmo_ref, x2_ref):
    w = mhg_ref.shape[-1]
    x1 = x_ref[0] + _dot(mhg_ref[0], wo_ref[0:w, :]) + _dot(mda_ref[0], wo_ref[w:2 * w, :])
    hq = _rms(x1, gq_ref[...]).astype(BF16)
    dh = wq_ref.shape[1] // MEM_HEADS
    q = (_dot(hq, wq_ref[...]) * (float(dh) ** -0.5)).astype(BF16)
    o = _cross_attn_heads(q, mk_ref[0], mv_ref[0])
    x2_ref[0] = x1 + _dot(o.astype(BF16), wmo_ref[...])


def _out_cross_prompt(x, mhg, mda, w_out, gq, w_mq, mkb, mvb, w_mo, ts):
    B, S, D = x.shape
    W = mhg.shape[-1]
    N, MW = mkb.shape[1], mkb.shape[2]
    tok = lambda d: pl.BlockSpec((1, ts, d), lambda b, s: (b, s, 0))
    mem = pl.BlockSpec((1, N, MW), lambda b, s: (b, 0, 0))
    return pl.pallas_call(
        _out_cross_kernel,
        grid=(B, S // ts),
        in_specs=[tok(D), tok(W), tok(W), _const_spec(w_out.shape), _const_spec((1, D)),
                  _const_spec(w_mq.shape), mem, mem, _const_spec(w_mo.shape)],
        out_specs=tok(D),
        out_shape=jax.ShapeDtypeStruct((B, S, D), F32),
        compiler_params=_cparams(("arbitrary", "arbitrary"), 48 << 20),
        name="out_cross_prompt",
    )(x, mhg, mda, w_out, gq, w_mq, mkb, mvb, w_mo)


def _out_cross_sample_kernel(x_ref, mhg_ref, mda_ref, wo_ref, gq_ref, wq_ref, mk_ref, mv_ref,
                             wmo_ref, x2_ref, x1_s, q_s, o_s):
    b = pl.program_id(0)
    w = mhg_ref.shape[-1]

    @pl.when(b == 0)
    def _():
        x1 = x_ref[...] + _dot(mhg_ref[...], wo_ref[0:w, :]) + _dot(mda_ref[...], wo_ref[w:2 * w, :])
        x1_s[...] = x1
        hq = _rms(x1, gq_ref[...]).astype(BF16)
        dh = wq_ref.shape[1] // MEM_HEADS
        q_s[...] = _dot(hq, wq_ref[...]) * (float(dh) ** -0.5)

    q = q_s[pl.ds(b, 1), :].astype(BF16)
    o_s[pl.ds(b, 1), :] = _cross_attn_heads(q, mk_ref[0].astype(BF16), mv_ref[0].astype(BF16))

    @pl.when(b == pl.num_programs(0) - 1)
    def _():
        x2_ref[...] = x1_s[...] + _dot(o_s[...].astype(BF16), wmo_ref[...])


def _out_cross_sample(x, mhg, mda, w_out, gq, w_mq, mem_k, mem_v, w_mo):
    T, D = x.shape
    N, MW = mem_k.shape[1], mem_k.shape[2]
    mem = pl.BlockSpec((1, N, MW), lambda b: (b, 0, 0))
    return pl.pallas_call(
        _out_cross_sample_kernel,
        grid=(T,),
        in_specs=[_const_spec(x.shape), _const_spec(mhg.shape), _const_spec(mda.shape),
                  _const_spec(w_out.shape), _const_spec((1, D)), _const_spec(w_mq.shape),
                  mem, mem, _const_spec(w_mo.shape)],
        out_specs=pl.BlockSpec((T, D), lambda b: (0, 0)),
        out_shape=jax.ShapeDtypeStruct((T, D), F32),
        scratch_shapes=[pltpu.VMEM((T, D), F32), pltpu.VMEM((T, MW), F32), pltpu.VMEM((T, MW), F32)],
        compiler_params=_cparams(("arbitrary",), 32 << 20),
        name="out_cross_sample",
    )(x, mhg, mda, w_out, gq, w_mq, mem_k, mem_v, w_mo)


def _ffn_kernel(x_ref, gf_ref, wg_ref, wu_ref, wd_ref, gfin_ref, y_ref):
    x = x_ref[...]
    h = _rms(x, gf_ref[...]).astype(BF16)
    g = _dot(h, wg_ref[...])
    u = _dot(h, wu_ref[...])
    a = (g * _sigmoid(g) * u).astype(BF16)
    x3 = x + _dot(a, wd_ref[...])
    y_ref[...] = _rms(x3, gfin_ref[...])


def _ffn(x, gf, wg, wu, wd, gfin, ts):
    T, D = x.shape
    tok = pl.BlockSpec((ts, D), lambda i: (i, 0))
    return pl.pallas_call(
        _ffn_kernel,
        grid=(T // ts,),
        in_specs=[tok, _const_spec((1, D)), _const_spec(wg.shape), _const_spec(wu.shape),
                  _const_spec(wd.shape), _const_spec((1, D))],
        out_specs=tok,
        out_shape=jax.ShapeDtypeStruct((T, D), F32),
        compiler_params=_cparams(("arbitrary",), VMEM_LIMIT_CAP),
        name="ffn",
    )(x, gf, wg, wu, wd, gfin)


def _sample_mix_kernel(x_ref, g_ref, w_ref, lb_ref, cos_ref, sin_ref,
                       qd_ref, k_ref, v_ref, hq_ref, kk_ref, f_ref, vh_ref, gate_ref):
    h = _rms(x_ref[...], g_ref[...]).astype(BF16)
    z = _dot(h, w_ref[...])
    lb = _lower_bound(lb_ref)
    hq, k_hg, logf, hi, gate, dq, dk, dv = _mixer_sections(z, lb)
    cos = cos_ref[...]
    sin = sin_ref[...]
    qd_ref[...] = _rope(dq, cos, sin) * (float(LANES // 2) ** -0.5)
    k_ref[...] = _rope(dk, cos, sin)
    v_ref[...] = dv
    hq_ref[...] = hq
    kk_ref[...] = k_hg
    f_ref[...] = jnp.exp(logf)
    vh_ref[...] = hi
    gate_ref[...] = gate


def _sample_mix(x, g, w_in, hg_lb, cos, sin):
    T, D = x.shape
    W = w_in.shape[1] // 7
    sds = jax.ShapeDtypeStruct
    return pl.pallas_call(
        _sample_mix_kernel,
        out_shape=[sds((T, W), F32)] * 8,
        compiler_params=_cparams(None, 32 << 20),
        name="sample_mix",
    )(x, g, w_in, hg_lb, cos, sin)


def _sample_hgrn_kernel(hq_ref, kk_ref, f_ref, vh_ref, gate_ref, onorm_ref, s0_ref,
                        mix_ref, s1_ref):
    nb = s0_ref.shape[0]
    g0 = pl.program_id(0) * nb
    W = hq_ref.shape[-1]

    def columns(ref, h):
        rows = ref[pl.ds(pl.multiple_of(g0, nb), nb), h * LANES:(h + 1) * LANES]
        pad = jnp.zeros((LANES - nb, LANES), F32)
        return jnp.concatenate([rows, pad], axis=0).T

    o_rows = []
    for h in range(HG_HEADS):
        q_t, k_t, f_t = columns(hq_ref, h), columns(kk_ref, h), columns(f_ref, h)
        v_rows = vh_ref[pl.ds(pl.multiple_of(g0, nb), nb), h * LANES:(h + 1) * LANES]
        o_h = []
        for j in range(nb):
            bcast = lambda tile: jnp.broadcast_to(tile[:, j:j + 1], (LANES, LANES))
            s_new = bcast(f_t) * s0_ref[j, h] + bcast(k_t) * v_rows[j:j + 1, :]
            s1_ref[j, h] = s_new
            o_h.append(jnp.sum(bcast(q_t) * s_new, axis=0, keepdims=True))
        o_rows.append(jnp.concatenate(o_h, axis=0))
    o = jnp.concatenate(o_rows, axis=-1)
    gate = gate_ref[pl.ds(pl.multiple_of(g0, nb), nb), :]
    mix_ref[...] = (_head_rms(o, onorm_ref[...]) * gate).astype(BF16)


def _sample_hgrn(hq, kk, f, vh, gate, onorm, s0, nb=8):
    T, W = hq.shape
    st = pl.BlockSpec((nb, HG_HEADS, LANES, LANES), lambda i: (i, 0, 0, 0))
    full = _const_spec((T, W))
    return pl.pallas_call(
        _sample_hgrn_kernel,
        grid=(T // nb,),
        in_specs=[full] * 5 + [_const_spec((1, LANES)), st],
        out_specs=[pl.BlockSpec((nb, W), lambda i: (i, 0)), st],
        out_shape=[jax.ShapeDtypeStruct((T, W), BF16), jax.ShapeDtypeStruct(s0.shape, F32)],
        compiler_params=_cparams(("arbitrary",), 32 << 20),
        name="sample_hgrn",
    )(hq, kk, f, vh, gate, onorm, s0)


def _paged_attn_kernel(pt_ref, q_ref, kn_ref, vn_ref, lam_ref, onorm_ref, *refs,
                       lam_init, n_pg):
    k_refs = refs[:n_pg]
    v_refs = refs[n_pg:2 * n_pg]
    o_ref = refs[2 * n_pg]
    q8_s, m_s, l_s, acc_s = refs[2 * n_pg + 1:]
    j = pl.program_id(1)
    half = LANES // 2
    H = DA_HEADS
    n_q = 2 * H

    def head_rows(ref):
        rows = [ref[0, :, h * LANES:(h + 1) * LANES].astype(F32) for h in range(H)]
        return jnp.concatenate(rows * 2, axis=0)

    @pl.when(j == 0)
    def _():
        row = lax.broadcasted_iota(jnp.int32, (n_q, LANES), 0)
        lane = lax.broadcasted_iota(jnp.int32, (n_q, LANES), 1)
        q8 = jnp.where((row // H) == (lane // half), head_rows(q_ref), 0.0)
        q8_s[...] = q8.astype(BF16)
        m_s[...] = jnp.sum(q8 * head_rows(kn_ref), axis=-1, keepdims=True)
        l_s[...] = jnp.ones(l_s.shape, F32)
        acc_s[...] = head_rows(vn_ref)

    q8 = q8_s[...]
    s = jnp.concatenate([_dot_nt(q8, k_refs[p][0].astype(BF16)) for p in range(n_pg)], axis=-1)
    row = lax.broadcasted_iota(jnp.int32, s.shape, 0)
    col = lax.broadcasted_iota(jnp.int32, s.shape, 1)
    s = jnp.where((col % H) == (row % H), s, -jnp.inf)
    m_old = m_s[...]
    m_new = jnp.maximum(m_old, jnp.max(s, axis=-1, keepdims=True))
    alpha = jnp.exp(m_old - m_new)
    pr = jnp.exp(s - m_new)
    l_s[...] = alpha * l_s[...] + jnp.sum(pr, axis=-1, keepdims=True)
    pr = pr.astype(BF16)
    rows = k_refs[0].shape[1]
    pv = _dot(pr[:, 0:rows], v_refs[0][0].astype(BF16))
    for p in range(1, n_pg):
        pv = pv + _dot(pr[:, p * rows:(p + 1) * rows], v_refs[p][0].astype(BF16))
    acc_s[...] = alpha * acc_s[...] + pv
    m_s[...] = m_new

    @pl.when(j == pl.num_programs(1) - 1)
    def _():
        lam = _lambda(lam_ref, lam_init)
        o = acc_s[...] / l_s[...]
        y = _rms(o[0:H] - lam * o[H:n_q], onorm_ref[...]) * (1.0 - lam_init)
        o_ref[0] = jnp.concatenate([y[h:h + 1, :] for h in range(H)], axis=-1).astype(BF16)


def _paged_attn(page_table, qd, k_new, v_new, da_lambda, onorm, cache_k, cache_v, lam_init, n_pg):
    T, n_pages = page_table.shape
    n_phys, rows, dk = cache_k.shape
    W = qd.shape[-1]
    row = pl.BlockSpec((1, 1, W), lambda b, j, pt: (b, 0, 0))

    def page_spec(p):
        return pl.BlockSpec((1, rows, dk), lambda b, j, pt: (pt[b, j * n_pg + p], 0, 0))

    const = lambda shape: pl.BlockSpec(shape, lambda b, j, pt: (0,) * len(shape))
    grid_spec = pltpu.PrefetchScalarGridSpec(
        num_scalar_prefetch=1,
        grid=(T, n_pages // n_pg),
        in_specs=[row, row, row, const(da_lambda.shape), const((1, LANES))]
        + [page_spec(p) for p in range(n_pg)] * 2,
        out_specs=row,
        scratch_shapes=[pltpu.VMEM((2 * DA_HEADS, LANES), BF16),
                        pltpu.VMEM((2 * DA_HEADS, 1), F32),
                        pltpu.VMEM((2 * DA_HEADS, 1), F32),
                        pltpu.VMEM((2 * DA_HEADS, LANES), F32)],
    )
    out = pl.pallas_call(
        functools.partial(_paged_attn_kernel, lam_init=lam_init, n_pg=n_pg),
        grid_spec=grid_spec,
        out_shape=jax.ShapeDtypeStruct((T, 1, W), BF16),
        compiler_params=_cparams(("arbitrary", "arbitrary"), 48 << 20),
        name="paged_diff_attn",
    )(page_table, qd.reshape(T, 1, W), k_new.reshape(T, 1, W), v_new.reshape(T, 1, W),
      da_lambda, onorm, *([cache_k] * n_pg), *([cache_v] * n_pg))
    return out.reshape(T, W)


def _rope_tables(pos, dh, n_maps):
    inv = ROPE_THETA ** (-jnp.arange(0, dh, 2, dtype=F32) / dh)
    ang = pos.astype(F32)[:, None] * inv[None, :]
    cos = jnp.cos(ang)
    sin = jnp.sin(ang)
    cos = jnp.tile(jnp.concatenate([cos, cos], axis=-1), (1, n_maps))
    sin = jnp.tile(jnp.concatenate([-sin, sin], axis=-1), (1, n_maps))
    return cos, sin


def _pick_tile(n, pref):
    t = min(n, pref)
    while n % t:
        t //= 2
    return t


def kernel(x_prompt, x_sample, mem_prompt, cache_k, cache_v, cache_mem_k, cache_mem_v, state_hgrn, page_table, norm_mix, w_in, hg_lb, hg_onorm, da_lambda, da_onorm, w_out, norm_mem_q, norm_mem_kv, w_mq, w_mk, w_mv, w_mo, norm_ffn, w_gate, w_up, w_down, norm_final):
    B, S, D = x_prompt.shape
    T = x_sample.shape[0]
    depth = w_in.shape[0]
    assert depth == 1 and x_sample.shape[1] == 1
    l = 0
    lam_init = 0.8 - 0.6 * math.exp(-0.3 * l)
    n_phys, page = cache_k.shape[1], cache_k.shape[2]
    past_len = page_table.shape[1] * page
    W = w_in.shape[2] // 7

    bf = lambda w: w.astype(BF16)
    w_in_b, w_out_b = bf(w_in[l]), bf(w_out[l])
    w_mq_b, w_mk_b, w_mv_b, w_mo_b = bf(w_mq[l]), bf(w_mk[l]), bf(w_mv[l]), bf(w_mo[l])
    w_gate_b, w_up_b, w_down_b = bf(w_gate[l]), bf(w_up[l]), bf(w_down[l])
    row = lambda g: g.reshape(1, -1)
    lb2 = hg_lb[l:l + 2]
    lam_p = da_lambda[l]

    cos_p, sin_p = _rope_tables(jnp.arange(S), LANES // 2, 2 * DA_HEADS)
    cos_s, sin_s = _rope_tables(past_len + jnp.arange(1), LANES // 2, 2 * DA_HEADS)
    cos_s = jnp.broadcast_to(cos_s, (T, W))
    sin_s = jnp.broadcast_to(sin_s, (T, W))

    mk, mv, mkb, mvb = _mem_kv(mem_prompt, row(norm_mem_kv[l]), w_mk_b, w_mv_b)
    qd, k_p, v_p, kb, mix_hg, hs_p = _prompt_mix(
        x_prompt, row(norm_mix[l]), w_in_b, lb2, cos_p, sin_p, row(hg_onorm[l]),
        ts=_pick_tile(S, 256))
    mix_da = _diff_attn_prompt(qd, kb, v_p, lam_p, row(da_onorm[l]), lam_init,
                               tq=_pick_tile(S, 1024), tk=_pick_tile(S, 256))
    x2 = _out_cross_prompt(x_prompt, mix_hg, mix_da, w_out_b, row(norm_mem_q[l]), w_mq_b,
                           mkb, mvb, w_mo_b, ts=_pick_tile(S, 512))
    y_p = _ffn(x2.reshape(B * S, D), row(norm_ffn[l]), w_gate_b, w_up_b, w_down_b,
               row(norm_final), ts=_pick_tile(B * S, 512)).reshape(B, S, D)

    xs = x_sample.reshape(T, D)
    qd_s, k_s, v_s, hq_s, kk_s, f_s, vh_s, gate_s = _sample_mix(
        xs, row(norm_mix[l]), w_in_b, lb2, cos_s, sin_s)
    mix_hg_s, hs_s = _sample_hgrn(hq_s, kk_s, f_s, vh_s, gate_s, row(hg_onorm[l]), state_hgrn[l])
    mix_da_s = _paged_attn(page_table, qd_s, k_s, v_s, lam_p, row(da_onorm[l]),
                           cache_k.reshape(depth * n_phys, page * DA_HEADS, LANES),
                           cache_v.reshape(depth * n_phys, page * DA_HEADS, LANES),
                           lam_init, n_pg=_pick_tile(page_table.shape[1], 16))
    MW = w_mk.shape[2]
    x2_s = _out_cross_sample(xs, mix_hg_s, mix_da_s, w_out_b, row(norm_mem_q[l]), w_mq_b,
                             cache_mem_k[l].reshape(T, -1, MW), cache_mem_v[l].reshape(T, -1, MW),
                             w_mo_b)
    y_s = _ffn(x2_s, row(norm_ffn[l]), w_gate_b, w_up_b, w_down_b, row(norm_final), ts=T)

    dk = LANES
    return (y_p, y_s.reshape(T, 1, D),
            hs_p[None],
            k_p.reshape(1, B, S, DA_HEADS, dk), v_p.reshape(1, B, S, DA_HEADS, dk),
            mk.reshape(1, B, -1, MEM_HEADS, MW // MEM_HEADS),
            mv.reshape(1, B, -1, MEM_HEADS, MW // MEM_HEADS),
            hs_s[None],
            k_s.reshape(1, T, 1, DA_HEADS, dk), v_s.reshape(1, T, 1, DA_HEADS, dk))
```

```python
import functools
import math

import jax
import jax.numpy as jnp
from jax import lax
from jax.experimental import pallas as pl
from jax.experimental.pallas import tpu as pltpu

F32 = jnp.float32
BF16 = jnp.bfloat16
EPS = 1e-6
ROPE_THETA = 10000.0

HG_HEADS = 4
DA_HEADS = 4
MEM_HEADS = 4
HG_CHUNK = 64
LANES = 128
BF16_ROWS = 16
VMEM_LIMIT_CAP = 56 << 20

_NT = (((1,), (1,)), ((), ()))
_TN = (((0,), (0,)), ((), ()))


def _dot(a, b):
    return jnp.dot(a, b, preferred_element_type=F32)


def _dot_nt(a, b):
    return lax.dot_general(a, b, _NT, preferred_element_type=F32)


def _dot_tn(a, b):
    return lax.dot_general(a, b, _TN, preferred_element_type=F32)


def _rms(x, g):
    ms = jnp.mean(x * x, axis=-1, keepdims=True)
    return x * lax.rsqrt(ms + EPS) * g


def _sigmoid(x):
    return 1.0 / (1.0 + jnp.exp(-x))


def _cparams(semantics, vmem_bytes):
    return pltpu.CompilerParams(
        dimension_semantics=semantics,
        vmem_limit_bytes=int(min(max(vmem_bytes, 16 << 20), VMEM_LIMIT_CAP)))


def _const_spec(shape):
    nd = len(shape)
    return pl.BlockSpec(shape, lambda *_: (0,) * nd, pipeline_mode=pl.Buffered(1))


def _lower_bound(lb_ref):
    a0 = lb_ref[0:1, :]
    a1 = lb_ref[1:2, :]
    m = jnp.maximum(a0, a1)
    e0 = jnp.exp(a0 - m)
    e1 = jnp.exp(a1 - m)
    return e0 / (e0 + e1)


def _rope(x, cos, sin_signed):
    n = x.shape[-1]
    lane = lax.broadcasted_iota(jnp.int32, x.shape, x.ndim - 1)
    swapped = jnp.where((lane & 63) < 32,
                        pltpu.roll(x, n - 32, x.ndim - 1),
                        pltpu.roll(x, 32, x.ndim - 1))
    return x * cos + swapped * sin_signed


def _lambda(lam_ref, lam_init):
    lp = lam_ref[...]
    s01 = jnp.sum(lp[0:1, :] * lp[1:2, :], axis=-1, keepdims=True)
    s23 = jnp.sum(lp[2:3, :] * lp[3:4, :], axis=-1, keepdims=True)
    return jnp.exp(s01) - jnp.exp(s23) + lam_init


def _mixer_sections(z, lb):
    w = z.shape[-1] // 7
    hq, zf, hi, hg, dq, dk, dv = (z[:, i * w:(i + 1) * w] for i in range(7))
    sig = _sigmoid(zf)
    logf = jnp.log(lb + (1.0 - lb) * sig)
    k_hg = (1.0 - lb) * (1.0 - sig)
    gate = hg * _sigmoid(hg)
    return hq, k_hg, logf, hi, gate, dq, dk, dv


def _head_rms(o, g):
    outs = []
    for h in range(o.shape[-1] // LANES):
        oh = o[:, h * LANES:(h + 1) * LANES]
        outs.append(_rms(oh, g))
    return jnp.concatenate(outs, axis=-1)


def _memkv_kernel(mem_ref, g_ref, wk_ref, wv_ref, k_ref, v_ref, kb_ref, vb_ref):
    m = _rms(mem_ref[0], g_ref[...]).astype(BF16)
    k = _dot(m, wk_ref[...])
    v = _dot(m, wv_ref[...])
    dh = k_ref.shape[-1]
    for h in range(MEM_HEADS):
        k_ref[0, :, h, :] = k[:, h * dh:(h + 1) * dh]
        v_ref[0, :, h, :] = v[:, h * dh:(h + 1) * dh]
    kb_ref[0] = k.astype(BF16)
    vb_ref[0] = v.astype(BF16)


def _mem_kv(mem, g, wk, wv):
    B, N, D = mem.shape
    W = wk.shape[1]
    dh = W // MEM_HEADS
    blk = lambda d: pl.BlockSpec((1, N, d), lambda b: (b, 0, 0))
    blk4 = pl.BlockSpec((1, N, MEM_HEADS, dh), lambda b: (b, 0, 0, 0))
    return pl.pallas_call(
        _memkv_kernel,
        grid=(B,),
        in_specs=[blk(D), _const_spec((1, D)), _const_spec((D, W)), _const_spec((D, W))],
        out_specs=[blk4, blk4, blk(W), blk(W)],
        out_shape=[jax.ShapeDtypeStruct((B, N, MEM_HEADS, dh), F32)] * 2
        + [jax.ShapeDtypeStruct((B, N, W), BF16)] * 2,
        compiler_params=_cparams(("arbitrary",), 40 << 20),
        name="mem_kv",
    )(mem, g, wk, wv)


def _hgrn_chunk(q, kk, lf, v, st_ref):
    C, W = q.shape
    t = lax.broadcasted_iota(jnp.int32, (C, W), 0)
    ti = lax.broadcasted_iota(jnp.int32, (C, C), 0)
    si = lax.broadcasted_iota(jnp.int32, (C, C), 1)
    n_lvl = int(math.log2(C))

    c = lf
    lvls = []
    for l in range(n_lvl):
        m = 1 << l
        upper = (t & m) != 0
        if m < 8:
            y = c
            for i in range(l):
                y = jnp.where((t & (1 << i)) == 0, pltpu.roll(y, C - (1 << i), 0), y)
            bc = jnp.where(upper, pltpu.roll(y, m, 0), y)
        else:
            pieces = []
            for j in range(C // (2 * m)):
                r = 2 * m * j + m - 1
                pieces.append(jnp.broadcast_to(c[r:r + 1, :], (2 * m, W)))
            bc = pieces[0] if len(pieces) == 1 else jnp.concatenate(pieces, axis=0)
        e = jnp.exp(jnp.where(upper, c, bc - c))
        lvls.append((jnp.where(upper, q, kk) * e).astype(BF16))
        c = c + jnp.where(upper, bc, 0.0)
    b = c
    b_last = b[C - 1:C, :]
    q_bf = q.astype(BF16)
    k_bf = kk.astype(BF16)
    v_bf = v.astype(BF16)
    q_state = (q * jnp.exp(b)).astype(BF16)
    k_state = (kk * jnp.exp(b_last - b)).astype(BF16)
    decay = jnp.exp(b_last)

    diff_bits = jnp.where(ti > si, ti ^ si, 0)
    keeps = [(diff_bits >> l) == 1 for l in range(n_lvl)]

    outs = []
    for h in range(W // LANES):
        sl = slice(h * LANES, (h + 1) * LANES)
        a = jnp.where(ti == si, _dot_nt(q_bf[:, sl], k_bf[:, sl]), 0.0)
        for l in range(n_lvl):
            r = lvls[l][:, sl]
            a = a + jnp.where(keeps[l], _dot_nt(r, r), 0.0)
        st = st_ref[h]
        o = _dot(a.astype(BF16), v_bf[:, sl]) + _dot_nt(q_state[:, sl], st.astype(BF16))
        st_ref[h] = st * decay[:, sl] + _dot_tn(v_bf[:, sl], k_state[:, sl])
        outs.append(o)
    return jnp.concatenate(outs, axis=-1)


def _prompt_mix_kernel(x_ref, g_ref, w_ref, lb_ref, cos_ref, sin_ref, onorm_ref,
                       qd_ref, k_ref, v_ref, kb_ref, va_ref, mix_ref, state_ref,
                       q_s, kk_s, lf_s, vh_s, o_s, st_s):
    s = pl.program_id(1)

    @pl.when(s == 0)
    def _():
        st_s[...] = jnp.zeros_like(st_s)

    h = _rms(x_ref[0], g_ref[...]).astype(BF16)
    z = _dot(h, w_ref[...])
    lb = _lower_bound(lb_ref)
    hq, k_hg, logf, hi, gate, dq, dk, dv = _mixer_sections(z, lb)

    cos = cos_ref[...]
    sin = sin_ref[...]
    q_da = _rope(dq, cos, sin)
    k_da = _rope(dk, cos, sin)
    qd_ref[0] = (q_da * (math.log2(math.e) * float(LANES // 2) ** -0.5)).astype(BF16)
    ts = x_ref.shape[1]
    for hh in range(DA_HEADS):
        rows = pl.ds(hh, ts, stride=DA_HEADS)
        k_ref[0, rows, :] = k_da[:, hh * LANES:(hh + 1) * LANES]
        v_ref[0, rows, :] = dv[:, hh * LANES:(hh + 1) * LANES]
    kb_ref[0] = k_da.astype(BF16)
    va_ref[0] = dv

    q_s[...] = hq
    kk_s[...] = k_hg
    lf_s[...] = logf
    vh_s[...] = hi

    def chunk(ci, carry):
        rows = pl.ds(pl.multiple_of(ci * HG_CHUNK, HG_CHUNK), HG_CHUNK)
        o_s[rows, :] = _hgrn_chunk(q_s[rows, :], kk_s[rows, :], lf_s[rows, :],
                                   vh_s[rows, :], st_s)
        return carry

    lax.fori_loop(0, x_ref.shape[1] // HG_CHUNK, chunk, 0, unroll=True)

    mix_ref[0] = (_head_rms(o_s[...], onorm_ref[...]) * gate).astype(BF16)

    @pl.when(s == pl.num_programs(1) - 1)
    def _():
        for hh in range(HG_HEADS):
            state_ref[0, hh] = st_s[hh].T


def _prompt_mix(x, g, w_in, hg_lb, cos, sin, onorm, ts):
    B, S, D = x.shape
    W = w_in.shape[1] // 7
    tok = lambda d: pl.BlockSpec((1, ts, d), lambda b, s: (b, s, 0))
    tab = pl.BlockSpec((ts, W), lambda b, s: (s, 0))
    rows = pl.BlockSpec((1, ts * DA_HEADS, LANES), lambda b, s: (b, s, 0))
    state = pl.BlockSpec((1, HG_HEADS, LANES, LANES), lambda b, s: (b, 0, 0, 0))
    sds = jax.ShapeDtypeStruct
    return pl.pallas_call(
        _prompt_mix_kernel,
        grid=(B, S // ts),
        in_specs=[tok(D), _const_spec((1, D)), _const_spec(w_in.shape),
                  _const_spec(hg_lb.shape), tab, tab, _const_spec((1, LANES))],
        out_specs=[tok(W), rows, rows, tok(W), tok(W), tok(W), state],
        out_shape=[sds((B, S, W), BF16), sds((B, S * DA_HEADS, LANES), F32),
                   sds((B, S * DA_HEADS, LANES), F32), sds((B, S, W), BF16),
                   sds((B, S, W), F32), sds((B, S, W), BF16),
                   sds((B, HG_HEADS, LANES, LANES), F32)],
        scratch_shapes=[pltpu.VMEM((ts, W), F32)] * 5
        + [pltpu.VMEM((HG_HEADS, LANES, LANES), F32)],
        compiler_params=_cparams(("arbitrary", "arbitrary"), 48 << 20),
        name="prompt_mix",
    )(x, g, w_in, hg_lb, cos, sin, onorm)


def _diff_attn_kernel(q_ref, k_ref, v_ref, lam_ref, onorm_ref, o_ref,
                      qq_s, vt_s, m_s, acc_s, *, lam_init, tk):
    i = pl.program_id(2)
    tq = q_ref.shape[1]
    n_ck = tq // tk
    n_kv = v_ref.shape[1] // tk

    @pl.when(i == 0)
    def _():
        ones = jnp.ones((vt_s.shape[1] - LANES, tk), BF16)
        for jj in range(n_kv):
            vt_s[jj, 0:LANES, :] = v_ref[0, jj * tk:(jj + 1) * tk, :].T.astype(BF16)
            vt_s[jj, LANES:, :] = ones

    q = q_ref[0]
    lane = lax.broadcasted_iota(jnp.int32, q.shape, 1)
    zero = jnp.zeros_like(q)
    qq_s[0:tq, :] = jnp.where(lane < LANES // 2, q, zero)
    qq_s[tq:2 * tq, :] = jnp.where(lane >= LANES // 2, q, zero)
    m_s[...] = jnp.full(m_s.shape, -jnp.inf, F32)
    acc_s[...] = jnp.zeros(acc_s.shape, F32)

    def scores(j, c):
        k = k_ref[0, pl.ds(pl.multiple_of(j * tk, tk), tk), :]
        return _dot_nt(k, qq_s[c * tk:(c + 1) * tk, :])

    def softmax(j, c, tri, s):
        cols = slice(c * tk, (c + 1) * tk)
        if tri:
            key = lax.broadcasted_iota(jnp.int32, s.shape, 0)
            qry = lax.broadcasted_iota(jnp.int32, s.shape, 1)
            s = jnp.where(key <= qry, s, -jnp.inf)
        m_old = m_s[:, cols]
        m_new = jnp.maximum(m_old, jnp.max(s, axis=0, keepdims=True))
        alpha = jnp.exp2(m_old - m_new)
        p = jnp.exp2(s - m_new)
        m_s[:, cols] = m_new
        return cols, alpha, _dot(vt_s[j], p.astype(BF16))

    def accumulate(cols, alpha, pv):
        acc_s[:, cols] = alpha * acc_s[:, cols] + pv

    def run(work):
        s = scores(*work[0][:2])
        pending = None
        for n, (j, c, tri) in enumerate(work):
            s_next = scores(*work[n + 1][:2]) if n + 1 < len(work) else None
            done = softmax(j, c, tri, s)
            if pending is not None:
                accumulate(*pending)
            pending = done
            s = s_next
        accumulate(*pending)

    def full_block(j, carry):
        run([(j, c, False) for c in range(2 * n_ck)])
        return carry

    lax.fori_loop(0, i * n_ck, full_block, 0)
    run([(i * n_ck + d, mp * n_ck + cq, cq == d)
         for d in range(n_ck) for mp in range(2) for cq in range(d, n_ck)])

    lam = _lambda(lam_ref, lam_init)
    a = acc_s[0:LANES, :] * (1.0 / acc_s[LANES:LANES + 1, :])
    o_t = a[:, 0:tq] - lam * a[:, tq:2 * tq]
    ms = jnp.mean(o_t * o_t, axis=0, keepdims=True)
    y = (o_t * lax.rsqrt(ms + EPS)).T
    o_ref[0] = (y * onorm_ref[...] * (1.0 - lam_init)).astype(BF16)


def _diff_attn_prompt(qd, kb, v, da_lambda, onorm, lam_init, tq, tk):
    B, S, W = qd.shape
    H = W // LANES
    qspec = pl.BlockSpec((1, tq, LANES), lambda b, h, i: (b, i, h))
    kvspec = pl.BlockSpec((1, S, LANES), lambda b, h, i: (b, 0, h))
    return pl.pallas_call(
        functools.partial(_diff_attn_kernel, lam_init=lam_init, tk=tk),
        grid=(B, H, S // tq),
        in_specs=[qspec, kvspec, kvspec, _const_spec(da_lambda.shape), _const_spec((1, LANES))],
        out_specs=qspec,
        out_shape=jax.ShapeDtypeStruct((B, S, W), BF16),
        scratch_shapes=[pltpu.VMEM((2 * tq, LANES), BF16),
                        pltpu.VMEM((S // tk, LANES + BF16_ROWS, tk), BF16),
                        pltpu.VMEM((1, 2 * tq), F32),
                        pltpu.VMEM((LANES + BF16_ROWS, 2 * tq), F32)],
        compiler_params=_cparams(("arbitrary",) * 3, 32 << 20),
        name="diff_attn_prompt",
    )(qd, kb, v, da_lambda, onorm)


def _cross_attn_heads(q, mk_head, mv_head):
    dh = q.shape[-1] // MEM_HEADS
    outs = []
    for h in range(MEM_HEADS):
        s = _dot_nt(q[:, h * dh:(h + 1) * dh], mk_head(h))
        m = jnp.max(s, axis=-1, keepdims=True)
        p = jnp.exp(s - m)
        l = jnp.sum(p, axis=-1, keepdims=True)
        outs.append(_dot((p / l).astype(BF16), mv_head(h)))
    return jnp.concatenate(outs, axis=-1)


def _out_cross_kernel(x_ref, mhg_ref, mda_ref, wo_ref, gq_ref, wq_ref, mk_ref, mv_ref,
                      wmo_ref, x2_ref):
    w = mhg_ref.shape[-1]
    x1 = x_ref[0] + _dot(mhg_ref[0], wo_ref[0:w, :]) + _dot(mda_ref[0], wo_ref[w:2 * w, :])
    hq = _rms(x1, gq_ref[...]).astype(BF16)
    dh = wq_ref.shape[1] // MEM_HEADS
    q = (_dot(hq, wq_ref[...]) * (float(dh) ** -0.5)).astype(BF16)
    o = _cross_attn_heads(q, lambda h: mk_ref[0, :, h * dh:(h + 1) * dh],
                          lambda h: mv_ref[0, :, h * dh:(h + 1) * dh])
    x2_ref[0] = x1 + _dot(o.astype(BF16), wmo_ref[...])


def _out_cross_prompt(x, mhg, mda, w_out, gq, w_mq, mkb, mvb, w_mo, ts):
    B, S, D = x.shape
    W = mhg.shape[-1]
    N, MW = mkb.shape[1], mkb.shape[2]
    tok = lambda d: pl.BlockSpec((1, ts, d), lambda b, s: (b, s, 0))
    mem = pl.BlockSpec((1, N, MW), lambda b, s: (b, 0, 0))
    return pl.pallas_call(
        _out_cross_kernel,
        grid=(B, S // ts),
        in_specs=[tok(D), tok(W), tok(W), _const_spec(w_out.shape), _const_spec((1, D)),
                  _const_spec(w_mq.shape), mem, mem, _const_spec(w_mo.shape)],
        out_specs=tok(D),
        out_shape=jax.ShapeDtypeStruct((B, S, D), F32),
        compiler_params=_cparams(("arbitrary", "arbitrary"), 48 << 20),
        name="out_cross_prompt",
    )(x, mhg, mda, w_out, gq, w_mq, mkb, mvb, w_mo)


def _out_cross_sample_kernel(x_ref, mhg_ref, mda_ref, wo_ref, gq_ref, wq_ref, mk_ref, mv_ref,
                             wmo_ref, x2_ref, x1_s, q_s, o_s):
    b = pl.program_id(0)
    w = mhg_ref.shape[-1]

    @pl.when(b == 0)
    def _():
        x1 = x_ref[...] + _dot(mhg_ref[...], wo_ref[0:w, :]) + _dot(mda_ref[...], wo_ref[w:2 * w, :])
        x1_s[...] = x1
        hq = _rms(x1, gq_ref[...]).astype(BF16)
        dh = wq_ref.shape[1] // MEM_HEADS
        q_s[...] = _dot(hq, wq_ref[...]) * (float(dh) ** -0.5)

    q = q_s[pl.ds(b, 1), :].astype(BF16)
    dh = q.shape[-1] // MEM_HEADS
    o_s[pl.ds(b, 1), :] = _cross_attn_heads(
        q, lambda h: mk_ref[0, :, h * dh:(h + 1) * dh].astype(BF16),
        lambda h: mv_ref[0, :, h * dh:(h + 1) * dh].astype(BF16))

    @pl.when(b == pl.num_programs(0) - 1)
    def _():
        x2_ref[...] = x1_s[...] + _dot(o_s[...].astype(BF16), wmo_ref[...])


def _out_cross_sample(x, mhg, mda, w_out, gq, w_mq, mem_k, mem_v, w_mo):
    T, D = x.shape
    N, MW = mem_k.shape[1], mem_k.shape[2]
    mem = pl.BlockSpec((1, N, MW), lambda b: (b, 0, 0))
    return pl.pallas_call(
        _out_cross_sample_kernel,
        grid=(T,),
        in_specs=[_const_spec(x.shape), _const_spec(mhg.shape), _const_spec(mda.shape),
                  _const_spec(w_out.shape), _const_spec((1, D)), _const_spec(w_mq.shape),
                  mem, mem, _const_spec(w_mo.shape)],
        out_specs=pl.BlockSpec((T, D), lambda b: (0, 0)),
        out_shape=jax.ShapeDtypeStruct((T, D), F32),
        scratch_shapes=[pltpu.VMEM((T, D), F32), pltpu.VMEM((T, MW), F32), pltpu.VMEM((T, MW), F32)],
        compiler_params=_cparams(("arbitrary",), 32 << 20),
        name="out_cross_sample",
    )(x, mhg, mda, w_out, gq, w_mq, mem_k, mem_v, w_mo)


def _ffn_kernel(x_ref, gf_ref, wg_ref, wu_ref, wd_ref, gfin_ref, y_ref):
    x = x_ref[...]
    h = _rms(x, gf_ref[...]).astype(BF16)
    g = _dot(h, wg_ref[...])
    u = _dot(h, wu_ref[...])
    a = (g * _sigmoid(g) * u).astype(BF16)
    x3 = x + _dot(a, wd_ref[...])
    y_ref[...] = _rms(x3, gfin_ref[...])


def _ffn(x, gf, wg, wu, wd, gfin, ts):
    T, D = x.shape
    tok = pl.BlockSpec((ts, D), lambda i: (i, 0))
    return pl.pallas_call(
        _ffn_kernel,
        grid=(T // ts,),
        in_specs=[tok, _const_spec((1, D)), _const_spec(wg.shape), _const_spec(wu.shape),
                  _const_spec(wd.shape), _const_spec((1, D))],
        out_specs=tok,
        out_shape=jax.ShapeDtypeStruct((T, D), F32),
        compiler_params=_cparams(("arbitrary",), VMEM_LIMIT_CAP),
        name="ffn",
    )(x, gf, wg, wu, wd, gfin)


def _sample_mix_kernel(x_ref, g_ref, w_ref, lb_ref, cos_ref, sin_ref,
                       qd_ref, k_ref, v_ref, hq_ref, kk_ref, f_ref, vh_ref, gate_ref):
    h = _rms(x_ref[...], g_ref[...]).astype(BF16)
    z = _dot(h, w_ref[...])
    lb = _lower_bound(lb_ref)
    hq, k_hg, logf, hi, gate, dq, dk, dv = _mixer_sections(z, lb)
    cos = cos_ref[...]
    sin = sin_ref[...]
    qd_ref[...] = _rope(dq, cos, sin) * (float(LANES // 2) ** -0.5)
    k_ref[...] = _rope(dk, cos, sin)
    v_ref[...] = dv
    hq_ref[...] = hq
    kk_ref[...] = k_hg
    f_ref[...] = jnp.exp(logf)
    vh_ref[...] = hi
    gate_ref[...] = gate


def _sample_mix(x, g, w_in, hg_lb, cos, sin):
    T, D = x.shape
    W = w_in.shape[1] // 7
    sds = jax.ShapeDtypeStruct
    return pl.pallas_call(
        _sample_mix_kernel,
        out_shape=[sds((T, W), F32)] * 8,
        compiler_params=_cparams(None, 32 << 20),
        name="sample_mix",
    )(x, g, w_in, hg_lb, cos, sin)


def _sample_hgrn_kernel(hq_ref, kk_ref, f_ref, vh_ref, gate_ref, onorm_ref, s0_ref,
                        mix_ref, s1_ref):
    nb = s0_ref.shape[0]
    g0 = pl.program_id(0) * nb
    W = hq_ref.shape[-1]

    def columns(ref, h):
        rows = ref[pl.ds(pl.multiple_of(g0, nb), nb), h * LANES:(h + 1) * LANES]
        pad = jnp.zeros((LANES - nb, LANES), F32)
        return jnp.concatenate([rows, pad], axis=0).T

    o_rows = []
    for h in range(HG_HEADS):
        q_t, k_t, f_t = columns(hq_ref, h), columns(kk_ref, h), columns(f_ref, h)
        v_rows = vh_ref[pl.ds(pl.multiple_of(g0, nb), nb), h * LANES:(h + 1) * LANES]
        o_h = []
        for j in range(nb):
            bcast = lambda tile: jnp.broadcast_to(tile[:, j:j + 1], (LANES, LANES))
            s_new = bcast(f_t) * s0_ref[j, h] + bcast(k_t) * v_rows[j:j + 1, :]
            s1_ref[j, h] = s_new
            o_h.append(jnp.sum(bcast(q_t) * s_new, axis=0, keepdims=True))
        o_rows.append(jnp.concatenate(o_h, axis=0))
    o = jnp.concatenate(o_rows, axis=-1)
    gate = gate_ref[pl.ds(pl.multiple_of(g0, nb), nb), :]
    mix_ref[...] = (_head_rms(o, onorm_ref[...]) * gate).astype(BF16)


def _sample_hgrn(hq, kk, f, vh, gate, onorm, s0, nb=8):
    T, W = hq.shape
    st = pl.BlockSpec((nb, HG_HEADS, LANES, LANES), lambda i: (i, 0, 0, 0))
    full = _const_spec((T, W))
    return pl.pallas_call(
        _sample_hgrn_kernel,
        grid=(T // nb,),
        in_specs=[full] * 5 + [_const_spec((1, LANES)), st],
        out_specs=[pl.BlockSpec((nb, W), lambda i: (i, 0)), st],
        out_shape=[jax.ShapeDtypeStruct((T, W), BF16), jax.ShapeDtypeStruct(s0.shape, F32)],
        compiler_params=_cparams(("arbitrary",), 32 << 20),
        name="sample_hgrn",
    )(hq, kk, f, vh, gate, onorm, s0)


def _paged_attn_kernel(pt_ref, q_ref, kn_ref, vn_ref, lam_ref, onorm_ref, *refs,
                       lam_init, n_pg):
    k_refs = refs[:n_pg]
    v_refs = refs[n_pg:2 * n_pg]
    o_ref = refs[2 * n_pg]
    q8_s, m_s, l_s, acc_s = refs[2 * n_pg + 1:]
    j = pl.program_id(1)
    half = LANES // 2
    H = DA_HEADS
    n_q = 2 * H

    def head_rows(ref):
        rows = [ref[0, :, h * LANES:(h + 1) * LANES].astype(F32) for h in range(H)]
        return jnp.concatenate(rows * 2, axis=0)

    @pl.when(j == 0)
    def _():
        row = lax.broadcasted_iota(jnp.int32, (n_q, LANES), 0)
        lane = lax.broadcasted_iota(jnp.int32, (n_q, LANES), 1)
        q8 = jnp.where((row // H) == (lane // half), head_rows(q_ref), 0.0)
        q8_s[...] = q8.astype(BF16)
        m_s[...] = jnp.sum(q8 * head_rows(kn_ref), axis=-1, keepdims=True)
        l_s[...] = jnp.ones(l_s.shape, F32)
        acc_s[...] = head_rows(vn_ref)

    q8 = q8_s[...]
    s = jnp.concatenate([_dot_nt(q8, k_refs[p][0].astype(BF16)) for p in range(n_pg)], axis=-1)
    row = lax.broadcasted_iota(jnp.int32, s.shape, 0)
    col = lax.broadcasted_iota(jnp.int32, s.shape, 1)
    s = jnp.where((col % H) == (row % H), s, -jnp.inf)
    m_old = m_s[...]
    m_new = jnp.maximum(m_old, jnp.max(s, axis=-1, keepdims=True))
    alpha = jnp.exp(m_old - m_new)
    pr = jnp.exp(s - m_new)
    l_s[...] = alpha * l_s[...] + jnp.sum(pr, axis=-1, keepdims=True)
    pr = pr.astype(BF16)
    rows = k_refs[0].shape[1]
    pv = _dot(pr[:, 0:rows], v_refs[0][0].astype(BF16))
    for p in range(1, n_pg):
        pv = pv + _dot(pr[:, p * rows:(p + 1) * rows], v_refs[p][0].astype(BF16))
    acc_s[...] = alpha * acc_s[...] + pv
    m_s[...] = m_new

    @pl.when(j == pl.num_programs(1) - 1)
    def _():
        lam = _lambda(lam_ref, lam_init)
        o = acc_s[...] / l_s[...]
        y = _rms(o[0:H] - lam * o[H:n_q], onorm_ref[...]) * (1.0 - lam_init)
        o_ref[0] = jnp.concatenate([y[h:h + 1, :] for h in range(H)], axis=-1).astype(BF16)


def _paged_attn(page_table, qd, k_new, v_new, da_lambda, onorm, cache_k, cache_v, lam_init, n_pg):
    T, n_pages = page_table.shape
    n_phys, rows, dk = cache_k.shape
    W = qd.shape[-1]
    row = pl.BlockSpec((1, 1, W), lambda b, j, pt: (b, 0, 0))

    def page_spec(p):
        return pl.BlockSpec((1, rows, dk), lambda b, j, pt: (pt[b, j * n_pg + p], 0, 0))

    const = lambda shape: pl.BlockSpec(shape, lambda b, j, pt: (0,) * len(shape))
    grid_spec = pltpu.PrefetchScalarGridSpec(
        num_scalar_prefetch=1,
        grid=(T, n_pages // n_pg),
        in_specs=[row, row, row, const(da_lambda.shape), const((1, LANES))]
        + [page_spec(p) for p in range(n_pg)] * 2,
        out_specs=row,
        scratch_shapes=[pltpu.VMEM((2 * DA_HEADS, LANES), BF16),
                        pltpu.VMEM((2 * DA_HEADS, 1), F32),
                        pltpu.VMEM((2 * DA_HEADS, 1), F32),
                        pltpu.VMEM((2 * DA_HEADS, LANES), F32)],
    )
    out = pl.pallas_call(
        functools.partial(_paged_attn_kernel, lam_init=lam_init, n_pg=n_pg),
        grid_spec=grid_spec,
        out_shape=jax.ShapeDtypeStruct((T, 1, W), BF16),
        compiler_params=_cparams(("arbitrary", "arbitrary"), 48 << 20),
        name="paged_diff_attn",
    )(page_table, qd.reshape(T, 1, W), k_new.reshape(T, 1, W), v_new.reshape(T, 1, W),
      da_lambda, onorm, *([cache_k] * n_pg), *([cache_v] * n_pg))
    return out.reshape(T, W)


def _rope_tables(pos, dh, n_maps):
    inv = ROPE_THETA ** (-jnp.arange(0, dh, 2, dtype=F32) / dh)
    ang = pos.astype(F32)[:, None] * inv[None, :]
    cos = jnp.cos(ang)
    sin = jnp.sin(ang)
    cos = jnp.tile(jnp.concatenate([cos, cos], axis=-1), (1, n_maps))
    sin = jnp.tile(jnp.concatenate([-sin, sin], axis=-1), (1, n_maps))
    return cos, sin


def _pick_tile(n, pref):
    t = min(n, pref)
    while n % t:
        t //= 2
    return t


def kernel(x_prompt, x_sample, mem_prompt, cache_k, cache_v, cache_mem_k, cache_mem_v, state_hgrn, page_table, norm_mix, w_in, hg_lb, hg_onorm, da_lambda, da_onorm, w_out, norm_mem_q, norm_mem_kv, w_mq, w_mk, w_mv, w_mo, norm_ffn, w_gate, w_up, w_down, norm_final):
    B, S, D = x_prompt.shape
    T = x_sample.shape[0]
    depth = w_in.shape[0]
    assert depth == 1 and x_sample.shape[1] == 1
    l = 0
    lam_init = 0.8 - 0.6 * math.exp(-0.3 * l)
    n_phys, page = cache_k.shape[1], cache_k.shape[2]
    past_len = page_table.shape[1] * page
    W = w_in.shape[2] // 7

    bf = lambda w: w.astype(BF16)
    w_in_b, w_out_b = bf(w_in[l]), bf(w_out[l])
    w_mq_b, w_mk_b, w_mv_b, w_mo_b = bf(w_mq[l]), bf(w_mk[l]), bf(w_mv[l]), bf(w_mo[l])
    w_gate_b, w_up_b, w_down_b = bf(w_gate[l]), bf(w_up[l]), bf(w_down[l])
    row = lambda g: g.reshape(1, -1)
    lb2 = hg_lb[l:l + 2]
    lam_p = da_lambda[l]

    cos_p, sin_p = _rope_tables(jnp.arange(S), LANES // 2, 2 * DA_HEADS)
    cos_s, sin_s = _rope_tables(past_len + jnp.arange(1), LANES // 2, 2 * DA_HEADS)
    cos_s = jnp.broadcast_to(cos_s, (T, W))
    sin_s = jnp.broadcast_to(sin_s, (T, W))

    mk, mv, mkb, mvb = _mem_kv(mem_prompt, row(norm_mem_kv[l]), w_mk_b, w_mv_b)
    qd, k_p, v_p, kb, v_att, mix_hg, hs_p = _prompt_mix(
        x_prompt, row(norm_mix[l]), w_in_b, lb2, cos_p, sin_p, row(hg_onorm[l]),
        ts=_pick_tile(S, 256))
    mix_da = _diff_attn_prompt(qd, kb, v_att, lam_p, row(da_onorm[l]), lam_init,
                               tq=_pick_tile(S, 1024), tk=_pick_tile(S, 256))
    x2 = _out_cross_prompt(x_prompt, mix_hg, mix_da, w_out_b, row(norm_mem_q[l]), w_mq_b,
                           mkb, mvb, w_mo_b, ts=_pick_tile(S, 512))
    y_p = _ffn(x2.reshape(B * S, D), row(norm_ffn[l]), w_gate_b, w_up_b, w_down_b,
               row(norm_final), ts=_pick_tile(B * S, 512)).reshape(B, S, D)

    xs = x_sample.reshape(T, D)
    qd_s, k_s, v_s, hq_s, kk_s, f_s, vh_s, gate_s = _sample_mix(
        xs, row(norm_mix[l]), w_in_b, lb2, cos_s, sin_s)
    mix_hg_s, hs_s = _sample_hgrn(hq_s, kk_s, f_s, vh_s, gate_s, row(hg_onorm[l]), state_hgrn[l])
    mix_da_s = _paged_attn(page_table, qd_s, k_s, v_s, lam_p, row(da_onorm[l]),
                           cache_k.reshape(depth * n_phys, page * DA_HEADS, LANES),
                           cache_v.reshape(depth * n_phys, page * DA_HEADS, LANES),
                           lam_init, n_pg=_pick_tile(page_table.shape[1], 16))
    MW = w_mk.shape[2]
    x2_s = _out_cross_sample(xs, mix_hg_s, mix_da_s, w_out_b, row(norm_mem_q[l]), w_mq_b,
                             cache_mem_k[l].reshape(T, -1, MW), cache_mem_v[l].reshape(T, -1, MW),
                             w_mo_b)
    y_s = _ffn(x2_s, row(norm_ffn[l]), w_gate_b, w_up_b, w_down_b, row(norm_final), ts=T)

    dk = LANES
    return (y_p, y_s.reshape(T, 1, D),
            hs_p[None],
            k_p.reshape(1, B, S, DA_HEADS, dk), v_p.reshape(1, B, S, DA_HEADS, dk),
            mk[None], mv[None],
            hs_s[None],
            k_s.reshape(1, T, 1, DA_HEADS, dk), v_s.reshape(1, T, 1, DA_HEADS, dk))
```

```python
import functools
import math

import jax
import jax.numpy as jnp
from jax import lax
from jax.experimental import pallas as pl
from jax.experimental.pallas import tpu as pltpu

F32 = jnp.float32
BF16 = jnp.bfloat16
EPS = 1e-6
ROPE_THETA = 10000.0

HG_HEADS = 4
DA_HEADS = 4
MEM_HEADS = 4
HG_CHUNK = 64
LANES = 128
BF16_ROWS = 16
VMEM_LIMIT_CAP = 56 << 20

_NT = (((1,), (1,)), ((), ()))
_TN = (((0,), (0,)), ((), ()))


def _dot(a, b):
    return jnp.dot(a, b, preferred_element_type=F32)


def _dot_nt(a, b):
    return lax.dot_general(a, b, _NT, preferred_element_type=F32)


def _dot_tn(a, b):
    return lax.dot_general(a, b, _TN, preferred_element_type=F32)


def _rms(x, g):
    ms = jnp.mean(x * x, axis=-1, keepdims=True)
    return x * lax.rsqrt(ms + EPS) * g


def _sigmoid(x):
    return 1.0 / (1.0 + jnp.exp(-x))


def _cparams(semantics, vmem_bytes):
    return pltpu.CompilerParams(
        dimension_semantics=semantics,
        vmem_limit_bytes=int(min(max(vmem_bytes, 16 << 20), VMEM_LIMIT_CAP)))


def _const_spec(shape):
    nd = len(shape)
    return pl.BlockSpec(shape, lambda *_: (0,) * nd, pipeline_mode=pl.Buffered(1))


def _lower_bound(lb_ref):
    a0 = lb_ref[0:1, :]
    a1 = lb_ref[1:2, :]
    m = jnp.maximum(a0, a1)
    e0 = jnp.exp(a0 - m)
    e1 = jnp.exp(a1 - m)
    return e0 / (e0 + e1)


def _rope(x, cos, sin_signed):
    n = x.shape[-1]
    lane = lax.broadcasted_iota(jnp.int32, x.shape, x.ndim - 1)
    swapped = jnp.where((lane & 63) < 32,
                        pltpu.roll(x, n - 32, x.ndim - 1),
                        pltpu.roll(x, 32, x.ndim - 1))
    return x * cos + swapped * sin_signed


def _lambda(lam_ref, lam_init):
    lp = lam_ref[...]
    s01 = jnp.sum(lp[0:1, :] * lp[1:2, :], axis=-1, keepdims=True)
    s23 = jnp.sum(lp[2:3, :] * lp[3:4, :], axis=-1, keepdims=True)
    return jnp.exp(s01) - jnp.exp(s23) + lam_init


def _mixer_sections(z, lb):
    w = z.shape[-1] // 7
    hq, zf, hi, hg, dq, dk, dv = (z[:, i * w:(i + 1) * w] for i in range(7))
    sig = _sigmoid(zf)
    logf = jnp.log(lb + (1.0 - lb) * sig)
    k_hg = (1.0 - lb) * (1.0 - sig)
    gate = hg * _sigmoid(hg)
    return hq, k_hg, logf, hi, gate, dq, dk, dv


def _head_rms(o, g):
    outs = []
    for h in range(o.shape[-1] // LANES):
        oh = o[:, h * LANES:(h + 1) * LANES]
        outs.append(_rms(oh, g))
    return jnp.concatenate(outs, axis=-1)


def _memkv_kernel(mem_ref, g_ref, wk_ref, wv_ref, k_ref, v_ref, kb_ref, vb_ref):
    m = _rms(mem_ref[0], g_ref[...]).astype(BF16)
    k = _dot(m, wk_ref[...])
    v = _dot(m, wv_ref[...])
    dh = k_ref.shape[-1]
    for h in range(MEM_HEADS):
        k_ref[0, :, h, :] = k[:, h * dh:(h + 1) * dh]
        v_ref[0, :, h, :] = v[:, h * dh:(h + 1) * dh]
    kb_ref[0] = k.astype(BF16)
    vb_ref[0] = v.astype(BF16)


def _mem_kv(mem, g, wk, wv):
    B, N, D = mem.shape
    W = wk.shape[1]
    dh = W // MEM_HEADS
    blk = lambda d: pl.BlockSpec((1, N, d), lambda b: (b, 0, 0))
    blk4 = pl.BlockSpec((1, N, MEM_HEADS, dh), lambda b: (b, 0, 0, 0))
    return pl.pallas_call(
        _memkv_kernel,
        grid=(B,),
        in_specs=[blk(D), _const_spec((1, D)), _const_spec((D, W)), _const_spec((D, W))],
        out_specs=[blk4, blk4, blk(W), blk(W)],
        out_shape=[jax.ShapeDtypeStruct((B, N, MEM_HEADS, dh), F32)] * 2
        + [jax.ShapeDtypeStruct((B, N, W), BF16)] * 2,
        compiler_params=_cparams(("arbitrary",), 40 << 20),
        name="mem_kv",
    )(mem, g, wk, wv)


def _hgrn_chunk(q, kk, lf, v, st_ref):
    C, W = q.shape
    t = lax.broadcasted_iota(jnp.int32, (C, W), 0)
    ti = lax.broadcasted_iota(jnp.int32, (C, C), 0)
    si = lax.broadcasted_iota(jnp.int32, (C, C), 1)
    n_lvl = int(math.log2(C))

    c = lf
    lvls = []
    for l in range(n_lvl):
        m = 1 << l
        upper = (t & m) != 0
        if m < 8:
            y = c
            for i in range(l):
                y = jnp.where((t & (1 << i)) == 0, pltpu.roll(y, C - (1 << i), 0), y)
            bc = jnp.where(upper, pltpu.roll(y, m, 0), y)
        else:
            pieces = []
            for j in range(C // (2 * m)):
                r = 2 * m * j + m - 1
                pieces.append(jnp.broadcast_to(c[r:r + 1, :], (2 * m, W)))
            bc = pieces[0] if len(pieces) == 1 else jnp.concatenate(pieces, axis=0)
        e = jnp.exp(jnp.where(upper, c, bc - c))
        lvls.append((jnp.where(upper, q, kk) * e).astype(BF16))
        c = c + jnp.where(upper, bc, 0.0)
    b = c
    b_last = b[C - 1:C, :]
    q_bf = q.astype(BF16)
    k_bf = kk.astype(BF16)
    v_bf = v.astype(BF16)
    q_state = (q * jnp.exp(b)).astype(BF16)
    k_state = (kk * jnp.exp(b_last - b)).astype(BF16)
    decay = jnp.exp(b_last)

    diff_bits = jnp.where(ti > si, ti ^ si, 0)
    keeps = [(diff_bits >> l) == 1 for l in range(n_lvl)]

    outs = []
    for h in range(W // LANES):
        sl = slice(h * LANES, (h + 1) * LANES)
        a = jnp.where(ti == si, _dot_nt(q_bf[:, sl], k_bf[:, sl]), 0.0)
        for l in range(n_lvl):
            r = lvls[l][:, sl]
            a = a + jnp.where(keeps[l], _dot_nt(r, r), 0.0)
        st = st_ref[h]
        o = _dot(a.astype(BF16), v_bf[:, sl]) + _dot_nt(q_state[:, sl], st.astype(BF16))
        st_ref[h] = st * decay[:, sl] + _dot_tn(v_bf[:, sl], k_state[:, sl])
        outs.append(o)
    return jnp.concatenate(outs, axis=-1)


def _prompt_mix_kernel(x_ref, g_ref, w_ref, lb_ref, cos_ref, sin_ref, onorm_ref,
                       qd_ref, k_ref, v_ref, kb_ref, va_ref, mix_ref, state_ref,
                       o_s, st_s):
    s = pl.program_id(1)
    last = pl.num_programs(1) - 1

    @pl.when(s == 0)
    def _():
        st_s[...] = jnp.zeros_like(st_s)

    h = _rms(x_ref[0], g_ref[...]).astype(BF16)

    ts = x_ref.shape[1]
    sec_w = w_ref.shape[1] // 7
    n_chunk = ts // HG_CHUNK
    done = [0]

    def project(sec):
        return _dot(h, w_ref[:, sec * sec_w:(sec + 1) * sec_w])

    def recur(upto):
        for ci in range(done[0], min(upto, n_chunk)):
            rows = slice(ci * HG_CHUNK, (ci + 1) * HG_CHUNK)
            o_s[rows, :] = _hgrn_chunk(hq[rows, :], k_hg[rows, :], logf[rows, :], hi[rows, :], st_s)
        done[0] = max(done[0], min(upto, n_chunk))

    def head_rows_out(ref, val):
        for hh in range(DA_HEADS):
            ref[0, pl.ds(hh, ts, stride=DA_HEADS), :] = val[:, hh * LANES:(hh + 1) * LANES]

    lb = _lower_bound(lb_ref)
    cos = cos_ref[...]
    sin = sin_ref[...]
    hq = project(0)
    sig = _sigmoid(project(1))
    logf = jnp.log(lb + (1.0 - lb) * sig)
    k_hg = (1.0 - lb) * (1.0 - sig)
    hi = project(2)
    hg = project(3)
    recur(n_chunk // 4)
    q_da = _rope(project(4), cos, sin)
    qd_ref[0] = (q_da * (math.log2(math.e) * float(LANES // 2) ** -0.5)).astype(BF16)
    recur(n_chunk // 2)
    k_da = _rope(project(5), cos, sin)
    head_rows_out(k_ref, k_da)
    kb_ref[0] = k_da.astype(BF16)
    recur(3 * n_chunk // 4)
    dv = project(6)
    head_rows_out(v_ref, dv)
    va_ref[0] = dv
    recur(n_chunk)
    gate = hg * _sigmoid(hg)
    mix_ref[0] = (_head_rms(o_s[...], onorm_ref[...]) * gate).astype(BF16)

    @pl.when(s == last)
    def _():
        for hh in range(HG_HEADS):
            state_ref[0, hh] = st_s[hh].T


def _prompt_mix(x, g, w_in, hg_lb, cos, sin, onorm, ts):
    B, S, D = x.shape
    W = w_in.shape[1] // 7
    tok = lambda d: pl.BlockSpec((1, ts, d), lambda b, s: (b, s, 0))
    tab = pl.BlockSpec((ts, W), lambda b, s: (s, 0))
    rows = pl.BlockSpec((1, ts * DA_HEADS, LANES), lambda b, s: (b, s, 0))
    state = pl.BlockSpec((1, HG_HEADS, LANES, LANES), lambda b, s: (b, 0, 0, 0))
    sds = jax.ShapeDtypeStruct
    return pl.pallas_call(
        _prompt_mix_kernel,
        grid=(B, S // ts),
        in_specs=[tok(D), _const_spec((1, D)), _const_spec(w_in.shape),
                  _const_spec(hg_lb.shape), tab, tab, _const_spec((1, LANES))],
        out_specs=[tok(W), rows, rows, tok(W), tok(W), tok(W), state],
        out_shape=[sds((B, S, W), BF16), sds((B, S * DA_HEADS, LANES), F32),
                   sds((B, S * DA_HEADS, LANES), F32), sds((B, S, W), BF16),
                   sds((B, S, W), F32), sds((B, S, W), BF16),
                   sds((B, HG_HEADS, LANES, LANES), F32)],
        scratch_shapes=[pltpu.VMEM((ts, W), F32), pltpu.VMEM((HG_HEADS, LANES, LANES), F32)],
        compiler_params=_cparams(("arbitrary", "arbitrary"), 48 << 20),
        name="prompt_mix",
    )(x, g, w_in, hg_lb, cos, sin, onorm)


def _diff_attn_kernel(q_ref, k_ref, v_ref, lam_ref, onorm_ref, o_ref,
                      qq_s, vt_s, m_s, acc_s, *, lam_init, tk):
    i = pl.program_id(2)
    tq = q_ref.shape[1]
    n_ck = tq // tk
    n_kv = v_ref.shape[1] // tk

    @pl.when(i == 0)
    def _():
        ones = jnp.ones((vt_s.shape[1] - LANES, tk), BF16)
        for jj in range(n_kv):
            vt_s[jj, 0:LANES, :] = v_ref[0, jj * tk:(jj + 1) * tk, :].T.astype(BF16)
            vt_s[jj, LANES:, :] = ones

    q = q_ref[0]
    lane = lax.broadcasted_iota(jnp.int32, q.shape, 1)
    zero = jnp.zeros_like(q)
    qq_s[0:tq, :] = jnp.where(lane < LANES // 2, q, zero)
    qq_s[tq:2 * tq, :] = jnp.where(lane >= LANES // 2, q, zero)
    m_s[...] = jnp.full(m_s.shape, -jnp.inf, F32)
    acc_s[...] = jnp.zeros(acc_s.shape, F32)

    def scores(j, c):
        k = k_ref[0, pl.ds(pl.multiple_of(j * tk, tk), tk), :]
        return _dot_nt(k, qq_s[c * tk:(c + 1) * tk, :])

    def softmax(j, c, tri, s):
        cols = slice(c * tk, (c + 1) * tk)
        if tri:
            key = lax.broadcasted_iota(jnp.int32, s.shape, 0)
            qry = lax.broadcasted_iota(jnp.int32, s.shape, 1)
            s = jnp.where(key <= qry, s, -jnp.inf)
        m_old = m_s[:, cols]
        m_new = jnp.maximum(m_old, jnp.max(s, axis=0, keepdims=True))
        alpha = jnp.exp2(m_old - m_new)
        p = jnp.exp2(s - m_new)
        m_s[:, cols] = m_new
        return cols, alpha, _dot(vt_s[j], p.astype(BF16))

    def accumulate(cols, alpha, pv):
        acc_s[:, cols] = alpha * acc_s[:, cols] + pv

    def run(work):
        s = scores(*work[0][:2])
        pending = None
        for n, (j, c, tri) in enumerate(work):
            s_next = scores(*work[n + 1][:2]) if n + 1 < len(work) else None
            done = softmax(j, c, tri, s)
            if pending is not None:
                accumulate(*pending)
            pending = done
            s = s_next
        accumulate(*pending)

    def full_block(j, carry):
        run([(j, c, False) for c in range(2 * n_ck)])
        return carry

    lax.fori_loop(0, i * n_ck, full_block, 0)
    run([(i * n_ck + d, mp * n_ck + cq, cq == d)
         for d in range(n_ck) for mp in range(2) for cq in range(d, n_ck)])

    lam = _lambda(lam_ref, lam_init)
    a = acc_s[0:LANES, :] * (1.0 / acc_s[LANES:LANES + 1, :])
    o_t = a[:, 0:tq] - lam * a[:, tq:2 * tq]
    ms = jnp.mean(o_t * o_t, axis=0, keepdims=True)
    y = (o_t * lax.rsqrt(ms + EPS)).T
    o_ref[0] = (y * onorm_ref[...] * (1.0 - lam_init)).astype(BF16)


def _diff_attn_prompt(qd, kb, v, da_lambda, onorm, lam_init, tq, tk):
    B, S, W = qd.shape
    H = W // LANES
    qspec = pl.BlockSpec((1, tq, LANES), lambda b, h, i: (b, i, h))
    kvspec = pl.BlockSpec((1, S, LANES), lambda b, h, i: (b, 0, h))
    return pl.pallas_call(
        functools.partial(_diff_attn_kernel, lam_init=lam_init, tk=tk),
        grid=(B, H, S // tq),
        in_specs=[qspec, kvspec, kvspec, _const_spec(da_lambda.shape), _const_spec((1, LANES))],
        out_specs=qspec,
        out_shape=jax.ShapeDtypeStruct((B, S, W), BF16),
        scratch_shapes=[pltpu.VMEM((2 * tq, LANES), BF16),
                        pltpu.VMEM((S // tk, LANES + BF16_ROWS, tk), BF16),
                        pltpu.VMEM((1, 2 * tq), F32),
                        pltpu.VMEM((LANES + BF16_ROWS, 2 * tq), F32)],
        compiler_params=_cparams(("arbitrary",) * 3, 32 << 20),
        name="diff_attn_prompt",
    )(qd, kb, v, da_lambda, onorm)


def _cross_attn_heads(q, mk_head, mv_head):
    dh = q.shape[-1] // MEM_HEADS
    outs = []
    for h in range(MEM_HEADS):
        s = _dot_nt(q[:, h * dh:(h + 1) * dh], mk_head(h))
        m = jnp.max(s, axis=-1, keepdims=True)
        p = jnp.exp(s - m)
        l = jnp.sum(p, axis=-1, keepdims=True)
        outs.append(_dot((p / l).astype(BF16), mv_head(h)))
    return jnp.concatenate(outs, axis=-1)


def _out_cross_kernel(x_ref, mhg_ref, mda_ref, wo_ref, gq_ref, wq_ref, mk_ref, mv_ref,
                      wmo_ref, x2_ref):
    w = mhg_ref.shape[-1]
    x1 = x_ref[0] + _dot(mhg_ref[0], wo_ref[0:w, :]) + _dot(mda_ref[0], wo_ref[w:2 * w, :])
    hq = _rms(x1, gq_ref[...]).astype(BF16)
    dh = wq_ref.shape[1] // MEM_HEADS
    q = (_dot(hq, wq_ref[...]) * (float(dh) ** -0.5)).astype(BF16)
    o = _cross_attn_heads(q, lambda h: mk_ref[0, :, h * dh:(h + 1) * dh],
                          lambda h: mv_ref[0, :, h * dh:(h + 1) * dh])
    x2_ref[0] = x1 + _dot(o.astype(BF16), wmo_ref[...])


def _out_cross_prompt(x, mhg, mda, w_out, gq, w_mq, mkb, mvb, w_mo, ts):
    B, S, D = x.shape
    W = mhg.shape[-1]
    N, MW = mkb.shape[1], mkb.shape[2]
    tok = lambda d: pl.BlockSpec((1, ts, d), lambda b, s: (b, s, 0))
    mem = pl.BlockSpec((1, N, MW), lambda b, s: (b, 0, 0))
    return pl.pallas_call(
        _out_cross_kernel,
        grid=(B, S // ts),
        in_specs=[tok(D), tok(W), tok(W), _const_spec(w_out.shape), _const_spec((1, D)),
                  _const_spec(w_mq.shape), mem, mem, _const_spec(w_mo.shape)],
        out_specs=tok(D),
        out_shape=jax.ShapeDtypeStruct((B, S, D), F32),
        compiler_params=_cparams(("arbitrary", "arbitrary"), 48 << 20),
        name="out_cross_prompt",
    )(x, mhg, mda, w_out, gq, w_mq, mkb, mvb, w_mo)


def _out_cross_sample_kernel(x_ref, mhg_ref, mda_ref, wo_ref, gq_ref, wq_ref, mk_ref, mv_ref,
                             wmo_ref, x2_ref, x1_s, q_s, o_s):
    b = pl.program_id(0)
    w = mhg_ref.shape[-1]

    @pl.when(b == 0)
    def _():
        x1 = x_ref[...] + _dot(mhg_ref[...], wo_ref[0:w, :]) + _dot(mda_ref[...], wo_ref[w:2 * w, :])
        x1_s[...] = x1
        hq = _rms(x1, gq_ref[...]).astype(BF16)
        dh = wq_ref.shape[1] // MEM_HEADS
        q_s[...] = _dot(hq, wq_ref[...]) * (float(dh) ** -0.5)

    q = q_s[pl.ds(b, 1), :].astype(BF16)
    dh = q.shape[-1] // MEM_HEADS
    o_s[pl.ds(b, 1), :] = _cross_attn_heads(
        q, lambda h: mk_ref[0, :, h * dh:(h + 1) * dh].astype(BF16),
        lambda h: mv_ref[0, :, h * dh:(h + 1) * dh].astype(BF16))

    @pl.when(b == pl.num_programs(0) - 1)
    def _():
        x2_ref[...] = x1_s[...] + _dot(o_s[...].astype(BF16), wmo_ref[...])


def _out_cross_sample(x, mhg, mda, w_out, gq, w_mq, mem_k, mem_v, w_mo):
    T, D = x.shape
    N, MW = mem_k.shape[1], mem_k.shape[2]
    mem = pl.BlockSpec((1, N, MW), lambda b: (b, 0, 0))
    return pl.pallas_call(
        _out_cross_sample_kernel,
        grid=(T,),
        in_specs=[_const_spec(x.shape), _const_spec(mhg.shape), _const_spec(mda.shape),
                  _const_spec(w_out.shape), _const_spec((1, D)), _const_spec(w_mq.shape),
                  mem, mem, _const_spec(w_mo.shape)],
        out_specs=pl.BlockSpec((T, D), lambda b: (0, 0)),
        out_shape=jax.ShapeDtypeStruct((T, D), F32),
        scratch_shapes=[pltpu.VMEM((T, D), F32), pltpu.VMEM((T, MW), F32), pltpu.VMEM((T, MW), F32)],
        compiler_params=_cparams(("arbitrary",), 32 << 20),
        name="out_cross_sample",
    )(x, mhg, mda, w_out, gq, w_mq, mem_k, mem_v, w_mo)


def _ffn_kernel(x_ref, gf_ref, wg_ref, wu_ref, wd_ref, gfin_ref, y_ref):
    x = x_ref[...]
    h = _rms(x, gf_ref[...]).astype(BF16)
    g = _dot(h, wg_ref[...])
    u = _dot(h, wu_ref[...])
    a = (g * _sigmoid(g) * u).astype(BF16)
    x3 = x + _dot(a, wd_ref[...])
    y_ref[...] = _rms(x3, gfin_ref[...])


def _ffn(x, gf, wg, wu, wd, gfin, ts):
    T, D = x.shape
    tok = pl.BlockSpec((ts, D), lambda i: (i, 0))
    return pl.pallas_call(
        _ffn_kernel,
        grid=(T // ts,),
        in_specs=[tok, _const_spec((1, D)), _const_spec(wg.shape), _const_spec(wu.shape),
                  _const_spec(wd.shape), _const_spec((1, D))],
        out_specs=tok,
        out_shape=jax.ShapeDtypeStruct((T, D), F32),
        compiler_params=_cparams(("arbitrary",), VMEM_LIMIT_CAP),
        name="ffn",
    )(x, gf, wg, wu, wd, gfin)


def _sample_mix_kernel(x_ref, g_ref, w_ref, lb_ref, cos_ref, sin_ref,
                       qd_ref, k_ref, v_ref, hq_ref, kk_ref, f_ref, vh_ref, gate_ref):
    h = _rms(x_ref[...], g_ref[...]).astype(BF16)
    z = _dot(h, w_ref[...])
    lb = _lower_bound(lb_ref)
    hq, k_hg, logf, hi, gate, dq, dk, dv = _mixer_sections(z, lb)
    cos = cos_ref[...]
    sin = sin_ref[...]
    qd_ref[...] = _rope(dq, cos, sin) * (float(LANES // 2) ** -0.5)
    k_ref[...] = _rope(dk, cos, sin)
    v_ref[...] = dv
    hq_ref[...] = hq
    kk_ref[...] = k_hg
    f_ref[...] = jnp.exp(logf)
    vh_ref[...] = hi
    gate_ref[...] = gate


def _sample_mix(x, g, w_in, hg_lb, cos, sin):
    T, D = x.shape
    W = w_in.shape[1] // 7
    sds = jax.ShapeDtypeStruct
    return pl.pallas_call(
        _sample_mix_kernel,
        out_shape=[sds((T, W), F32)] * 8,
        compiler_params=_cparams(None, 32 << 20),
        name="sample_mix",
    )(x, g, w_in, hg_lb, cos, sin)


def _sample_hgrn_kernel(hq_ref, kk_ref, f_ref, vh_ref, gate_ref, onorm_ref, s0_ref,
                        mix_ref, s1_ref):
    nb = s0_ref.shape[0]
    g0 = pl.program_id(0) * nb
    W = hq_ref.shape[-1]

    def columns(ref, h):
        rows = ref[pl.ds(pl.multiple_of(g0, nb), nb), h * LANES:(h + 1) * LANES]
        pad = jnp.zeros((LANES - nb, LANES), F32)
        return jnp.concatenate([rows, pad], axis=0).T

    o_rows = []
    for h in range(HG_HEADS):
        q_t, k_t, f_t = columns(hq_ref, h), columns(kk_ref, h), columns(f_ref, h)
        v_rows = vh_ref[pl.ds(pl.multiple_of(g0, nb), nb), h * LANES:(h + 1) * LANES]
        o_h = []
        for j in range(nb):
            bcast = lambda tile: jnp.broadcast_to(tile[:, j:j + 1], (LANES, LANES))
            s_new = bcast(f_t) * s0_ref[j, h] + bcast(k_t) * v_rows[j:j + 1, :]
            s1_ref[j, h] = s_new
            o_h.append(jnp.sum(bcast(q_t) * s_new, axis=0, keepdims=True))
        o_rows.append(jnp.concatenate(o_h, axis=0))
    o = jnp.concatenate(o_rows, axis=-1)
    gate = gate_ref[pl.ds(pl.multiple_of(g0, nb), nb), :]
    mix_ref[...] = (_head_rms(o, onorm_ref[...]) * gate).astype(BF16)


def _sample_hgrn(hq, kk, f, vh, gate, onorm, s0, nb=8):
    T, W = hq.shape
    st = pl.BlockSpec((nb, HG_HEADS, LANES, LANES), lambda i: (i, 0, 0, 0))
    full = _const_spec((T, W))
    return pl.pallas_call(
        _sample_hgrn_kernel,
        grid=(T // nb,),
        in_specs=[full] * 5 + [_const_spec((1, LANES)), st],
        out_specs=[pl.BlockSpec((nb, W), lambda i: (i, 0)), st],
        out_shape=[jax.ShapeDtypeStruct((T, W), BF16), jax.ShapeDtypeStruct(s0.shape, F32)],
        compiler_params=_cparams(("arbitrary",), 32 << 20),
        name="sample_hgrn",
    )(hq, kk, f, vh, gate, onorm, s0)


def _decode_seed(q, k_new, v_new, state):
    q8_s, m_s, l_s, acc_s = state
    H = DA_HEADS

    def head_rows(x):
        return jnp.concatenate([x[:, h * LANES:(h + 1) * LANES] for h in range(H)] * 2, axis=0)

    row = lax.broadcasted_iota(jnp.int32, (2 * H, LANES), 0)
    lane = lax.broadcasted_iota(jnp.int32, (2 * H, LANES), 1)
    q8 = jnp.where((row // H) == (lane // (LANES // 2)), head_rows(q), 0.0)
    q8_s[...] = q8.astype(BF16)
    m_s[...] = jnp.sum(q8 * head_rows(k_new), axis=-1, keepdims=True)
    l_s[...] = jnp.ones(l_s.shape, F32)
    acc_s[...] = head_rows(v_new)


def _decode_pages(k_pages, v_pages, state):
    q8_s, m_s, l_s, acc_s = state
    H = DA_HEADS
    q8 = q8_s[...]
    s = jnp.concatenate([_dot_nt(q8, k.astype(BF16)) for k in k_pages], axis=-1)
    row = lax.broadcasted_iota(jnp.int32, s.shape, 0)
    col = lax.broadcasted_iota(jnp.int32, s.shape, 1)
    s = jnp.where((col % H) == (row % H), s, -jnp.inf)
    m_old = m_s[...]
    m_new = jnp.maximum(m_old, jnp.max(s, axis=-1, keepdims=True))
    alpha = jnp.exp(m_old - m_new)
    pr = jnp.exp(s - m_new)
    l_s[...] = alpha * l_s[...] + jnp.sum(pr, axis=-1, keepdims=True)
    pr = pr.astype(BF16)
    rows = k_pages[0].shape[0]
    pv = _dot(pr[:, 0:rows], v_pages[0].astype(BF16))
    for p in range(1, len(v_pages)):
        pv = pv + _dot(pr[:, p * rows:(p + 1) * rows], v_pages[p].astype(BF16))
    acc_s[...] = alpha * acc_s[...] + pv
    m_s[...] = m_new


def _decode_finish(lam_ref, onorm_ref, lam_init, state):
    _, _, l_s, acc_s = state
    H = DA_HEADS
    lam = _lambda(lam_ref, lam_init)
    o = acc_s[...] / l_s[...]
    y = _rms(o[0:H] - lam * o[H:2 * H], onorm_ref[...]) * (1.0 - lam_init)
    return jnp.concatenate([y[h:h + 1, :] for h in range(H)], axis=-1).astype(BF16)


def _paged_attn_kernel(pt_ref, q_ref, kn_ref, vn_ref, lam_ref, onorm_ref, *refs,
                       lam_init, n_pg):
    k_refs = refs[:n_pg]
    v_refs = refs[n_pg:2 * n_pg]
    o_ref = refs[2 * n_pg]
    state = refs[2 * n_pg + 1:]
    j = pl.program_id(1)

    @pl.when(j == 0)
    def _():
        _decode_seed(q_ref[0], kn_ref[0], vn_ref[0], state)

    _decode_pages([r[0] for r in k_refs], [r[0] for r in v_refs], state)

    @pl.when(j == pl.num_programs(1) - 1)
    def _():
        o_ref[0] = _decode_finish(lam_ref, onorm_ref, lam_init, state)


def _paged_attn(page_table, qd, k_new, v_new, da_lambda, onorm, cache_k, cache_v, lam_init, n_pg):
    T, n_pages = page_table.shape
    n_phys, rows, dk = cache_k.shape
    W = qd.shape[-1]
    row = pl.BlockSpec((1, 1, W), lambda b, j, pt: (b, 0, 0))

    def page_spec(p):
        return pl.BlockSpec((1, rows, dk), lambda b, j, pt: (pt[b, j * n_pg + p], 0, 0))

    const = lambda shape: pl.BlockSpec(shape, lambda b, j, pt: (0,) * len(shape))
    grid_spec = pltpu.PrefetchScalarGridSpec(
        num_scalar_prefetch=1,
        grid=(T, n_pages // n_pg),
        in_specs=[row, row, row, const(da_lambda.shape), const((1, LANES))]
        + [page_spec(p) for p in range(n_pg)] * 2,
        out_specs=row,
        scratch_shapes=[pltpu.VMEM((2 * DA_HEADS, LANES), BF16),
                        pltpu.VMEM((2 * DA_HEADS, 1), F32),
                        pltpu.VMEM((2 * DA_HEADS, 1), F32),
                        pltpu.VMEM((2 * DA_HEADS, LANES), F32)],
    )
    out = pl.pallas_call(
        functools.partial(_paged_attn_kernel, lam_init=lam_init, n_pg=n_pg),
        grid_spec=grid_spec,
        out_shape=jax.ShapeDtypeStruct((T, 1, W), BF16),
        compiler_params=_cparams(("arbitrary", "arbitrary"), 48 << 20),
        name="paged_diff_attn",
    )(page_table, qd.reshape(T, 1, W), k_new.reshape(T, 1, W), v_new.reshape(T, 1, W),
      da_lambda, onorm, *([cache_k] * n_pg), *([cache_v] * n_pg))
    return out.reshape(T, W)


def _ffn_decode_kernel(pt_ref, x_ref, gf_ref, wg_ref, wu_ref, wd_ref, gfin_ref,
                       q_ref, kn_ref, vn_ref, lam_ref, onorm_ref, ck_hbm, cv_hbm,
                       y_ref, o_ref, kbuf, vbuf, sem, q8_s, m_s, l_s, acc_s,
                       *, lam_init, n_pg, sps, col_bounds):
    i = pl.program_id(0)
    n_steps = pl.num_programs(0)
    state = (q8_s, m_s, l_s, acc_s)
    rows = kbuf.shape[1] // n_pg
    gps = pt_ref.shape[1] // n_pg
    groups = [(ls, g) for ls in range(sps) for g in range(gps)]
    n_grp = len(groups)

    def copies(b, g, slot):
        out = []
        for p in range(n_pg):
            page = pt_ref[b, g * n_pg + p]
            dst = pl.ds(p * rows, rows)
            out.append(pltpu.make_async_copy(ck_hbm.at[page], kbuf.at[slot, dst], sem.at[slot, 0]))
            out.append(pltpu.make_async_copy(cv_hbm.at[page], vbuf.at[slot, dst], sem.at[slot, 1]))
        return out

    def start(b, g, slot):
        for cp in copies(b, g, slot):
            cp.start()

    @pl.when(i == 0)
    def _():
        start(0, 0, 0)

    def fold_group(n):
        ls, g = groups[n]
        b = i * sps + ls
        slot = n % 2
        if n + 1 < n_grp:
            ls2, g2 = groups[n + 1]
            start(i * sps + ls2, g2, 1 - slot)
        else:
            @pl.when(i + 1 < n_steps)
            def _():
                start((i + 1) * sps, 0, 1 - slot)
        for cp in copies(b, g, slot):
            cp.wait()
        if g == 0:
            _decode_seed(q_ref[b], kn_ref[b], vn_ref[b], state)
        _decode_pages([kbuf[slot, p * rows:(p + 1) * rows, :] for p in range(n_pg)],
                      [vbuf[slot, p * rows:(p + 1) * rows, :] for p in range(n_pg)], state)
        if g == gps - 1:
            o_ref[b] = _decode_finish(lam_ref, onorm_ref, lam_init, state)

    x = x_ref[...]
    h = _rms(x, gf_ref[...]).astype(BF16)
    n_piece = len(col_bounds) - 1
    acc = None
    for k in range(n_piece):
        lo, hi = col_bounds[k], col_bounds[k + 1]
        g = _dot(h, wg_ref[:, lo:hi])
        u = _dot(h, wu_ref[:, lo:hi])
        part = _dot((g * _sigmoid(g) * u).astype(BF16), wd_ref[lo:hi, :])
        acc = part if acc is None else acc + part
        for n in range(k * n_grp // n_piece, (k + 1) * n_grp // n_piece):
            fold_group(n)
    y_ref[...] = _rms(x + acc, gfin_ref[...])


def _ffn_decode(x, gf, wg, wu, wd, gfin, ts, page_table, qd, k_new, v_new, da_lambda, onorm,
                cache_k, cache_v, lam_init, n_pg):
    T, D = x.shape
    F = wg.shape[1]
    Ts, n_pages = page_table.shape
    n_phys, rows, dk = cache_k.shape
    W = qd.shape[-1]
    n_tiles = T // ts
    sps = Ts // n_tiles
    mxu_n = 2 * LANES
    piece = max(mxu_n, (F // 4) // mxu_n * mxu_n)
    col_bounds = tuple(range(0, F, piece)) + (F,)
    tok = pl.BlockSpec((ts, D), lambda i, pt: (i, 0))
    const = lambda shape: pl.BlockSpec(shape, lambda i, pt: (0,) * len(shape),
                                       pipeline_mode=pl.Buffered(1))
    sample = const((Ts, 1, W))
    hbm = pl.BlockSpec(memory_space=pl.ANY)
    grid_spec = pltpu.PrefetchScalarGridSpec(
        num_scalar_prefetch=1,
        grid=(n_tiles,),
        in_specs=[tok, const((1, D)), const(wg.shape), const(wu.shape), const(wd.shape),
                  const((1, D)), sample, sample, sample, const(da_lambda.shape),
                  const((1, LANES)), hbm, hbm],
        out_specs=[tok, pl.BlockSpec((Ts, 1, W), lambda i, pt: (0, 0, 0))],
        scratch_shapes=[pltpu.VMEM((2, n_pg * rows, dk), F32),
                        pltpu.VMEM((2, n_pg * rows, dk), F32),
                        pltpu.SemaphoreType.DMA((2, 2)),
                        pltpu.VMEM((2 * DA_HEADS, LANES), BF16),
                        pltpu.VMEM((2 * DA_HEADS, 1), F32),
                        pltpu.VMEM((2 * DA_HEADS, 1), F32),
                        pltpu.VMEM((2 * DA_HEADS, LANES), F32)],
    )
    y, o = pl.pallas_call(
        functools.partial(_ffn_decode_kernel, lam_init=lam_init, n_pg=n_pg, sps=sps,
                          col_bounds=col_bounds),
        grid_spec=grid_spec,
        out_shape=[jax.ShapeDtypeStruct((T, D), F32), jax.ShapeDtypeStruct((Ts, 1, W), BF16)],
        compiler_params=_cparams(("arbitrary",), VMEM_LIMIT_CAP),
        name="ffn_decode",
    )(page_table, x, gf, wg, wu, wd, gfin, qd.reshape(Ts, 1, W), k_new.reshape(Ts, 1, W),
      v_new.reshape(Ts, 1, W), da_lambda, onorm, cache_k, cache_v)
    return y, o.reshape(Ts, W)


def _rope_tables(pos, dh, n_maps):
    inv = ROPE_THETA ** (-jnp.arange(0, dh, 2, dtype=F32) / dh)
    ang = pos.astype(F32)[:, None] * inv[None, :]
    cos = jnp.cos(ang)
    sin = jnp.sin(ang)
    cos = jnp.tile(jnp.concatenate([cos, cos], axis=-1), (1, n_maps))
    sin = jnp.tile(jnp.concatenate([-sin, sin], axis=-1), (1, n_maps))
    return cos, sin


def _pick_tile(n, pref):
    t = min(n, pref)
    while n % t:
        t //= 2
    return t


def kernel(x_prompt, x_sample, mem_prompt, cache_k, cache_v, cache_mem_k, cache_mem_v, state_hgrn, page_table, norm_mix, w_in, hg_lb, hg_onorm, da_lambda, da_onorm, w_out, norm_mem_q, norm_mem_kv, w_mq, w_mk, w_mv, w_mo, norm_ffn, w_gate, w_up, w_down, norm_final):
    B, S, D = x_prompt.shape
    T = x_sample.shape[0]
    depth = w_in.shape[0]
    assert depth == 1 and x_sample.shape[1] == 1
    l = 0
    lam_init = 0.8 - 0.6 * math.exp(-0.3 * l)
    n_phys, page = cache_k.shape[1], cache_k.shape[2]
    past_len = page_table.shape[1] * page
    W = w_in.shape[2] // 7

    bf = lambda w: w.astype(BF16)
    w_in_b, w_out_b = bf(w_in[l]), bf(w_out[l])
    w_mq_b, w_mk_b, w_mv_b, w_mo_b = bf(w_mq[l]), bf(w_mk[l]), bf(w_mv[l]), bf(w_mo[l])
    w_gate_b, w_up_b, w_down_b = bf(w_gate[l]), bf(w_up[l]), bf(w_down[l])
    row = lambda g: g.reshape(1, -1)
    lb2 = hg_lb[l:l + 2]
    lam_p = da_lambda[l]

    cos_p, sin_p = _rope_tables(jnp.arange(S), LANES // 2, 2 * DA_HEADS)
    cos_s, sin_s = _rope_tables(past_len + jnp.arange(1), LANES // 2, 2 * DA_HEADS)
    cos_s = jnp.broadcast_to(cos_s, (T, W))
    sin_s = jnp.broadcast_to(sin_s, (T, W))

    mk, mv, mkb, mvb = _mem_kv(mem_prompt, row(norm_mem_kv[l]), w_mk_b, w_mv_b)
    qd, k_p, v_p, kb, v_att, mix_hg, hs_p = _prompt_mix(
        x_prompt, row(norm_mix[l]), w_in_b, lb2, cos_p, sin_p, row(hg_onorm[l]),
        ts=_pick_tile(S, 256))
    mix_da = _diff_attn_prompt(qd, kb, v_att, lam_p, row(da_onorm[l]), lam_init,
                               tq=_pick_tile(S, 1024), tk=_pick_tile(S, 256))
    x2 = _out_cross_prompt(x_prompt, mix_hg, mix_da, w_out_b, row(norm_mem_q[l]), w_mq_b,
                           mkb, mvb, w_mo_b, ts=_pick_tile(S, 512))

    xs = x_sample.reshape(T, D)
    qd_s, k_s, v_s, hq_s, kk_s, f_s, vh_s, gate_s = _sample_mix(
        xs, row(norm_mix[l]), w_in_b, lb2, cos_s, sin_s)
    mix_hg_s, hs_s = _sample_hgrn(hq_s, kk_s, f_s, vh_s, gate_s, row(hg_onorm[l]), state_hgrn[l])
    ck = cache_k.reshape(depth * n_phys, page * DA_HEADS, LANES)
    cv = cache_v.reshape(depth * n_phys, page * DA_HEADS, LANES)
    n_pages = page_table.shape[1]
    ffn_ts = _pick_tile(B * S, 512)
    n_tiles = (B * S) // ffn_ts
    n_pg = _pick_tile(n_pages, 8)
    ffn_args = (row(norm_ffn[l]), w_gate_b, w_up_b, w_down_b, row(norm_final))
    if T % n_tiles == 0 and ((T // n_tiles) * (n_pages // n_pg)) % 2 == 0:
        y_p, mix_da_s = _ffn_decode(x2.reshape(B * S, D), *ffn_args, ffn_ts, page_table, qd_s, k_s,
                                    v_s, lam_p, row(da_onorm[l]), ck, cv, lam_init, n_pg)
    else:
        y_p = _ffn(x2.reshape(B * S, D), *ffn_args, ts=ffn_ts)
        mix_da_s = _paged_attn(page_table, qd_s, k_s, v_s, lam_p, row(da_onorm[l]), ck, cv,
                               lam_init, n_pg=_pick_tile(n_pages, 16))
    y_p = y_p.reshape(B, S, D)
    MW = w_mk.shape[2]
    x2_s = _out_cross_sample(xs, mix_hg_s, mix_da_s, w_out_b, row(norm_mem_q[l]), w_mq_b,
                             cache_mem_k[l].reshape(T, -1, MW), cache_mem_v[l].reshape(T, -1, MW),
                             w_mo_b)
    y_s = _ffn(x2_s, row(norm_ffn[l]), w_gate_b, w_up_b, w_down_b, row(norm_final), ts=T)

    dk = LANES
    return (y_p, y_s.reshape(T, 1, D),
            hs_p[None],
            k_p.reshape(1, B, S, DA_HEADS, dk), v_p.reshape(1, B, S, DA_HEADS, dk),
            mk[None], mv[None],
            hs_s[None],
            k_s.reshape(1, T, 1, DA_HEADS, dk), v_s.reshape(1, T, 1, DA_HEADS, dk))
```

```python
import functools
import math

import jax
import jax.numpy as jnp
from jax import lax
from jax.experimental import pallas as pl
from jax.experimental.pallas import tpu as pltpu

F32 = jnp.float32
BF16 = jnp.bfloat16
EPS = 1e-6
ROPE_THETA = 10000.0

HG_HEADS = 4
DA_HEADS = 4
MEM_HEADS = 4
HG_CHUNK = 64
LANES = 128
BF16_ROWS = 16
VMEM_LIMIT_CAP = 56 << 20

_NT = (((1,), (1,)), ((), ()))
_TN = (((0,), (0,)), ((), ()))


def _dot(a, b):
    return jnp.dot(a, b, preferred_element_type=F32)


def _dot_nt(a, b):
    return lax.dot_general(a, b, _NT, preferred_element_type=F32)


def _dot_tn(a, b):
    return lax.dot_general(a, b, _TN, preferred_element_type=F32)


def _rms(x, g):
    ms = jnp.mean(x * x, axis=-1, keepdims=True)
    return x * lax.rsqrt(ms + EPS) * g


def _sigmoid(x):
    return 1.0 / (1.0 + jnp.exp(-x))


def _cparams(semantics, vmem_bytes):
    return pltpu.CompilerParams(
        dimension_semantics=semantics,
        vmem_limit_bytes=int(min(max(vmem_bytes, 16 << 20), VMEM_LIMIT_CAP)))


def _const_spec(shape):
    nd = len(shape)
    return pl.BlockSpec(shape, lambda *_: (0,) * nd, pipeline_mode=pl.Buffered(1))


def _lower_bound(lb_ref):
    a0 = lb_ref[0:1, :]
    a1 = lb_ref[1:2, :]
    m = jnp.maximum(a0, a1)
    e0 = jnp.exp(a0 - m)
    e1 = jnp.exp(a1 - m)
    return e0 / (e0 + e1)


def _rope(x, cos, sin_signed):
    n = x.shape[-1]
    lane = lax.broadcasted_iota(jnp.int32, x.shape, x.ndim - 1)
    swapped = jnp.where((lane & 63) < 32,
                        pltpu.roll(x, n - 32, x.ndim - 1),
                        pltpu.roll(x, 32, x.ndim - 1))
    return x * cos + swapped * sin_signed


def _lambda(lam_ref, lam_init):
    lp = lam_ref[...]
    s01 = jnp.sum(lp[0:1, :] * lp[1:2, :], axis=-1, keepdims=True)
    s23 = jnp.sum(lp[2:3, :] * lp[3:4, :], axis=-1, keepdims=True)
    return jnp.exp(s01) - jnp.exp(s23) + lam_init


def _mixer_sections(z, lb):
    w = z.shape[-1] // 7
    hq, zf, hi, hg, dq, dk, dv = (z[:, i * w:(i + 1) * w] for i in range(7))
    sig = _sigmoid(zf)
    logf = jnp.log(lb + (1.0 - lb) * sig)
    k_hg = (1.0 - lb) * (1.0 - sig)
    gate = hg * _sigmoid(hg)
    return hq, k_hg, logf, hi, gate, dq, dk, dv


def _head_rms(o, g):
    outs = []
    for h in range(o.shape[-1] // LANES):
        oh = o[:, h * LANES:(h + 1) * LANES]
        outs.append(_rms(oh, g))
    return jnp.concatenate(outs, axis=-1)


def _memkv_kernel(mem_ref, g_ref, wk_ref, wv_ref, k_ref, v_ref, kb_ref, vb_ref):
    m = _rms(mem_ref[0], g_ref[...]).astype(BF16)
    k = _dot(m, wk_ref[...])
    v = _dot(m, wv_ref[...])
    dh = k_ref.shape[-1]
    for h in range(MEM_HEADS):
        k_ref[0, :, h, :] = k[:, h * dh:(h + 1) * dh]
        v_ref[0, :, h, :] = v[:, h * dh:(h + 1) * dh]
    kb_ref[0] = k.astype(BF16)
    vb_ref[0] = v.astype(BF16)


def _mem_kv(mem, g, wk, wv):
    B, N, D = mem.shape
    W = wk.shape[1]
    dh = W // MEM_HEADS
    blk = lambda d: pl.BlockSpec((1, N, d), lambda b: (b, 0, 0))
    blk4 = pl.BlockSpec((1, N, MEM_HEADS, dh), lambda b: (b, 0, 0, 0))
    return pl.pallas_call(
        _memkv_kernel,
        grid=(B,),
        in_specs=[blk(D), _const_spec((1, D)), _const_spec((D, W)), _const_spec((D, W))],
        out_specs=[blk4, blk4, blk(W), blk(W)],
        out_shape=[jax.ShapeDtypeStruct((B, N, MEM_HEADS, dh), F32)] * 2
        + [jax.ShapeDtypeStruct((B, N, W), BF16)] * 2,
        compiler_params=_cparams(("arbitrary",), 40 << 20),
        name="mem_kv",
    )(mem, g, wk, wv)


def _hgrn_chunk(q, kk, lf, v, st_ref):
    C, W = q.shape
    t = lax.broadcasted_iota(jnp.int32, (C, W), 0)
    ti = lax.broadcasted_iota(jnp.int32, (C, C), 0)
    si = lax.broadcasted_iota(jnp.int32, (C, C), 1)
    n_lvl = int(math.log2(C))

    c = lf
    lvls = []
    for l in range(n_lvl):
        m = 1 << l
        upper = (t & m) != 0
        if m < 8:
            y = c
            for i in range(l):
                y = jnp.where((t & (1 << i)) == 0, pltpu.roll(y, C - (1 << i), 0), y)
            bc = jnp.where(upper, pltpu.roll(y, m, 0), y)
        else:
            pieces = []
            for j in range(C // (2 * m)):
                r = 2 * m * j + m - 1
                pieces.append(jnp.broadcast_to(c[r:r + 1, :], (2 * m, W)))
            bc = pieces[0] if len(pieces) == 1 else jnp.concatenate(pieces, axis=0)
        e = jnp.exp(jnp.where(upper, c, bc - c))
        lvls.append((jnp.where(upper, q, kk) * e).astype(BF16))
        c = c + jnp.where(upper, bc, 0.0)
    b = c
    b_last = b[C - 1:C, :]
    q_bf = q.astype(BF16)
    k_bf = kk.astype(BF16)
    v_bf = v.astype(BF16)
    q_state = (q * jnp.exp(b)).astype(BF16)
    k_state = (kk * jnp.exp(b_last - b)).astype(BF16)
    decay = jnp.exp(b_last)

    diff_bits = jnp.where(ti > si, ti ^ si, 0)
    keeps = [(diff_bits >> l) == 1 for l in range(n_lvl)]

    outs = []
    for h in range(W // LANES):
        sl = slice(h * LANES, (h + 1) * LANES)
        a = jnp.where(ti == si, _dot_nt(q_bf[:, sl], k_bf[:, sl]), 0.0)
        for l in range(n_lvl):
            r = lvls[l][:, sl]
            a = a + jnp.where(keeps[l], _dot_nt(r, r), 0.0)
        st = st_ref[h]
        o = _dot(a.astype(BF16), v_bf[:, sl]) + _dot_nt(q_state[:, sl], st.astype(BF16))
        st_ref[h] = st * decay[:, sl] + _dot_tn(v_bf[:, sl], k_state[:, sl])
        outs.append(o)
    return jnp.concatenate(outs, axis=-1)


def _prompt_mix_kernel(x_ref, g_ref, w_ref, lb_ref, cos_ref, sin_ref, onorm_ref,
                       qd_ref, k_ref, v_ref, kb_ref, va_ref, mix_ref, state_ref,
                       o_s, st_s):
    s = pl.program_id(1)
    last = pl.num_programs(1) - 1

    @pl.when(s == 0)
    def _():
        st_s[...] = jnp.zeros_like(st_s)

    h = _rms(x_ref[0], g_ref[...]).astype(BF16)

    ts = x_ref.shape[1]
    sec_w = w_ref.shape[1] // 7
    n_chunk = ts // HG_CHUNK
    done = [0]

    def project(sec):
        return _dot(h, w_ref[:, sec * sec_w:(sec + 1) * sec_w])

    def recur(upto):
        for ci in range(done[0], min(upto, n_chunk)):
            rows = slice(ci * HG_CHUNK, (ci + 1) * HG_CHUNK)
            o_s[rows, :] = _hgrn_chunk(hq[rows, :], k_hg[rows, :], logf[rows, :], hi[rows, :], st_s)
        done[0] = max(done[0], min(upto, n_chunk))

    def head_rows_out(ref, val):
        for hh in range(DA_HEADS):
            ref[0, pl.ds(hh, ts, stride=DA_HEADS), :] = val[:, hh * LANES:(hh + 1) * LANES]

    lb = _lower_bound(lb_ref)
    cos = cos_ref[...]
    sin = sin_ref[...]
    hq = project(0)
    sig = _sigmoid(project(1))
    logf = jnp.log(lb + (1.0 - lb) * sig)
    k_hg = (1.0 - lb) * (1.0 - sig)
    hi = project(2)
    hg = project(3)
    recur(n_chunk // 4)
    q_da = _rope(project(4), cos, sin)
    qd_ref[0] = (q_da * (math.log2(math.e) * float(LANES // 2) ** -0.5)).astype(BF16)
    recur(n_chunk // 2)
    k_da = _rope(project(5), cos, sin)
    head_rows_out(k_ref, k_da)
    kb_ref[0] = k_da.astype(BF16)
    recur(3 * n_chunk // 4)
    dv = project(6)
    head_rows_out(v_ref, dv)
    va_ref[0] = dv
    recur(n_chunk)
    gate = hg * _sigmoid(hg)
    mix_ref[0] = (_head_rms(o_s[...], onorm_ref[...]) * gate).astype(BF16)

    @pl.when(s == last)
    def _():
        for hh in range(HG_HEADS):
            state_ref[0, hh] = st_s[hh].T


def _prompt_mix(x, g, w_in, hg_lb, cos, sin, onorm, ts):
    B, S, D = x.shape
    W = w_in.shape[1] // 7
    tok = lambda d: pl.BlockSpec((1, ts, d), lambda b, s: (b, s, 0))
    tab = pl.BlockSpec((ts, W), lambda b, s: (s, 0))
    rows = pl.BlockSpec((1, ts * DA_HEADS, LANES), lambda b, s: (b, s, 0))
    state = pl.BlockSpec((1, HG_HEADS, LANES, LANES), lambda b, s: (b, 0, 0, 0))
    sds = jax.ShapeDtypeStruct
    return pl.pallas_call(
        _prompt_mix_kernel,
        grid=(B, S // ts),
        in_specs=[tok(D), _const_spec((1, D)), _const_spec(w_in.shape),
                  _const_spec(hg_lb.shape), tab, tab, _const_spec((1, LANES))],
        out_specs=[tok(W), rows, rows, tok(W), tok(W), tok(W), state],
        out_shape=[sds((B, S, W), BF16), sds((B, S * DA_HEADS, LANES), F32),
                   sds((B, S * DA_HEADS, LANES), F32), sds((B, S, W), BF16),
                   sds((B, S, W), F32), sds((B, S, W), BF16),
                   sds((B, HG_HEADS, LANES, LANES), F32)],
        scratch_shapes=[pltpu.VMEM((ts, W), F32), pltpu.VMEM((HG_HEADS, LANES, LANES), F32)],
        compiler_params=_cparams(("arbitrary", "arbitrary"), 48 << 20),
        name="prompt_mix",
    )(x, g, w_in, hg_lb, cos, sin, onorm)


def _diff_attn_kernel(q_ref, k_ref, v_ref, lam_ref, onorm_ref, o_ref,
                      qq_s, vt_s, m_s, acc_s, *, lam_init, tk):
    i = pl.program_id(2)
    tq = q_ref.shape[1]
    n_ck = tq // tk
    n_kv = v_ref.shape[1] // tk

    @pl.when(i == 0)
    def _():
        ones = jnp.ones((vt_s.shape[1] - LANES, tk), BF16)
        for jj in range(n_kv):
            vt_s[jj, 0:LANES, :] = v_ref[0, jj * tk:(jj + 1) * tk, :].T.astype(BF16)
            vt_s[jj, LANES:, :] = ones

    q = q_ref[0]
    lane = lax.broadcasted_iota(jnp.int32, q.shape, 1)
    zero = jnp.zeros_like(q)
    qq_s[0:tq, :] = jnp.where(lane < LANES // 2, q, zero)
    qq_s[tq:2 * tq, :] = jnp.where(lane >= LANES // 2, q, zero)
    m_s[...] = jnp.full(m_s.shape, -jnp.inf, F32)
    acc_s[...] = jnp.zeros(acc_s.shape, F32)

    def scores(j, c):
        k = k_ref[0, pl.ds(pl.multiple_of(j * tk, tk), tk), :]
        return _dot_nt(k, qq_s[c * tk:(c + 1) * tk, :])

    def softmax(j, c, tri, s):
        cols = slice(c * tk, (c + 1) * tk)
        if tri:
            key = lax.broadcasted_iota(jnp.int32, s.shape, 0)
            qry = lax.broadcasted_iota(jnp.int32, s.shape, 1)
            s = jnp.where(key <= qry, s, -jnp.inf)
        m_old = m_s[:, cols]
        m_new = jnp.maximum(m_old, jnp.max(s, axis=0, keepdims=True))
        alpha = jnp.exp2(m_old - m_new)
        p = jnp.exp2(s - m_new)
        m_s[:, cols] = m_new
        return cols, alpha, _dot(vt_s[j], p.astype(BF16))

    def accumulate(cols, alpha, pv):
        acc_s[:, cols] = alpha * acc_s[:, cols] + pv

    def run(work):
        s = scores(*work[0][:2])
        pending = None
        for n, (j, c, tri) in enumerate(work):
            s_next = scores(*work[n + 1][:2]) if n + 1 < len(work) else None
            done = softmax(j, c, tri, s)
            if pending is not None:
                accumulate(*pending)
            pending = done
            s = s_next
        accumulate(*pending)

    def full_block(j, carry):
        run([(j, c, False) for c in range(2 * n_ck)])
        return carry

    lax.fori_loop(0, i * n_ck, full_block, 0)
    run([(i * n_ck + d, mp * n_ck + cq, cq == d)
         for d in range(n_ck) for mp in range(2) for cq in range(d, n_ck)])

    lam = _lambda(lam_ref, lam_init)
    a = acc_s[0:LANES, :] * (1.0 / acc_s[LANES:LANES + 1, :])
    o_t = a[:, 0:tq] - lam * a[:, tq:2 * tq]
    ms = jnp.mean(o_t * o_t, axis=0, keepdims=True)
    y = (o_t * lax.rsqrt(ms + EPS)).T
    o_ref[0] = (y * onorm_ref[...] * (1.0 - lam_init)).astype(BF16)


def _diff_attn_prompt(qd, kb, v, da_lambda, onorm, lam_init, tq, tk):
    B, S, W = qd.shape
    H = W // LANES
    qspec = pl.BlockSpec((1, tq, LANES), lambda b, h, i: (b, i, h))
    kvspec = pl.BlockSpec((1, S, LANES), lambda b, h, i: (b, 0, h))
    return pl.pallas_call(
        functools.partial(_diff_attn_kernel, lam_init=lam_init, tk=tk),
        grid=(B, H, S // tq),
        in_specs=[qspec, kvspec, kvspec, _const_spec(da_lambda.shape), _const_spec((1, LANES))],
        out_specs=qspec,
        out_shape=jax.ShapeDtypeStruct((B, S, W), BF16),
        scratch_shapes=[pltpu.VMEM((2 * tq, LANES), BF16),
                        pltpu.VMEM((S // tk, LANES + BF16_ROWS, tk), BF16),
                        pltpu.VMEM((1, 2 * tq), F32),
                        pltpu.VMEM((LANES + BF16_ROWS, 2 * tq), F32)],
        compiler_params=_cparams(("arbitrary",) * 3, 32 << 20),
        name="diff_attn_prompt",
    )(qd, kb, v, da_lambda, onorm)


def _cross_attn_heads(q, mk_head, mv_head):
    dh = q.shape[-1] // MEM_HEADS
    outs = []
    for h in range(MEM_HEADS):
        s = _dot_nt(q[:, h * dh:(h + 1) * dh], mk_head(h))
        m = jnp.max(s, axis=-1, keepdims=True)
        p = jnp.exp(s - m)
        l = jnp.sum(p, axis=-1, keepdims=True)
        outs.append(_dot((p / l).astype(BF16), mv_head(h)))
    return jnp.concatenate(outs, axis=-1)


def _out_cross_kernel(x_ref, mhg_ref, mda_ref, wo_ref, gq_ref, wq_ref, mk_ref, mv_ref,
                      wmo_ref, x2_ref):
    w = mhg_ref.shape[-1]
    x1 = x_ref[0] + _dot(mhg_ref[0], wo_ref[0:w, :]) + _dot(mda_ref[0], wo_ref[w:2 * w, :])
    hq = _rms(x1, gq_ref[...]).astype(BF16)
    dh = wq_ref.shape[1] // MEM_HEADS
    q = (_dot(hq, wq_ref[...]) * (float(dh) ** -0.5)).astype(BF16)
    o = _cross_attn_heads(q, lambda h: mk_ref[0, :, h * dh:(h + 1) * dh],
                          lambda h: mv_ref[0, :, h * dh:(h + 1) * dh])
    x2_ref[0] = x1 + _dot(o.astype(BF16), wmo_ref[...])


def _out_cross_prompt(x, mhg, mda, w_out, gq, w_mq, mkb, mvb, w_mo, ts):
    B, S, D = x.shape
    W = mhg.shape[-1]
    N, MW = mkb.shape[1], mkb.shape[2]
    tok = lambda d: pl.BlockSpec((1, ts, d), lambda b, s: (b, s, 0))
    mem = pl.BlockSpec((1, N, MW), lambda b, s: (b, 0, 0))
    return pl.pallas_call(
        _out_cross_kernel,
        grid=(B, S // ts),
        in_specs=[tok(D), tok(W), tok(W), _const_spec(w_out.shape), _const_spec((1, D)),
                  _const_spec(w_mq.shape), mem, mem, _const_spec(w_mo.shape)],
        out_specs=tok(D),
        out_shape=jax.ShapeDtypeStruct((B, S, D), F32),
        compiler_params=_cparams(("arbitrary", "arbitrary"), 48 << 20),
        name="out_cross_prompt",
    )(x, mhg, mda, w_out, gq, w_mq, mkb, mvb, w_mo)


def _out_cross_sample_kernel(x_ref, mhg_ref, mda_ref, wo_ref, gq_ref, wq_ref, mk_ref, mv_ref,
                             wmo_ref, x2_ref, x1_s, q_s, o_s):
    b = pl.program_id(0)
    w = mhg_ref.shape[-1]

    @pl.when(b == 0)
    def _():
        x1 = x_ref[...] + _dot(mhg_ref[...], wo_ref[0:w, :]) + _dot(mda_ref[...], wo_ref[w:2 * w, :])
        x1_s[...] = x1
        hq = _rms(x1, gq_ref[...]).astype(BF16)
        dh = wq_ref.shape[1] // MEM_HEADS
        q_s[...] = _dot(hq, wq_ref[...]) * (float(dh) ** -0.5)

    q = q_s[pl.ds(b, 1), :].astype(BF16)
    dh = q.shape[-1] // MEM_HEADS
    o_s[pl.ds(b, 1), :] = _cross_attn_heads(
        q, lambda h: mk_ref[0, :, h * dh:(h + 1) * dh].astype(BF16),
        lambda h: mv_ref[0, :, h * dh:(h + 1) * dh].astype(BF16))

    @pl.when(b == pl.num_programs(0) - 1)
    def _():
        x2_ref[...] = x1_s[...] + _dot(o_s[...].astype(BF16), wmo_ref[...])


def _out_cross_sample(x, mhg, mda, w_out, gq, w_mq, mem_k, mem_v, w_mo):
    T, D = x.shape
    N, MW = mem_k.shape[1], mem_k.shape[2]
    mem = pl.BlockSpec((1, N, MW), lambda b: (b, 0, 0))
    return pl.pallas_call(
        _out_cross_sample_kernel,
        grid=(T,),
        in_specs=[_const_spec(x.shape), _const_spec(mhg.shape), _const_spec(mda.shape),
                  _const_spec(w_out.shape), _const_spec((1, D)), _const_spec(w_mq.shape),
                  mem, mem, _const_spec(w_mo.shape)],
        out_specs=pl.BlockSpec((T, D), lambda b: (0, 0)),
        out_shape=jax.ShapeDtypeStruct((T, D), F32),
        scratch_shapes=[pltpu.VMEM((T, D), F32), pltpu.VMEM((T, MW), F32), pltpu.VMEM((T, MW), F32)],
        compiler_params=_cparams(("arbitrary",), 32 << 20),
        name="out_cross_sample",
    )(x, mhg, mda, w_out, gq, w_mq, mem_k, mem_v, w_mo)


def _ffn_kernel(x_ref, gf_ref, wg_ref, wu_ref, wd_ref, gfin_ref, y_ref):
    x = x_ref[...]
    h = _rms(x, gf_ref[...]).astype(BF16)
    g = _dot(h, wg_ref[...])
    u = _dot(h, wu_ref[...])
    a = (g * _sigmoid(g) * u).astype(BF16)
    x3 = x + _dot(a, wd_ref[...])
    y_ref[...] = _rms(x3, gfin_ref[...])


def _ffn(x, gf, wg, wu, wd, gfin, ts):
    T, D = x.shape
    tok = pl.BlockSpec((ts, D), lambda i: (i, 0))
    return pl.pallas_call(
        _ffn_kernel,
        grid=(T // ts,),
        in_specs=[tok, _const_spec((1, D)), _const_spec(wg.shape), _const_spec(wu.shape),
                  _const_spec(wd.shape), _const_spec((1, D))],
        out_specs=tok,
        out_shape=jax.ShapeDtypeStruct((T, D), F32),
        compiler_params=_cparams(("arbitrary",), VMEM_LIMIT_CAP),
        name="ffn",
    )(x, gf, wg, wu, wd, gfin)


def _sample_mix_kernel(x_ref, g_ref, w_ref, lb_ref, cos_ref, sin_ref,
                       qd_ref, k_ref, v_ref, hq_ref, kk_ref, f_ref, vh_ref, gate_ref):
    h = _rms(x_ref[...], g_ref[...]).astype(BF16)
    z = _dot(h, w_ref[...])
    lb = _lower_bound(lb_ref)
    hq, k_hg, logf, hi, gate, dq, dk, dv = _mixer_sections(z, lb)
    cos = cos_ref[...]
    sin = sin_ref[...]
    qd_ref[...] = _rope(dq, cos, sin) * (float(LANES // 2) ** -0.5)
    k_ref[...] = _rope(dk, cos, sin)
    v_ref[...] = dv
    hq_ref[...] = hq
    kk_ref[...] = k_hg
    f_ref[...] = jnp.exp(logf)
    vh_ref[...] = hi
    gate_ref[...] = gate


def _sample_mix(x, g, w_in, hg_lb, cos, sin):
    T, D = x.shape
    W = w_in.shape[1] // 7
    sds = jax.ShapeDtypeStruct
    return pl.pallas_call(
        _sample_mix_kernel,
        out_shape=[sds((T, W), F32)] * 8,
        compiler_params=_cparams(None, 32 << 20),
        name="sample_mix",
    )(x, g, w_in, hg_lb, cos, sin)


def _sample_hgrn_kernel(hq_ref, kk_ref, f_ref, vh_ref, gate_ref, onorm_ref, s0_ref,
                        mix_ref, s1_ref):
    nb = s0_ref.shape[0]
    g0 = pl.program_id(0) * nb
    W = hq_ref.shape[-1]

    def columns(ref, h):
        rows = ref[pl.ds(pl.multiple_of(g0, nb), nb), h * LANES:(h + 1) * LANES]
        pad = jnp.zeros((LANES - nb, LANES), F32)
        return jnp.concatenate([rows, pad], axis=0).T

    o_rows = []
    for h in range(HG_HEADS):
        q_t, k_t, f_t = columns(hq_ref, h), columns(kk_ref, h), columns(f_ref, h)
        v_rows = vh_ref[pl.ds(pl.multiple_of(g0, nb), nb), h * LANES:(h + 1) * LANES]
        o_h = []
        for j in range(nb):
            bcast = lambda tile: jnp.broadcast_to(tile[:, j:j + 1], (LANES, LANES))
            s_new = bcast(f_t) * s0_ref[j, h] + bcast(k_t) * v_rows[j:j + 1, :]
            s1_ref[j, h] = s_new
            o_h.append(jnp.sum(bcast(q_t) * s_new, axis=0, keepdims=True))
        o_rows.append(jnp.concatenate(o_h, axis=0))
    o = jnp.concatenate(o_rows, axis=-1)
    gate = gate_ref[pl.ds(pl.multiple_of(g0, nb), nb), :]
    mix_ref[...] = (_head_rms(o, onorm_ref[...]) * gate).astype(BF16)


def _sample_hgrn(hq, kk, f, vh, gate, onorm, s0, nb=8):
    T, W = hq.shape
    st = pl.BlockSpec((nb, HG_HEADS, LANES, LANES), lambda i: (i, 0, 0, 0))
    full = _const_spec((T, W))
    return pl.pallas_call(
        _sample_hgrn_kernel,
        grid=(T // nb,),
        in_specs=[full] * 5 + [_const_spec((1, LANES)), st],
        out_specs=[pl.BlockSpec((nb, W), lambda i: (i, 0)), st],
        out_shape=[jax.ShapeDtypeStruct((T, W), BF16), jax.ShapeDtypeStruct(s0.shape, F32)],
        compiler_params=_cparams(("arbitrary",), 32 << 20),
        name="sample_hgrn",
    )(hq, kk, f, vh, gate, onorm, s0)


def _decode_seed(q, k_new, v_new, state):
    q8_s, m_s, l_s, acc_s = state
    H = DA_HEADS

    def head_rows(x):
        return jnp.concatenate([x[:, h * LANES:(h + 1) * LANES] for h in range(H)] * 2, axis=0)

    row = lax.broadcasted_iota(jnp.int32, (2 * H, LANES), 0)
    lane = lax.broadcasted_iota(jnp.int32, (2 * H, LANES), 1)
    q8 = jnp.where((row // H) == (lane // (LANES // 2)), head_rows(q), 0.0)
    q8_s[...] = q8.astype(BF16)
    m_s[...] = jnp.sum(q8 * head_rows(k_new), axis=-1, keepdims=True)
    l_s[...] = jnp.ones(l_s.shape, F32)
    acc_s[...] = head_rows(v_new)


def _decode_pages(k_pages, v_pages, state):
    q8_s, m_s, l_s, acc_s = state
    H = DA_HEADS
    q8 = q8_s[...]
    s = jnp.concatenate([_dot_nt(q8, k.astype(BF16)) for k in k_pages], axis=-1)
    row = lax.broadcasted_iota(jnp.int32, s.shape, 0)
    col = lax.broadcasted_iota(jnp.int32, s.shape, 1)
    s = jnp.where((col % H) == (row % H), s, -jnp.inf)
    m_old = m_s[...]
    m_new = jnp.maximum(m_old, jnp.max(s, axis=-1, keepdims=True))
    alpha = jnp.exp(m_old - m_new)
    pr = jnp.exp(s - m_new)
    l_s[...] = alpha * l_s[...] + jnp.sum(pr, axis=-1, keepdims=True)
    pr = pr.astype(BF16)
    rows = k_pages[0].shape[0]
    pv = _dot(pr[:, 0:rows], v_pages[0].astype(BF16))
    for p in range(1, len(v_pages)):
        pv = pv + _dot(pr[:, p * rows:(p + 1) * rows], v_pages[p].astype(BF16))
    acc_s[...] = alpha * acc_s[...] + pv
    m_s[...] = m_new


def _decode_finish(lam_ref, onorm_ref, lam_init, state):
    _, _, l_s, acc_s = state
    H = DA_HEADS
    lam = _lambda(lam_ref, lam_init)
    o = acc_s[...] / l_s[...]
    y = _rms(o[0:H] - lam * o[H:2 * H], onorm_ref[...]) * (1.0 - lam_init)
    return jnp.concatenate([y[h:h + 1, :] for h in range(H)], axis=-1).astype(BF16)


def _paged_attn_kernel(pt_ref, q_ref, kn_ref, vn_ref, lam_ref, onorm_ref, *refs,
                       lam_init, n_pg):
    k_refs = refs[:n_pg]
    v_refs = refs[n_pg:2 * n_pg]
    o_ref = refs[2 * n_pg]
    state = refs[2 * n_pg + 1:]
    j = pl.program_id(1)

    @pl.when(j == 0)
    def _():
        _decode_seed(q_ref[0], kn_ref[0], vn_ref[0], state)

    _decode_pages([r[0] for r in k_refs], [r[0] for r in v_refs], state)

    @pl.when(j == pl.num_programs(1) - 1)
    def _():
        o_ref[0] = _decode_finish(lam_ref, onorm_ref, lam_init, state)


def _paged_attn(page_table, qd, k_new, v_new, da_lambda, onorm, cache_k, cache_v, lam_init, n_pg):
    T, n_pages = page_table.shape
    n_phys, rows, dk = cache_k.shape
    W = qd.shape[-1]
    row = pl.BlockSpec((1, 1, W), lambda b, j, pt: (b, 0, 0))

    def page_spec(p):
        return pl.BlockSpec((1, rows, dk), lambda b, j, pt: (pt[b, j * n_pg + p], 0, 0))

    const = lambda shape: pl.BlockSpec(shape, lambda b, j, pt: (0,) * len(shape))
    grid_spec = pltpu.PrefetchScalarGridSpec(
        num_scalar_prefetch=1,
        grid=(T, n_pages // n_pg),
        in_specs=[row, row, row, const(da_lambda.shape), const((1, LANES))]
        + [page_spec(p) for p in range(n_pg)] * 2,
        out_specs=row,
        scratch_shapes=[pltpu.VMEM((2 * DA_HEADS, LANES), BF16),
                        pltpu.VMEM((2 * DA_HEADS, 1), F32),
                        pltpu.VMEM((2 * DA_HEADS, 1), F32),
                        pltpu.VMEM((2 * DA_HEADS, LANES), F32)],
    )
    out = pl.pallas_call(
        functools.partial(_paged_attn_kernel, lam_init=lam_init, n_pg=n_pg),
        grid_spec=grid_spec,
        out_shape=jax.ShapeDtypeStruct((T, 1, W), BF16),
        compiler_params=_cparams(("arbitrary", "arbitrary"), 48 << 20),
        name="paged_diff_attn",
    )(page_table, qd.reshape(T, 1, W), k_new.reshape(T, 1, W), v_new.reshape(T, 1, W),
      da_lambda, onorm, *([cache_k] * n_pg), *([cache_v] * n_pg))
    return out.reshape(T, W)


def _ffn_decode_kernel(pt_ref, x_ref, gf_ref, wg_ref, wu_ref, wd_ref, gfin_ref,
                       q_ref, kn_ref, vn_ref, lam_ref, onorm_ref, ck_hbm, cv_hbm,
                       y_ref, o_ref, kbuf, vbuf, sem, q8_s, m_s, l_s, acc_s,
                       *, lam_init, n_pg, n_slot, sps, col_bounds):
    i = pl.program_id(0)
    n_steps = pl.num_programs(0)
    state = (q8_s, m_s, l_s, acc_s)
    rows = kbuf.shape[1] // n_pg
    gps = pt_ref.shape[1] // n_pg
    groups = [(ls, g) for ls in range(sps) for g in range(gps)]
    n_grp = len(groups)

    def copies(b, g, slot):
        out = []
        for p in range(n_pg):
            page = pt_ref[b, g * n_pg + p]
            dst = pl.ds(p * rows, rows)
            out.append(pltpu.make_async_copy(ck_hbm.at[page], kbuf.at[slot, dst], sem.at[slot, 0]))
            out.append(pltpu.make_async_copy(cv_hbm.at[page], vbuf.at[slot, dst], sem.at[slot, 1]))
        return out

    def start(b, g, slot):
        for cp in copies(b, g, slot):
            cp.start()

    ahead = n_slot - 1

    @pl.when(i == 0)
    def _():
        for n in range(ahead):
            start(groups[n][0], groups[n][1], n % n_slot)

    def fold_group(n):
        ls, g = groups[n]
        b = i * sps + ls
        slot = n % n_slot
        nxt = n + ahead
        if nxt < n_grp:
            start(i * sps + groups[nxt][0], groups[nxt][1], nxt % n_slot)
        else:
            @pl.when(i + 1 < n_steps)
            def _():
                ls2, g2 = groups[nxt - n_grp]
                start((i + 1) * sps + ls2, g2, nxt % n_slot)
        for cp in copies(b, g, slot):
            cp.wait()
        if g == 0:
            _decode_seed(q_ref[b], kn_ref[b], vn_ref[b], state)
        _decode_pages([kbuf[slot, p * rows:(p + 1) * rows, :] for p in range(n_pg)],
                      [vbuf[slot, p * rows:(p + 1) * rows, :] for p in range(n_pg)], state)
        if g == gps - 1:
            o_ref[b] = _decode_finish(lam_ref, onorm_ref, lam_init, state)

    x = x_ref[...]
    h = _rms(x, gf_ref[...]).astype(BF16)
    n_piece = len(col_bounds) - 1
    acc = None
    for k in range(n_piece):
        lo, hi = col_bounds[k], col_bounds[k + 1]
        g = _dot(h, wg_ref[:, lo:hi])
        u = _dot(h, wu_ref[:, lo:hi])
        part = _dot((g * _sigmoid(g) * u).astype(BF16), wd_ref[lo:hi, :])
        acc = part if acc is None else acc + part
        for n in range(k * n_grp // n_piece, (k + 1) * n_grp // n_piece):
            fold_group(n)
    y_ref[...] = _rms(x + acc, gfin_ref[...])


def _ffn_decode(x, gf, wg, wu, wd, gfin, ts, page_table, qd, k_new, v_new, da_lambda, onorm,
                cache_k, cache_v, lam_init, n_pg, n_slot):
    T, D = x.shape
    F = wg.shape[1]
    Ts, n_pages = page_table.shape
    n_phys, rows, dk = cache_k.shape
    W = qd.shape[-1]
    n_tiles = T // ts
    sps = Ts // n_tiles
    mxu_n = 2 * LANES
    piece = max(mxu_n, (F // 4) // mxu_n * mxu_n)
    col_bounds = tuple(range(0, F, piece)) + (F,)
    tok = pl.BlockSpec((ts, D), lambda i, pt: (i, 0))
    const = lambda shape: pl.BlockSpec(shape, lambda i, pt: (0,) * len(shape),
                                       pipeline_mode=pl.Buffered(1))
    sample = const((Ts, 1, W))
    hbm = pl.BlockSpec(memory_space=pl.ANY)
    grid_spec = pltpu.PrefetchScalarGridSpec(
        num_scalar_prefetch=1,
        grid=(n_tiles,),
        in_specs=[tok, const((1, D)), const(wg.shape), const(wu.shape), const(wd.shape),
                  const((1, D)), sample, sample, sample, const(da_lambda.shape),
                  const((1, LANES)), hbm, hbm],
        out_specs=[tok, pl.BlockSpec((Ts, 1, W), lambda i, pt: (0, 0, 0))],
        scratch_shapes=[pltpu.VMEM((n_slot, n_pg * rows, dk), F32),
                        pltpu.VMEM((n_slot, n_pg * rows, dk), F32),
                        pltpu.SemaphoreType.DMA((n_slot, 2)),
                        pltpu.VMEM((2 * DA_HEADS, LANES), BF16),
                        pltpu.VMEM((2 * DA_HEADS, 1), F32),
                        pltpu.VMEM((2 * DA_HEADS, 1), F32),
                        pltpu.VMEM((2 * DA_HEADS, LANES), F32)],
    )
    y, o = pl.pallas_call(
        functools.partial(_ffn_decode_kernel, lam_init=lam_init, n_pg=n_pg, n_slot=n_slot, sps=sps,
                          col_bounds=col_bounds),
        grid_spec=grid_spec,
        out_shape=[jax.ShapeDtypeStruct((T, D), F32), jax.ShapeDtypeStruct((Ts, 1, W), BF16)],
        compiler_params=_cparams(("arbitrary",), VMEM_LIMIT_CAP),
        name="ffn_decode",
    )(page_table, x, gf, wg, wu, wd, gfin, qd.reshape(Ts, 1, W), k_new.reshape(Ts, 1, W),
      v_new.reshape(Ts, 1, W), da_lambda, onorm, cache_k, cache_v)
    return y, o.reshape(Ts, W)


def _rope_tables(pos, dh, n_maps):
    inv = ROPE_THETA ** (-jnp.arange(0, dh, 2, dtype=F32) / dh)
    ang = pos.astype(F32)[:, None] * inv[None, :]
    cos = jnp.cos(ang)
    sin = jnp.sin(ang)
    cos = jnp.tile(jnp.concatenate([cos, cos], axis=-1), (1, n_maps))
    sin = jnp.tile(jnp.concatenate([-sin, sin], axis=-1), (1, n_maps))
    return cos, sin


def _pick_tile(n, pref):
    t = min(n, pref)
    while n % t:
        t //= 2
    return t


def kernel(x_prompt, x_sample, mem_prompt, cache_k, cache_v, cache_mem_k, cache_mem_v, state_hgrn, page_table, norm_mix, w_in, hg_lb, hg_onorm, da_lambda, da_onorm, w_out, norm_mem_q, norm_mem_kv, w_mq, w_mk, w_mv, w_mo, norm_ffn, w_gate, w_up, w_down, norm_final):
    B, S, D = x_prompt.shape
    T = x_sample.shape[0]
    depth = w_in.shape[0]
    assert depth == 1 and x_sample.shape[1] == 1
    l = 0
    lam_init = 0.8 - 0.6 * math.exp(-0.3 * l)
    n_phys, page = cache_k.shape[1], cache_k.shape[2]
    past_len = page_table.shape[1] * page
    W = w_in.shape[2] // 7

    bf = lambda w: w.astype(BF16)
    w_in_b, w_out_b = bf(w_in[l]), bf(w_out[l])
    w_mq_b, w_mk_b, w_mv_b, w_mo_b = bf(w_mq[l]), bf(w_mk[l]), bf(w_mv[l]), bf(w_mo[l])
    w_gate_b, w_up_b, w_down_b = bf(w_gate[l]), bf(w_up[l]), bf(w_down[l])
    row = lambda g: g.reshape(1, -1)
    lb2 = hg_lb[l:l + 2]
    lam_p = da_lambda[l]

    cos_p, sin_p = _rope_tables(jnp.arange(S), LANES // 2, 2 * DA_HEADS)
    cos_s, sin_s = _rope_tables(past_len + jnp.arange(1), LANES // 2, 2 * DA_HEADS)
    cos_s = jnp.broadcast_to(cos_s, (T, W))
    sin_s = jnp.broadcast_to(sin_s, (T, W))

    mk, mv, mkb, mvb = _mem_kv(mem_prompt, row(norm_mem_kv[l]), w_mk_b, w_mv_b)
    qd, k_p, v_p, kb, v_att, mix_hg, hs_p = _prompt_mix(
        x_prompt, row(norm_mix[l]), w_in_b, lb2, cos_p, sin_p, row(hg_onorm[l]),
        ts=_pick_tile(S, 256))
    mix_da = _diff_attn_prompt(qd, kb, v_att, lam_p, row(da_onorm[l]), lam_init,
                               tq=_pick_tile(S, 1024), tk=_pick_tile(S, 256))
    x2 = _out_cross_prompt(x_prompt, mix_hg, mix_da, w_out_b, row(norm_mem_q[l]), w_mq_b,
                           mkb, mvb, w_mo_b, ts=_pick_tile(S, 512))

    xs = x_sample.reshape(T, D)
    qd_s, k_s, v_s, hq_s, kk_s, f_s, vh_s, gate_s = _sample_mix(
        xs, row(norm_mix[l]), w_in_b, lb2, cos_s, sin_s)
    mix_hg_s, hs_s = _sample_hgrn(hq_s, kk_s, f_s, vh_s, gate_s, row(hg_onorm[l]), state_hgrn[l])
    ck = cache_k.reshape(depth * n_phys, page * DA_HEADS, LANES)
    cv = cache_v.reshape(depth * n_phys, page * DA_HEADS, LANES)
    n_pages = page_table.shape[1]
    ffn_ts = _pick_tile(B * S, 512)
    n_tiles = (B * S) // ffn_ts
    n_pg = _pick_tile(n_pages, 8)
    n_slot = 4
    ffn_args = (row(norm_ffn[l]), w_gate_b, w_up_b, w_down_b, row(norm_final))
    if T % n_tiles == 0 and ((T // n_tiles) * (n_pages // n_pg)) % n_slot == 0:
        y_p, mix_da_s = _ffn_decode(x2.reshape(B * S, D), *ffn_args, ffn_ts, page_table, qd_s, k_s,
                                    v_s, lam_p, row(da_onorm[l]), ck, cv, lam_init, n_pg, n_slot)
    else:
        y_p = _ffn(x2.reshape(B * S, D), *ffn_args, ts=ffn_ts)
        mix_da_s = _paged_attn(page_table, qd_s, k_s, v_s, lam_p, row(da_onorm[l]), ck, cv,
                               lam_init, n_pg=_pick_tile(n_pages, 16))
    y_p = y_p.reshape(B, S, D)
    MW = w_mk.shape[2]
    x2_s = _out_cross_sample(xs, mix_hg_s, mix_da_s, w_out_b, row(norm_mem_q[l]), w_mq_b,
                             cache_mem_k[l].reshape(T, -1, MW), cache_mem_v[l].reshape(T, -1, MW),
                             w_mo_b)
    y_s = _ffn(x2_s, row(norm_ffn[l]), w_gate_b, w_up_b, w_down_b, row(norm_final), ts=T)

    dk = LANES
    return (y_p, y_s.reshape(T, 1, D),
            hs_p[None],
            k_p.reshape(1, B, S, DA_HEADS, dk), v_p.reshape(1, B, S, DA_HEADS, dk),
            mk[None], mv[None],
            hs_s[None],
            k_s.reshape(1, T, 1, DA_HEADS, dk), v_s.reshape(1, T, 1, DA_HEADS, dk))
```

```python
import functools
import math

import jax
import jax.numpy as jnp
from jax import lax
from jax.experimental import pallas as pl
from jax.experimental.pallas import tpu as pltpu

F32 = jnp.float32
BF16 = jnp.bfloat16
EPS = 1e-6
ROPE_THETA = 10000.0

HG_HEADS = 4
DA_HEADS = 4
MEM_HEADS = 4
HG_CHUNK = 64
LANES = 128
BF16_ROWS = 16
VMEM_LIMIT_CAP = 56 << 20

_NT = (((1,), (1,)), ((), ()))
_TN = (((0,), (0,)), ((), ()))


def _dot(a, b):
    return jnp.dot(a, b, preferred_element_type=F32)


def _dot_nt(a, b):
    return lax.dot_general(a, b, _NT, preferred_element_type=F32)


def _dot_tn(a, b):
    return lax.dot_general(a, b, _TN, preferred_element_type=F32)


def _rms(x, g):
    ms = jnp.mean(x * x, axis=-1, keepdims=True)
    return x * lax.rsqrt(ms + EPS) * g


def _sigmoid(x):
    return 1.0 / (1.0 + jnp.exp(-x))


def _cparams(semantics, vmem_bytes):
    return pltpu.CompilerParams(
        dimension_semantics=semantics,
        vmem_limit_bytes=int(min(max(vmem_bytes, 16 << 20), VMEM_LIMIT_CAP)))


def _const_spec(shape):
    nd = len(shape)
    return pl.BlockSpec(shape, lambda *_: (0,) * nd, pipeline_mode=pl.Buffered(1))


def _lower_bound(lb_ref):
    a0 = lb_ref[0:1, :]
    a1 = lb_ref[1:2, :]
    m = jnp.maximum(a0, a1)
    e0 = jnp.exp(a0 - m)
    e1 = jnp.exp(a1 - m)
    return e0 / (e0 + e1)


def _rope(x, cos, sin_signed):
    n = x.shape[-1]
    lane = lax.broadcasted_iota(jnp.int32, x.shape, x.ndim - 1)
    swapped = jnp.where((lane & 63) < 32,
                        pltpu.roll(x, n - 32, x.ndim - 1),
                        pltpu.roll(x, 32, x.ndim - 1))
    return x * cos + swapped * sin_signed


def _lambda(lam_ref, lam_init):
    lp = lam_ref[...]
    s01 = jnp.sum(lp[0:1, :] * lp[1:2, :], axis=-1, keepdims=True)
    s23 = jnp.sum(lp[2:3, :] * lp[3:4, :], axis=-1, keepdims=True)
    return jnp.exp(s01) - jnp.exp(s23) + lam_init


def _mixer_sections(z, lb):
    w = z.shape[-1] // 7
    hq, zf, hi, hg, dq, dk, dv = (z[:, i * w:(i + 1) * w] for i in range(7))
    sig = _sigmoid(zf)
    logf = jnp.log(lb + (1.0 - lb) * sig)
    k_hg = (1.0 - lb) * (1.0 - sig)
    gate = hg * _sigmoid(hg)
    return hq, k_hg, logf, hi, gate, dq, dk, dv


def _head_rms(o, g):
    outs = []
    for h in range(o.shape[-1] // LANES):
        oh = o[:, h * LANES:(h + 1) * LANES]
        outs.append(_rms(oh, g))
    return jnp.concatenate(outs, axis=-1)


def _memkv_kernel(mem_ref, g_ref, wk_ref, wv_ref, k_ref, v_ref, kb_ref, vb_ref):
    m = _rms(mem_ref[0], g_ref[...]).astype(BF16)
    k = _dot(m, wk_ref[...])
    v = _dot(m, wv_ref[...])
    dh = k_ref.shape[-1]
    for h in range(MEM_HEADS):
        k_ref[0, :, h, :] = k[:, h * dh:(h + 1) * dh]
        v_ref[0, :, h, :] = v[:, h * dh:(h + 1) * dh]
    kb_ref[0] = k.astype(BF16)
    vb_ref[0] = v.astype(BF16)


def _mem_kv(mem, g, wk, wv):
    B, N, D = mem.shape
    W = wk.shape[1]
    dh = W // MEM_HEADS
    blk = lambda d: pl.BlockSpec((1, N, d), lambda b: (b, 0, 0))
    blk4 = pl.BlockSpec((1, N, MEM_HEADS, dh), lambda b: (b, 0, 0, 0))
    return pl.pallas_call(
        _memkv_kernel,
        grid=(B,),
        in_specs=[blk(D), _const_spec((1, D)), _const_spec((D, W)), _const_spec((D, W))],
        out_specs=[blk4, blk4, blk(W), blk(W)],
        out_shape=[jax.ShapeDtypeStruct((B, N, MEM_HEADS, dh), F32)] * 2
        + [jax.ShapeDtypeStruct((B, N, W), BF16)] * 2,
        compiler_params=_cparams(("arbitrary",), 40 << 20),
        name="mem_kv",
    )(mem, g, wk, wv)


def _hgrn_chunk(q, kk, lf, v, st_ref):
    C, W = q.shape
    t = lax.broadcasted_iota(jnp.int32, (C, W), 0)
    ti = lax.broadcasted_iota(jnp.int32, (C, C), 0)
    si = lax.broadcasted_iota(jnp.int32, (C, C), 1)
    n_lvl = int(math.log2(C))

    c = lf
    lvls = []
    for l in range(n_lvl):
        m = 1 << l
        upper = (t & m) != 0
        if m < 8:
            y = c
            for i in range(l):
                y = jnp.where((t & (1 << i)) == 0, pltpu.roll(y, C - (1 << i), 0), y)
            bc = jnp.where(upper, pltpu.roll(y, m, 0), y)
        else:
            pieces = []
            for j in range(C // (2 * m)):
                r = 2 * m * j + m - 1
                pieces.append(jnp.broadcast_to(c[r:r + 1, :], (2 * m, W)))
            bc = pieces[0] if len(pieces) == 1 else jnp.concatenate(pieces, axis=0)
        e = jnp.exp(jnp.where(upper, c, bc - c))
        lvls.append((jnp.where(upper, q, kk) * e).astype(BF16))
        c = c + jnp.where(upper, bc, 0.0)
    b = c
    b_last = b[C - 1:C, :]
    q_bf = q.astype(BF16)
    k_bf = kk.astype(BF16)
    v_bf = v.astype(BF16)
    q_state = (q * jnp.exp(b)).astype(BF16)
    k_state = (kk * jnp.exp(b_last - b)).astype(BF16)
    decay = jnp.exp(b_last)

    diff_bits = jnp.where(ti > si, ti ^ si, 0)
    keeps = [(diff_bits >> l) == 1 for l in range(n_lvl)]

    outs = []
    for h in range(W // LANES):
        sl = slice(h * LANES, (h + 1) * LANES)
        a = jnp.where(ti == si, _dot_nt(q_bf[:, sl], k_bf[:, sl]), 0.0)
        for l in range(n_lvl):
            r = lvls[l][:, sl]
            a = a + jnp.where(keeps[l], _dot_nt(r, r), 0.0)
        st = st_ref[h]
        o = _dot(a.astype(BF16), v_bf[:, sl]) + _dot_nt(q_state[:, sl], st.astype(BF16))
        st_ref[h] = st * decay[:, sl] + _dot_tn(v_bf[:, sl], k_state[:, sl])
        outs.append(o)
    return jnp.concatenate(outs, axis=-1)


def _prompt_mix_kernel(x_ref, g_ref, w_ref, lb_ref, cos_ref, sin_ref, onorm_ref,
                       qd_ref, k_ref, v_ref, kb_ref, va_ref, mix_ref, state_ref,
                       o_s, st_s):
    s = pl.program_id(1)
    last = pl.num_programs(1) - 1

    @pl.when(s == 0)
    def _():
        st_s[...] = jnp.zeros_like(st_s)

    h = _rms(x_ref[0], g_ref[...]).astype(BF16)

    ts = x_ref.shape[1]
    sec_w = w_ref.shape[1] // 7
    n_chunk = ts // HG_CHUNK
    done = [0]

    def project(sec):
        return _dot(h, w_ref[:, sec * sec_w:(sec + 1) * sec_w])

    def recur(upto):
        for ci in range(done[0], min(upto, n_chunk)):
            rows = slice(ci * HG_CHUNK, (ci + 1) * HG_CHUNK)
            o_s[rows, :] = _hgrn_chunk(hq[rows, :], k_hg[rows, :], logf[rows, :], hi[rows, :], st_s)
        done[0] = max(done[0], min(upto, n_chunk))

    def head_rows_out(ref, val):
        for hh in range(DA_HEADS):
            ref[0, pl.ds(hh, ts, stride=DA_HEADS), :] = val[:, hh * LANES:(hh + 1) * LANES]

    lb = _lower_bound(lb_ref)
    cos = cos_ref[...]
    sin = sin_ref[...]
    hq = project(0)
    sig = _sigmoid(project(1))
    logf = jnp.log(lb + (1.0 - lb) * sig)
    k_hg = (1.0 - lb) * (1.0 - sig)
    hi = project(2)
    hg = project(3)
    recur(n_chunk // 4)
    q_da = _rope(project(4), cos, sin)
    qd_ref[0] = (q_da * (math.log2(math.e) * float(LANES // 2) ** -0.5)).astype(BF16)
    recur(n_chunk // 2)
    k_da = _rope(project(5), cos, sin)
    head_rows_out(k_ref, k_da)
    kb_ref[0] = k_da.astype(BF16)
    recur(3 * n_chunk // 4)
    dv = project(6)
    head_rows_out(v_ref, dv)
    va_ref[0] = dv
    recur(n_chunk)
    gate = hg * _sigmoid(hg)
    mix_ref[0] = (_head_rms(o_s[...], onorm_ref[...]) * gate).astype(BF16)

    @pl.when(s == last)
    def _():
        for hh in range(HG_HEADS):
            state_ref[0, hh] = st_s[hh].T


def _prompt_mix(x, g, w_in, hg_lb, cos, sin, onorm, ts):
    B, S, D = x.shape
    W = w_in.shape[1] // 7
    tok = lambda d: pl.BlockSpec((1, ts, d), lambda b, s: (b, s, 0))
    tab = pl.BlockSpec((ts, W), lambda b, s: (s, 0))
    rows = pl.BlockSpec((1, ts * DA_HEADS, LANES), lambda b, s: (b, s, 0))
    state = pl.BlockSpec((1, HG_HEADS, LANES, LANES), lambda b, s: (b, 0, 0, 0))
    sds = jax.ShapeDtypeStruct
    return pl.pallas_call(
        _prompt_mix_kernel,
        grid=(B, S // ts),
        in_specs=[tok(D), _const_spec((1, D)), _const_spec(w_in.shape),
                  _const_spec(hg_lb.shape), tab, tab, _const_spec((1, LANES))],
        out_specs=[tok(W), rows, rows, tok(W), tok(W), tok(W), state],
        out_shape=[sds((B, S, W), BF16), sds((B, S * DA_HEADS, LANES), F32),
                   sds((B, S * DA_HEADS, LANES), F32), sds((B, S, W), BF16),
                   sds((B, S, W), F32), sds((B, S, W), BF16),
                   sds((B, HG_HEADS, LANES, LANES), F32)],
        scratch_shapes=[pltpu.VMEM((ts, W), F32), pltpu.VMEM((HG_HEADS, LANES, LANES), F32)],
        compiler_params=_cparams(("arbitrary", "arbitrary"), 48 << 20),
        name="prompt_mix",
    )(x, g, w_in, hg_lb, cos, sin, onorm)


def _diff_attn_kernel(q_ref, k_ref, v_ref, lam_ref, onorm_ref, o_ref,
                      qq_s, vt_s, m_s, acc_s, *, lam_init, tk):
    i = pl.program_id(2)
    tq = q_ref.shape[1]
    n_ck = tq // tk
    n_kv = v_ref.shape[1] // tk

    @pl.when(i == 0)
    def _():
        ones = jnp.ones((vt_s.shape[1] - LANES, tk), BF16)
        for jj in range(n_kv):
            vt_s[jj, 0:LANES, :] = v_ref[0, jj * tk:(jj + 1) * tk, :].T.astype(BF16)
            vt_s[jj, LANES:, :] = ones

    q = q_ref[0]
    lane = lax.broadcasted_iota(jnp.int32, q.shape, 1)
    zero = jnp.zeros_like(q)
    qq_s[0:tq, :] = jnp.where(lane < LANES // 2, q, zero)
    qq_s[tq:2 * tq, :] = jnp.where(lane >= LANES // 2, q, zero)
    m_s[...] = jnp.full(m_s.shape, -jnp.inf, F32)
    acc_s[...] = jnp.zeros(acc_s.shape, F32)

    def scores(j, c):
        k = k_ref[0, pl.ds(pl.multiple_of(j * tk, tk), tk), :]
        return _dot_nt(k, qq_s[c * tk:(c + 1) * tk, :])

    def softmax(j, c, tri, s):
        cols = slice(c * tk, (c + 1) * tk)
        if tri:
            key = lax.broadcasted_iota(jnp.int32, s.shape, 0)
            qry = lax.broadcasted_iota(jnp.int32, s.shape, 1)
            s = jnp.where(key <= qry, s, -jnp.inf)
        m_old = m_s[:, cols]
        m_new = jnp.maximum(m_old, jnp.max(s, axis=0, keepdims=True))
        alpha = jnp.exp2(m_old - m_new)
        p = jnp.exp2(s - m_new)
        m_s[:, cols] = m_new
        return cols, alpha, _dot(vt_s[j], p.astype(BF16))

    def accumulate(cols, alpha, pv):
        acc_s[:, cols] = alpha * acc_s[:, cols] + pv

    def run(work):
        s = scores(*work[0][:2])
        pending = None
        for n, (j, c, tri) in enumerate(work):
            s_next = scores(*work[n + 1][:2]) if n + 1 < len(work) else None
            done = softmax(j, c, tri, s)
            if pending is not None:
                accumulate(*pending)
            pending = done
            s = s_next
        accumulate(*pending)

    def full_block(j, carry):
        run([(j, c, False) for c in range(2 * n_ck)])
        return carry

    lax.fori_loop(0, i * n_ck, full_block, 0)
    run([(i * n_ck + d, mp * n_ck + cq, cq == d)
         for d in range(n_ck) for mp in range(2) for cq in range(d, n_ck)])

    lam = _lambda(lam_ref, lam_init)
    a = acc_s[0:LANES, :] * (1.0 / acc_s[LANES:LANES + 1, :])
    o_t = a[:, 0:tq] - lam * a[:, tq:2 * tq]
    ms = jnp.mean(o_t * o_t, axis=0, keepdims=True)
    y = (o_t * lax.rsqrt(ms + EPS)).T
    o_ref[0] = (y * onorm_ref[...] * (1.0 - lam_init)).astype(BF16)


def _diff_attn_prompt(qd, kb, v, da_lambda, onorm, lam_init, tq, tk):
    B, S, W = qd.shape
    H = W // LANES
    qspec = pl.BlockSpec((1, tq, LANES), lambda b, h, i: (b, i, h))
    kvspec = pl.BlockSpec((1, S, LANES), lambda b, h, i: (b, 0, h))
    return pl.pallas_call(
        functools.partial(_diff_attn_kernel, lam_init=lam_init, tk=tk),
        grid=(B, H, S // tq),
        in_specs=[qspec, kvspec, kvspec, _const_spec(da_lambda.shape), _const_spec((1, LANES))],
        out_specs=qspec,
        out_shape=jax.ShapeDtypeStruct((B, S, W), BF16),
        scratch_shapes=[pltpu.VMEM((2 * tq, LANES), BF16),
                        pltpu.VMEM((S // tk, LANES + BF16_ROWS, tk), BF16),
                        pltpu.VMEM((1, 2 * tq), F32),
                        pltpu.VMEM((LANES + BF16_ROWS, 2 * tq), F32)],
        compiler_params=_cparams(("arbitrary",) * 3, 32 << 20),
        name="diff_attn_prompt",
    )(qd, kb, v, da_lambda, onorm)


def _cross_attn_heads(q, mk_head, mv_head):
    dh = q.shape[-1] // MEM_HEADS
    outs = []
    for h in range(MEM_HEADS):
        s = _dot_nt(q[:, h * dh:(h + 1) * dh], mk_head(h))
        m = jnp.max(s, axis=-1, keepdims=True)
        p = jnp.exp(s - m)
        l = jnp.sum(p, axis=-1, keepdims=True)
        outs.append(_dot((p / l).astype(BF16), mv_head(h)))
    return jnp.concatenate(outs, axis=-1)


def _out_cross_kernel(x_ref, mhg_ref, mda_ref, wo_ref, gq_ref, wq_ref, mk_ref, mv_ref,
                      wmo_ref, x2_ref):
    w = mhg_ref.shape[-1]
    x1 = x_ref[0] + _dot(mhg_ref[0], wo_ref[0:w, :]) + _dot(mda_ref[0], wo_ref[w:2 * w, :])
    hq = _rms(x1, gq_ref[...]).astype(BF16)
    dh = wq_ref.shape[1] // MEM_HEADS
    q = (_dot(hq, wq_ref[...]) * (float(dh) ** -0.5)).astype(BF16)
    o = _cross_attn_heads(q, lambda h: mk_ref[0, :, h * dh:(h + 1) * dh],
                          lambda h: mv_ref[0, :, h * dh:(h + 1) * dh])
    x2_ref[0] = x1 + _dot(o.astype(BF16), wmo_ref[...])


def _out_cross_prompt(x, mhg, mda, w_out, gq, w_mq, mkb, mvb, w_mo, ts):
    B, S, D = x.shape
    W = mhg.shape[-1]
    N, MW = mkb.shape[1], mkb.shape[2]
    tok = lambda d: pl.BlockSpec((1, ts, d), lambda b, s: (b, s, 0))
    mem = pl.BlockSpec((1, N, MW), lambda b, s: (b, 0, 0))
    return pl.pallas_call(
        _out_cross_kernel,
        grid=(B, S // ts),
        in_specs=[tok(D), tok(W), tok(W), _const_spec(w_out.shape), _const_spec((1, D)),
                  _const_spec(w_mq.shape), mem, mem, _const_spec(w_mo.shape)],
        out_specs=tok(D),
        out_shape=jax.ShapeDtypeStruct((B, S, D), F32),
        compiler_params=_cparams(("arbitrary", "arbitrary"), 48 << 20),
        name="out_cross_prompt",
    )(x, mhg, mda, w_out, gq, w_mq, mkb, mvb, w_mo)


def _out_cross_sample_kernel(x_ref, mhg_ref, mda_ref, wo_ref, gq_ref, wq_ref, mk_hbm, mv_hbm,
                             wmo_ref, x2_ref, x1_s, q_s, o_s, kbuf, vbuf, sem):
    b = pl.program_id(0)
    n_b = pl.num_programs(0)
    w = mhg_ref.shape[-1]

    def copies(bb, slot):
        out = []
        for h in range(MEM_HEADS):
            out.append(pltpu.make_async_copy(mk_hbm.at[bb, :, h, :], kbuf.at[slot, h], sem.at[slot, 0]))
            out.append(pltpu.make_async_copy(mv_hbm.at[bb, :, h, :], vbuf.at[slot, h], sem.at[slot, 1]))
        return out

    @pl.when(b == 0)
    def _():
        for cp in copies(0, 0):
            cp.start()
        x1 = x_ref[...] + _dot(mhg_ref[...], wo_ref[0:w, :]) + _dot(mda_ref[...], wo_ref[w:2 * w, :])
        x1_s[...] = x1
        hq = _rms(x1, gq_ref[...]).astype(BF16)
        dh = wq_ref.shape[1] // MEM_HEADS
        q_s[...] = _dot(hq, wq_ref[...]) * (float(dh) ** -0.5)

    slot = b % 2

    @pl.when(b + 1 < n_b)
    def _():
        for cp in copies(b + 1, 1 - slot):
            cp.start()

    for cp in copies(b, slot):
        cp.wait()
    q = q_s[pl.ds(b, 1), :].astype(BF16)
    o_s[pl.ds(b, 1), :] = _cross_attn_heads(q, lambda h: kbuf[slot, h].astype(BF16),
                                            lambda h: vbuf[slot, h].astype(BF16))

    @pl.when(b == n_b - 1)
    def _():
        x2_ref[...] = x1_s[...] + _dot(o_s[...].astype(BF16), wmo_ref[...])


def _out_cross_sample(x, mhg, mda, w_out, gq, w_mq, mem_k, mem_v, w_mo):
    T, D = x.shape
    _, N, heads, dh = mem_k.shape
    MW = heads * dh
    hbm = pl.BlockSpec(memory_space=pl.ANY)
    return pl.pallas_call(
        _out_cross_sample_kernel,
        grid=(T,),
        in_specs=[_const_spec(x.shape), _const_spec(mhg.shape), _const_spec(mda.shape),
                  _const_spec(w_out.shape), _const_spec((1, D)), _const_spec(w_mq.shape),
                  hbm, hbm, _const_spec(w_mo.shape)],
        out_specs=pl.BlockSpec((T, D), lambda b: (0, 0)),
        out_shape=jax.ShapeDtypeStruct((T, D), F32),
        scratch_shapes=[pltpu.VMEM((T, D), F32), pltpu.VMEM((T, MW), F32), pltpu.VMEM((T, MW), F32),
                        pltpu.VMEM((2, heads, N, dh), F32), pltpu.VMEM((2, heads, N, dh), F32),
                        pltpu.SemaphoreType.DMA((2, 2))],
        compiler_params=_cparams(("arbitrary",), 32 << 20),
        name="out_cross_sample",
    )(x, mhg, mda, w_out, gq, w_mq, mem_k, mem_v, w_mo)


def _ffn_kernel(x_ref, gf_ref, wg_ref, wu_ref, wd_ref, gfin_ref, y_ref):
    x = x_ref[...]
    h = _rms(x, gf_ref[...]).astype(BF16)
    g = _dot(h, wg_ref[...])
    u = _dot(h, wu_ref[...])
    a = (g * _sigmoid(g) * u).astype(BF16)
    x3 = x + _dot(a, wd_ref[...])
    y_ref[...] = _rms(x3, gfin_ref[...])


def _ffn(x, gf, wg, wu, wd, gfin, ts):
    T, D = x.shape
    tok = pl.BlockSpec((ts, D), lambda i: (i, 0))
    return pl.pallas_call(
        _ffn_kernel,
        grid=(T // ts,),
        in_specs=[tok, _const_spec((1, D)), _const_spec(wg.shape), _const_spec(wu.shape),
                  _const_spec(wd.shape), _const_spec((1, D))],
        out_specs=tok,
        out_shape=jax.ShapeDtypeStruct((T, D), F32),
        compiler_params=_cparams(("arbitrary",), VMEM_LIMIT_CAP),
        name="ffn",
    )(x, gf, wg, wu, wd, gfin)


def _sample_mix_kernel(x_ref, g_ref, w_ref, lb_ref, cos_ref, sin_ref,
                       qd_ref, k_ref, v_ref, hq_ref, kk_ref, f_ref, vh_ref, gate_ref):
    h = _rms(x_ref[...], g_ref[...]).astype(BF16)
    z = _dot(h, w_ref[...])
    lb = _lower_bound(lb_ref)
    hq, k_hg, logf, hi, gate, dq, dk, dv = _mixer_sections(z, lb)
    cos = cos_ref[...]
    sin = sin_ref[...]
    qd_ref[...] = _rope(dq, cos, sin) * (float(LANES // 2) ** -0.5)
    k_ref[...] = _rope(dk, cos, sin)
    v_ref[...] = dv
    hq_ref[...] = hq
    kk_ref[...] = k_hg
    f_ref[...] = jnp.exp(logf)
    vh_ref[...] = hi
    gate_ref[...] = gate


def _sample_mix(x, g, w_in, hg_lb, cos, sin):
    T, D = x.shape
    W = w_in.shape[1] // 7
    sds = jax.ShapeDtypeStruct
    return pl.pallas_call(
        _sample_mix_kernel,
        out_shape=[sds((T, W), F32)] * 8,
        compiler_params=_cparams(None, 32 << 20),
        name="sample_mix",
    )(x, g, w_in, hg_lb, cos, sin)


def _sample_hgrn_kernel(hq_ref, kk_ref, f_ref, vh_ref, gate_ref, onorm_ref, s0_ref,
                        mix_ref, s1_ref):
    nb = s0_ref.shape[0]
    g0 = pl.program_id(0) * nb
    W = hq_ref.shape[-1]

    def columns(ref, h):
        rows = ref[pl.ds(pl.multiple_of(g0, nb), nb), h * LANES:(h + 1) * LANES]
        pad = jnp.zeros((LANES - nb, LANES), F32)
        return jnp.concatenate([rows, pad], axis=0).T

    o_rows = []
    for h in range(HG_HEADS):
        q_t, k_t, f_t = columns(hq_ref, h), columns(kk_ref, h), columns(f_ref, h)
        v_rows = vh_ref[pl.ds(pl.multiple_of(g0, nb), nb), h * LANES:(h + 1) * LANES]
        o_h = []
        for j in range(nb):
            bcast = lambda tile: jnp.broadcast_to(tile[:, j:j + 1], (LANES, LANES))
            s_new = bcast(f_t) * s0_ref[j, h] + bcast(k_t) * v_rows[j:j + 1, :]
            s1_ref[j, h] = s_new
            o_h.append(jnp.sum(bcast(q_t) * s_new, axis=0, keepdims=True))
        o_rows.append(jnp.concatenate(o_h, axis=0))
    o = jnp.concatenate(o_rows, axis=-1)
    gate = gate_ref[pl.ds(pl.multiple_of(g0, nb), nb), :]
    mix_ref[...] = (_head_rms(o, onorm_ref[...]) * gate).astype(BF16)


def _sample_hgrn(hq, kk, f, vh, gate, onorm, s0, nb=8):
    T, W = hq.shape
    st = pl.BlockSpec((nb, HG_HEADS, LANES, LANES), lambda i: (i, 0, 0, 0))
    full = _const_spec((T, W))
    return pl.pallas_call(
        _sample_hgrn_kernel,
        grid=(T // nb,),
        in_specs=[full] * 5 + [_const_spec((1, LANES)), st],
        out_specs=[pl.BlockSpec((nb, W), lambda i: (i, 0)), st],
        out_shape=[jax.ShapeDtypeStruct((T, W), BF16), jax.ShapeDtypeStruct(s0.shape, F32)],
        compiler_params=_cparams(("arbitrary",), 32 << 20),
        name="sample_hgrn",
    )(hq, kk, f, vh, gate, onorm, s0)


def _decode_seed(q, k_new, v_new, state):
    q8_s, m_s, l_s, acc_s = state
    H = DA_HEADS

    def head_rows(x):
        return jnp.concatenate([x[:, h * LANES:(h + 1) * LANES] for h in range(H)] * 2, axis=0)

    row = lax.broadcasted_iota(jnp.int32, (2 * H, LANES), 0)
    lane = lax.broadcasted_iota(jnp.int32, (2 * H, LANES), 1)
    q8 = jnp.where((row // H) == (lane // (LANES // 2)), head_rows(q), 0.0)
    q8_s[...] = q8.astype(BF16)
    m_s[...] = jnp.sum(q8 * head_rows(k_new), axis=-1, keepdims=True)
    l_s[...] = jnp.ones(l_s.shape, F32)
    acc_s[...] = head_rows(v_new)


def _decode_pages(k_pages, v_pages, state):
    q8_s, m_s, l_s, acc_s = state
    H = DA_HEADS
    q8 = q8_s[...]
    s = jnp.concatenate([_dot_nt(q8, k.astype(BF16)) for k in k_pages], axis=-1)
    row = lax.broadcasted_iota(jnp.int32, s.shape, 0)
    col = lax.broadcasted_iota(jnp.int32, s.shape, 1)
    s = jnp.where((col % H) == (row % H), s, -jnp.inf)
    m_old = m_s[...]
    m_new = jnp.maximum(m_old, jnp.max(s, axis=-1, keepdims=True))
    alpha = jnp.exp(m_old - m_new)
    pr = jnp.exp(s - m_new)
    l_s[...] = alpha * l_s[...] + jnp.sum(pr, axis=-1, keepdims=True)
    pr = pr.astype(BF16)
    rows = k_pages[0].shape[0]
    pv = _dot(pr[:, 0:rows], v_pages[0].astype(BF16))
    for p in range(1, len(v_pages)):
        pv = pv + _dot(pr[:, p * rows:(p + 1) * rows], v_pages[p].astype(BF16))
    acc_s[...] = alpha * acc_s[...] + pv
    m_s[...] = m_new


def _decode_finish(lam_ref, onorm_ref, lam_init, state):
    _, _, l_s, acc_s = state
    H = DA_HEADS
    lam = _lambda(lam_ref, lam_init)
    o = acc_s[...] / l_s[...]
    y = _rms(o[0:H] - lam * o[H:2 * H], onorm_ref[...]) * (1.0 - lam_init)
    return jnp.concatenate([y[h:h + 1, :] for h in range(H)], axis=-1).astype(BF16)


def _paged_attn_kernel(pt_ref, q_ref, kn_ref, vn_ref, lam_ref, onorm_ref, *refs,
                       lam_init, n_pg):
    k_refs = refs[:n_pg]
    v_refs = refs[n_pg:2 * n_pg]
    o_ref = refs[2 * n_pg]
    state = refs[2 * n_pg + 1:]
    j = pl.program_id(1)

    @pl.when(j == 0)
    def _():
        _decode_seed(q_ref[0], kn_ref[0], vn_ref[0], state)

    _decode_pages([r[0] for r in k_refs], [r[0] for r in v_refs], state)

    @pl.when(j == pl.num_programs(1) - 1)
    def _():
        o_ref[0] = _decode_finish(lam_ref, onorm_ref, lam_init, state)


def _paged_attn(page_table, qd, k_new, v_new, da_lambda, onorm, cache_k, cache_v, lam_init, n_pg):
    T, n_pages = page_table.shape
    n_phys, rows, dk = cache_k.shape
    W = qd.shape[-1]
    row = pl.BlockSpec((1, 1, W), lambda b, j, pt: (b, 0, 0))

    def page_spec(p):
        return pl.BlockSpec((1, rows, dk), lambda b, j, pt: (pt[b, j * n_pg + p], 0, 0))

    const = lambda shape: pl.BlockSpec(shape, lambda b, j, pt: (0,) * len(shape))
    grid_spec = pltpu.PrefetchScalarGridSpec(
        num_scalar_prefetch=1,
        grid=(T, n_pages // n_pg),
        in_specs=[row, row, row, const(da_lambda.shape), const((1, LANES))]
        + [page_spec(p) for p in range(n_pg)] * 2,
        out_specs=row,
        scratch_shapes=[pltpu.VMEM((2 * DA_HEADS, LANES), BF16),
                        pltpu.VMEM((2 * DA_HEADS, 1), F32),
                        pltpu.VMEM((2 * DA_HEADS, 1), F32),
                        pltpu.VMEM((2 * DA_HEADS, LANES), F32)],
    )
    out = pl.pallas_call(
        functools.partial(_paged_attn_kernel, lam_init=lam_init, n_pg=n_pg),
        grid_spec=grid_spec,
        out_shape=jax.ShapeDtypeStruct((T, 1, W), BF16),
        compiler_params=_cparams(("arbitrary", "arbitrary"), 48 << 20),
        name="paged_diff_attn",
    )(page_table, qd.reshape(T, 1, W), k_new.reshape(T, 1, W), v_new.reshape(T, 1, W),
      da_lambda, onorm, *([cache_k] * n_pg), *([cache_v] * n_pg))
    return out.reshape(T, W)


def _ffn_decode_kernel(pt_ref, x_ref, gf_ref, wg_ref, wu_ref, wd_ref, gfin_ref,
                       q_ref, kn_ref, vn_ref, lam_ref, onorm_ref, ck_hbm, cv_hbm,
                       y_ref, o_ref, kbuf, vbuf, sem, q8_s, m_s, l_s, acc_s,
                       *, lam_init, n_pg, n_slot, sps, col_bounds):
    i = pl.program_id(0)
    n_steps = pl.num_programs(0)
    state = (q8_s, m_s, l_s, acc_s)
    rows = kbuf.shape[1] // n_pg
    gps = pt_ref.shape[1] // n_pg
    groups = [(ls, g) for ls in range(sps) for g in range(gps)]
    n_grp = len(groups)

    def copies(b, g, slot):
        out = []
        for p in range(n_pg):
            page = pt_ref[b, g * n_pg + p]
            dst = pl.ds(p * rows, rows)
            out.append(pltpu.make_async_copy(ck_hbm.at[page], kbuf.at[slot, dst], sem.at[slot, 0]))
            out.append(pltpu.make_async_copy(cv_hbm.at[page], vbuf.at[slot, dst], sem.at[slot, 1]))
        return out

    def start(b, g, slot):
        for cp in copies(b, g, slot):
            cp.start()

    ahead = n_slot - 1

    @pl.when(i == 0)
    def _():
        for n in range(ahead):
            start(groups[n][0], groups[n][1], n % n_slot)

    def fold_group(n):
        ls, g = groups[n]
        b = i * sps + ls
        slot = n % n_slot
        nxt = n + ahead
        if nxt < n_grp:
            start(i * sps + groups[nxt][0], groups[nxt][1], nxt % n_slot)
        else:
            @pl.when(i + 1 < n_steps)
            def _():
                ls2, g2 = groups[nxt - n_grp]
                start((i + 1) * sps + ls2, g2, nxt % n_slot)
        for cp in copies(b, g, slot):
            cp.wait()
        if g == 0:
            _decode_seed(q_ref[b], kn_ref[b], vn_ref[b], state)
        _decode_pages([kbuf[slot, p * rows:(p + 1) * rows, :] for p in range(n_pg)],
                      [vbuf[slot, p * rows:(p + 1) * rows, :] for p in range(n_pg)], state)
        if g == gps - 1:
            o_ref[b] = _decode_finish(lam_ref, onorm_ref, lam_init, state)

    x = x_ref[...]
    h = _rms(x, gf_ref[...]).astype(BF16)
    n_piece = len(col_bounds) - 1
    acc = None
    for k in range(n_piece):
        lo, hi = col_bounds[k], col_bounds[k + 1]
        g = _dot(h, wg_ref[:, lo:hi])
        u = _dot(h, wu_ref[:, lo:hi])
        part = _dot((g * _sigmoid(g) * u).astype(BF16), wd_ref[lo:hi, :])
        acc = part if acc is None else acc + part
        for n in range(k * n_grp // n_piece, (k + 1) * n_grp // n_piece):
            fold_group(n)
    y_ref[...] = _rms(x + acc, gfin_ref[...])


def _ffn_decode(x, gf, wg, wu, wd, gfin, ts, page_table, qd, k_new, v_new, da_lambda, onorm,
                cache_k, cache_v, lam_init, n_pg, n_slot):
    T, D = x.shape
    F = wg.shape[1]
    Ts, n_pages = page_table.shape
    n_phys, rows, dk = cache_k.shape
    W = qd.shape[-1]
    n_tiles = T // ts
    sps = Ts // n_tiles
    mxu_n = 2 * LANES
    piece = max(mxu_n, (F // 4) // mxu_n * mxu_n)
    col_bounds = tuple(range(0, F, piece)) + (F,)
    tok = pl.BlockSpec((ts, D), lambda i, pt: (i, 0))
    const = lambda shape: pl.BlockSpec(shape, lambda i, pt: (0,) * len(shape),
                                       pipeline_mode=pl.Buffered(1))
    sample = const((Ts, 1, W))
    hbm = pl.BlockSpec(memory_space=pl.ANY)
    grid_spec = pltpu.PrefetchScalarGridSpec(
        num_scalar_prefetch=1,
        grid=(n_tiles,),
        in_specs=[tok, const((1, D)), const(wg.shape), const(wu.shape), const(wd.shape),
                  const((1, D)), sample, sample, sample, const(da_lambda.shape),
                  const((1, LANES)), hbm, hbm],
        out_specs=[tok, pl.BlockSpec((Ts, 1, W), lambda i, pt: (0, 0, 0))],
        scratch_shapes=[pltpu.VMEM((n_slot, n_pg * rows, dk), F32),
                        pltpu.VMEM((n_slot, n_pg * rows, dk), F32),
                        pltpu.SemaphoreType.DMA((n_slot, 2)),
                        pltpu.VMEM((2 * DA_HEADS, LANES), BF16),
                        pltpu.VMEM((2 * DA_HEADS, 1), F32),
                        pltpu.VMEM((2 * DA_HEADS, 1), F32),
                        pltpu.VMEM((2 * DA_HEADS, LANES), F32)],
    )
    y, o = pl.pallas_call(
        functools.partial(_ffn_decode_kernel, lam_init=lam_init, n_pg=n_pg, n_slot=n_slot, sps=sps,
                          col_bounds=col_bounds),
        grid_spec=grid_spec,
        out_shape=[jax.ShapeDtypeStruct((T, D), F32), jax.ShapeDtypeStruct((Ts, 1, W), BF16)],
        compiler_params=_cparams(("arbitrary",), VMEM_LIMIT_CAP),
        name="ffn_decode",
    )(page_table, x, gf, wg, wu, wd, gfin, qd.reshape(Ts, 1, W), k_new.reshape(Ts, 1, W),
      v_new.reshape(Ts, 1, W), da_lambda, onorm, cache_k, cache_v)
    return y, o.reshape(Ts, W)


def _rope_tables(pos, dh, n_maps):
    inv = ROPE_THETA ** (-jnp.arange(0, dh, 2, dtype=F32) / dh)
    ang = pos.astype(F32)[:, None] * inv[None, :]
    cos = jnp.cos(ang)
    sin = jnp.sin(ang)
    cos = jnp.tile(jnp.concatenate([cos, cos], axis=-1), (1, n_maps))
    sin = jnp.tile(jnp.concatenate([-sin, sin], axis=-1), (1, n_maps))
    return cos, sin


def _pick_tile(n, pref):
    t = min(n, pref)
    while n % t:
        t //= 2
    return t


def kernel(x_prompt, x_sample, mem_prompt, cache_k, cache_v, cache_mem_k, cache_mem_v, state_hgrn, page_table, norm_mix, w_in, hg_lb, hg_onorm, da_lambda, da_onorm, w_out, norm_mem_q, norm_mem_kv, w_mq, w_mk, w_mv, w_mo, norm_ffn, w_gate, w_up, w_down, norm_final):
    B, S, D = x_prompt.shape
    T = x_sample.shape[0]
    depth = w_in.shape[0]
    assert depth == 1 and x_sample.shape[1] == 1
    l = 0
    lam_init = 0.8 - 0.6 * math.exp(-0.3 * l)
    n_phys, page = cache_k.shape[1], cache_k.shape[2]
    past_len = page_table.shape[1] * page
    W = w_in.shape[2] // 7

    bf = lambda w: w.astype(BF16)
    w_in_b, w_out_b = bf(w_in[l]), bf(w_out[l])
    w_mq_b, w_mk_b, w_mv_b, w_mo_b = bf(w_mq[l]), bf(w_mk[l]), bf(w_mv[l]), bf(w_mo[l])
    w_gate_b, w_up_b, w_down_b = bf(w_gate[l]), bf(w_up[l]), bf(w_down[l])
    row = lambda g: g.reshape(1, -1)
    lb2 = hg_lb[l:l + 2]
    lam_p = da_lambda[l]

    cos_p, sin_p = _rope_tables(jnp.arange(S), LANES // 2, 2 * DA_HEADS)
    cos_s, sin_s = _rope_tables(past_len + jnp.arange(1), LANES // 2, 2 * DA_HEADS)
    cos_s = jnp.broadcast_to(cos_s, (T, W))
    sin_s = jnp.broadcast_to(sin_s, (T, W))

    mk, mv, mkb, mvb = _mem_kv(mem_prompt, row(norm_mem_kv[l]), w_mk_b, w_mv_b)
    qd, k_p, v_p, kb, v_att, mix_hg, hs_p = _prompt_mix(
        x_prompt, row(norm_mix[l]), w_in_b, lb2, cos_p, sin_p, row(hg_onorm[l]),
        ts=_pick_tile(S, 256))
    mix_da = _diff_attn_prompt(qd, kb, v_att, lam_p, row(da_onorm[l]), lam_init,
                               tq=_pick_tile(S, 1024), tk=_pick_tile(S, 256))
    x2 = _out_cross_prompt(x_prompt, mix_hg, mix_da, w_out_b, row(norm_mem_q[l]), w_mq_b,
                           mkb, mvb, w_mo_b, ts=_pick_tile(S, 512))

    xs = x_sample.reshape(T, D)
    qd_s, k_s, v_s, hq_s, kk_s, f_s, vh_s, gate_s = _sample_mix(
        xs, row(norm_mix[l]), w_in_b, lb2, cos_s, sin_s)
    mix_hg_s, hs_s = _sample_hgrn(hq_s, kk_s, f_s, vh_s, gate_s, row(hg_onorm[l]), state_hgrn[l])
    ck = cache_k.reshape(depth * n_phys, page * DA_HEADS, LANES)
    cv = cache_v.reshape(depth * n_phys, page * DA_HEADS, LANES)
    n_pages = page_table.shape[1]
    ffn_ts = _pick_tile(B * S, 512)
    n_tiles = (B * S) // ffn_ts
    n_pg = _pick_tile(n_pages, 8)
    n_slot = 4
    ffn_args = (row(norm_ffn[l]), w_gate_b, w_up_b, w_down_b, row(norm_final))
    if T % n_tiles == 0 and ((T // n_tiles) * (n_pages // n_pg)) % n_slot == 0:
        y_p, mix_da_s = _ffn_decode(x2.reshape(B * S, D), *ffn_args, ffn_ts, page_table, qd_s, k_s,
                                    v_s, lam_p, row(da_onorm[l]), ck, cv, lam_init, n_pg, n_slot)
    else:
        y_p = _ffn(x2.reshape(B * S, D), *ffn_args, ts=ffn_ts)
        mix_da_s = _paged_attn(page_table, qd_s, k_s, v_s, lam_p, row(da_onorm[l]), ck, cv,
                               lam_init, n_pg=_pick_tile(n_pages, 16))
    y_p = y_p.reshape(B, S, D)
    MW = w_mk.shape[2]
    x2_s = _out_cross_sample(xs, mix_hg_s, mix_da_s, w_out_b, row(norm_mem_q[l]), w_mq_b,
                             cache_mem_k[l], cache_mem_v[l],
                             w_mo_b)
    y_s = _ffn(x2_s, row(norm_ffn[l]), w_gate_b, w_up_b, w_down_b, row(norm_final), ts=T)

    dk = LANES
    return (y_p, y_s.reshape(T, 1, D),
            hs_p[None],
            k_p.reshape(1, B, S, DA_HEADS, dk), v_p.reshape(1, B, S, DA_HEADS, dk),
            mk[None], mv[None],
            hs_s[None],
            k_s.reshape(1, T, 1, DA_HEADS, dk), v_s.reshape(1, T, 1, DA_HEADS, dk))
```

```python
import functools
import math

import jax
import jax.numpy as jnp
from jax import lax
from jax.experimental import pallas as pl
from jax.experimental.pallas import tpu as pltpu

F32 = jnp.float32
BF16 = jnp.bfloat16
EPS = 1e-6
ROPE_THETA = 10000.0

HG_HEADS = 4
DA_HEADS = 4
MEM_HEADS = 4
HG_CHUNK = 64
LANES = 128
BF16_ROWS = 16
VMEM_LIMIT_CAP = 56 << 20

_NT = (((1,), (1,)), ((), ()))
_TN = (((0,), (0,)), ((), ()))


def _dot(a, b):
    return jnp.dot(a, b, preferred_element_type=F32)


def _dot_nt(a, b):
    return lax.dot_general(a, b, _NT, preferred_element_type=F32)


def _dot_tn(a, b):
    return lax.dot_general(a, b, _TN, preferred_element_type=F32)


def _rms(x, g):
    ms = jnp.mean(x * x, axis=-1, keepdims=True)
    return x * lax.rsqrt(ms + EPS) * g


def _sigmoid(x):
    return 1.0 / (1.0 + jnp.exp(-x))


def _cparams(semantics, vmem_bytes):
    return pltpu.CompilerParams(
        dimension_semantics=semantics,
        vmem_limit_bytes=int(min(max(vmem_bytes, 16 << 20), VMEM_LIMIT_CAP)))


def _const_spec(shape):
    nd = len(shape)
    return pl.BlockSpec(shape, lambda *_: (0,) * nd, pipeline_mode=pl.Buffered(1))


def _lower_bound(lb_ref):
    a0 = lb_ref[0:1, :]
    a1 = lb_ref[1:2, :]
    m = jnp.maximum(a0, a1)
    e0 = jnp.exp(a0 - m)
    e1 = jnp.exp(a1 - m)
    return e0 / (e0 + e1)


def _rope(x, cos, sin_signed):
    n = x.shape[-1]
    lane = lax.broadcasted_iota(jnp.int32, x.shape, x.ndim - 1)
    swapped = jnp.where((lane & 63) < 32,
                        pltpu.roll(x, n - 32, x.ndim - 1),
                        pltpu.roll(x, 32, x.ndim - 1))
    return x * cos + swapped * sin_signed


def _lambda(lam_ref, lam_init):
    lp = lam_ref[...]
    s01 = jnp.sum(lp[0:1, :] * lp[1:2, :], axis=-1, keepdims=True)
    s23 = jnp.sum(lp[2:3, :] * lp[3:4, :], axis=-1, keepdims=True)
    return jnp.exp(s01) - jnp.exp(s23) + lam_init


def _mixer_sections(z, lb):
    w = z.shape[-1] // 7
    hq, zf, hi, hg, dq, dk, dv = (z[:, i * w:(i + 1) * w] for i in range(7))
    sig = _sigmoid(zf)
    logf = jnp.log(lb + (1.0 - lb) * sig)
    k_hg = (1.0 - lb) * (1.0 - sig)
    gate = hg * _sigmoid(hg)
    return hq, k_hg, logf, hi, gate, dq, dk, dv


def _head_rms(o, g):
    outs = []
    for h in range(o.shape[-1] // LANES):
        oh = o[:, h * LANES:(h + 1) * LANES]
        outs.append(_rms(oh, g))
    return jnp.concatenate(outs, axis=-1)


def _memkv_kernel(mem_ref, g_ref, wk_ref, wv_ref, k_ref, v_ref, kb_ref, vb_ref):
    m = _rms(mem_ref[0], g_ref[...]).astype(BF16)
    k = _dot(m, wk_ref[...])
    v = _dot(m, wv_ref[...])
    dh = k_ref.shape[-1]
    for h in range(MEM_HEADS):
        k_ref[0, :, h, :] = k[:, h * dh:(h + 1) * dh]
        v_ref[0, :, h, :] = v[:, h * dh:(h + 1) * dh]
    kb_ref[0] = k.astype(BF16)
    vb_ref[0] = v.astype(BF16)


def _mem_kv(mem, g, wk, wv):
    B, N, D = mem.shape
    W = wk.shape[1]
    dh = W // MEM_HEADS
    blk = lambda d: pl.BlockSpec((1, N, d), lambda b: (b, 0, 0))
    blk4 = pl.BlockSpec((1, N, MEM_HEADS, dh), lambda b: (b, 0, 0, 0))
    return pl.pallas_call(
        _memkv_kernel,
        grid=(B,),
        in_specs=[blk(D), _const_spec((1, D)), _const_spec((D, W)), _const_spec((D, W))],
        out_specs=[blk4, blk4, blk(W), blk(W)],
        out_shape=[jax.ShapeDtypeStruct((B, N, MEM_HEADS, dh), F32)] * 2
        + [jax.ShapeDtypeStruct((B, N, W), BF16)] * 2,
        compiler_params=_cparams(("arbitrary",), 40 << 20),
        name="mem_kv",
    )(mem, g, wk, wv)


def _hgrn_levels(q, kk, lf, v):
    C, W = q.shape
    t = lax.broadcasted_iota(jnp.int32, (C, W), 0)
    n_lvl = int(math.log2(C))

    c = lf
    lvls = []
    for l in range(n_lvl):
        m = 1 << l
        upper = (t & m) != 0
        if m < 8:
            y = c
            for i in range(l):
                y = jnp.where((t & (1 << i)) == 0, pltpu.roll(y, C - (1 << i), 0), y)
            bc = jnp.where(upper, pltpu.roll(y, m, 0), y)
        else:
            pieces = []
            for j in range(C // (2 * m)):
                r = 2 * m * j + m - 1
                pieces.append(jnp.broadcast_to(c[r:r + 1, :], (2 * m, W)))
            bc = pieces[0] if len(pieces) == 1 else jnp.concatenate(pieces, axis=0)
        e = jnp.exp(jnp.where(upper, c, bc - c))
        lvls.append((jnp.where(upper, q, kk) * e).astype(BF16))
        c = c + jnp.where(upper, bc, 0.0)
    b = c
    b_last = b[C - 1:C, :]
    return dict(lvls=lvls, q=q.astype(BF16), k=kk.astype(BF16), v=v.astype(BF16),
                q_state=(q * jnp.exp(b)).astype(BF16),
                k_state=(kk * jnp.exp(b_last - b)).astype(BF16),
                decay=jnp.exp(b_last))


def _hgrn_products(lv, st_ref):
    out = []
    for h in range(lv["q"].shape[1] // LANES):
        sl = slice(h * LANES, (h + 1) * LANES)
        pairs = [_dot_nt(lv["q"][:, sl], lv["k"][:, sl])]
        pairs += [_dot_nt(r[:, sl], r[:, sl]) for r in lv["lvls"]]
        o_state = _dot_nt(lv["q_state"][:, sl], st_ref[h].astype(BF16))
        st_term = _dot_tn(lv["v"][:, sl], lv["k_state"][:, sl])
        out.append((pairs, o_state, st_term))
    return out


def _hgrn_finish(lv, prods, st_ref):
    C = lv["q"].shape[0]
    ti = lax.broadcasted_iota(jnp.int32, (C, C), 0)
    si = lax.broadcasted_iota(jnp.int32, (C, C), 1)
    diff_bits = jnp.where(ti > si, ti ^ si, 0)
    outs = []
    for h, (pairs, o_state, st_term) in enumerate(prods):
        sl = slice(h * LANES, (h + 1) * LANES)
        a = jnp.where(ti == si, pairs[0], 0.0)
        for l, pr in enumerate(pairs[1:]):
            a = a + jnp.where((diff_bits >> l) == 1, pr, 0.0)
        outs.append(_dot(a.astype(BF16), lv["v"][:, sl]) + o_state)
        st_ref[h] = st_ref[h] * lv["decay"][:, sl] + st_term
    return jnp.concatenate(outs, axis=-1)


def _prompt_mix_kernel(x_ref, g_ref, w_ref, lb_ref, cos_ref, sin_ref, onorm_ref,
                       qd_ref, k_ref, v_ref, kb_ref, va_ref, mix_ref, state_ref,
                       o_s, st_s):
    s = pl.program_id(1)
    last = pl.num_programs(1) - 1

    @pl.when(s == 0)
    def _():
        st_s[...] = jnp.zeros_like(st_s)

    h = _rms(x_ref[0], g_ref[...]).astype(BF16)

    ts = x_ref.shape[1]
    sec_w = w_ref.shape[1] // 7
    n_chunk = ts // HG_CHUNK
    stage = {"done": 0, "lv": None}

    def project(sec):
        return _dot(h, w_ref[:, sec * sec_w:(sec + 1) * sec_w])

    def levels(ci):
        rows = slice(ci * HG_CHUNK, (ci + 1) * HG_CHUNK)
        return _hgrn_levels(hq[rows, :], k_hg[rows, :], logf[rows, :], hi[rows, :])

    def recur(upto):
        for ci in range(stage["done"], min(upto, n_chunk)):
            lv = stage["lv"] if stage["lv"] is not None else levels(ci)
            prods = _hgrn_products(lv, st_s)
            stage["lv"] = levels(ci + 1) if ci + 1 < n_chunk else None
            o_s[ci * HG_CHUNK:(ci + 1) * HG_CHUNK, :] = _hgrn_finish(lv, prods, st_s)
        stage["done"] = max(stage["done"], min(upto, n_chunk))

    def head_rows_out(ref, val):
        for hh in range(DA_HEADS):
            ref[0, pl.ds(hh, ts, stride=DA_HEADS), :] = val[:, hh * LANES:(hh + 1) * LANES]

    lb = _lower_bound(lb_ref)
    cos = cos_ref[...]
    sin = sin_ref[...]
    hq = project(0)
    sig = _sigmoid(project(1))
    logf = jnp.log(lb + (1.0 - lb) * sig)
    k_hg = (1.0 - lb) * (1.0 - sig)
    hi = project(2)
    hg = project(3)
    recur(n_chunk // 4)
    q_da = _rope(project(4), cos, sin)
    qd_ref[0] = (q_da * (math.log2(math.e) * float(LANES // 2) ** -0.5)).astype(BF16)
    recur(n_chunk // 2)
    k_da = _rope(project(5), cos, sin)
    head_rows_out(k_ref, k_da)
    kb_ref[0] = k_da.astype(BF16)
    recur(3 * n_chunk // 4)
    dv = project(6)
    head_rows_out(v_ref, dv)
    va_ref[0] = dv
    recur(n_chunk)
    gate = hg * _sigmoid(hg)
    mix_ref[0] = (_head_rms(o_s[...], onorm_ref[...]) * gate).astype(BF16)

    @pl.when(s == last)
    def _():
        for hh in range(HG_HEADS):
            state_ref[0, hh] = st_s[hh].T


def _prompt_mix(x, g, w_in, hg_lb, cos, sin, onorm, ts):
    B, S, D = x.shape
    W = w_in.shape[1] // 7
    tok = lambda d: pl.BlockSpec((1, ts, d), lambda b, s: (b, s, 0))
    tab = pl.BlockSpec((ts, W), lambda b, s: (s, 0))
    rows = pl.BlockSpec((1, ts * DA_HEADS, LANES), lambda b, s: (b, s, 0))
    state = pl.BlockSpec((1, HG_HEADS, LANES, LANES), lambda b, s: (b, 0, 0, 0))
    sds = jax.ShapeDtypeStruct
    return pl.pallas_call(
        _prompt_mix_kernel,
        grid=(B, S // ts),
        in_specs=[tok(D), _const_spec((1, D)), _const_spec(w_in.shape),
                  _const_spec(hg_lb.shape), tab, tab, _const_spec((1, LANES))],
        out_specs=[tok(W), rows, rows, tok(W), tok(W), tok(W), state],
        out_shape=[sds((B, S, W), BF16), sds((B, S * DA_HEADS, LANES), F32),
                   sds((B, S * DA_HEADS, LANES), F32), sds((B, S, W), BF16),
                   sds((B, S, W), F32), sds((B, S, W), BF16),
                   sds((B, HG_HEADS, LANES, LANES), F32)],
        scratch_shapes=[pltpu.VMEM((ts, W), F32), pltpu.VMEM((HG_HEADS, LANES, LANES), F32)],
        compiler_params=_cparams(("arbitrary", "arbitrary"), 48 << 20),
        name="prompt_mix",
    )(x, g, w_in, hg_lb, cos, sin, onorm)


def _diff_attn_kernel(q_ref, k_ref, v_ref, lam_ref, onorm_ref, o_ref,
                      qq_s, vt_s, m_s, acc_s, *, lam_init, tk):
    i = pl.program_id(2)
    tq = q_ref.shape[1]
    n_ck = tq // tk
    n_kv = v_ref.shape[1] // tk

    @pl.when(i == 0)
    def _():
        ones = jnp.ones((vt_s.shape[1] - LANES, tk), BF16)
        for jj in range(n_kv):
            vt_s[jj, 0:LANES, :] = v_ref[0, jj * tk:(jj + 1) * tk, :].T.astype(BF16)
            vt_s[jj, LANES:, :] = ones

    q = q_ref[0]
    lane = lax.broadcasted_iota(jnp.int32, q.shape, 1)
    zero = jnp.zeros_like(q)
    qq_s[0:tq, :] = jnp.where(lane < LANES // 2, q, zero)
    qq_s[tq:2 * tq, :] = jnp.where(lane >= LANES // 2, q, zero)
    m_s[...] = jnp.full(m_s.shape, -jnp.inf, F32)
    acc_s[...] = jnp.zeros(acc_s.shape, F32)

    def scores(j, c):
        k = k_ref[0, pl.ds(pl.multiple_of(j * tk, tk), tk), :]
        return _dot_nt(k, qq_s[c * tk:(c + 1) * tk, :])

    def softmax(j, c, tri, s):
        cols = slice(c * tk, (c + 1) * tk)
        if tri:
            key = lax.broadcasted_iota(jnp.int32, s.shape, 0)
            qry = lax.broadcasted_iota(jnp.int32, s.shape, 1)
            s = jnp.where(key <= qry, s, -jnp.inf)
        m_old = m_s[:, cols]
        m_new = jnp.maximum(m_old, jnp.max(s, axis=0, keepdims=True))
        alpha = jnp.exp2(m_old - m_new)
        p = jnp.exp2(s - m_new)
        m_s[:, cols] = m_new
        return cols, alpha, _dot(vt_s[j], p.astype(BF16))

    def accumulate(cols, alpha, pv):
        acc_s[:, cols] = alpha * acc_s[:, cols] + pv

    ahead = 4

    def run(work):
        s = {n: scores(*work[n][:2]) for n in range(min(ahead, len(work)))}
        pending = None
        for n, (j, c, tri) in enumerate(work):
            if n + ahead < len(work):
                s[n + ahead] = scores(*work[n + ahead][:2])
            done = softmax(j, c, tri, s.pop(n))
            if pending is not None:
                accumulate(*pending)
            pending = done
        accumulate(*pending)

    per_iter = 2 if n_ck % 2 == 0 else 1

    def full_blocks(jj, carry):
        run([(jj * per_iter + r, c, False) for r in range(per_iter) for c in range(2 * n_ck)])
        return carry

    lax.fori_loop(0, (i * n_ck) // per_iter, full_blocks, 0)
    run([(i * n_ck + d, mp * n_ck + cq, cq == d)
         for d in range(n_ck) for mp in range(2) for cq in range(d, n_ck)])

    lam = _lambda(lam_ref, lam_init)
    a = acc_s[0:LANES, :] * (1.0 / acc_s[LANES:LANES + 1, :])
    o_t = a[:, 0:tq] - lam * a[:, tq:2 * tq]
    ms = jnp.mean(o_t * o_t, axis=0, keepdims=True)
    y = (o_t * lax.rsqrt(ms + EPS)).T
    o_ref[0] = (y * onorm_ref[...] * (1.0 - lam_init)).astype(BF16)


def _diff_attn_prompt(qd, kb, v, da_lambda, onorm, lam_init, tq, tk):
    B, S, W = qd.shape
    H = W // LANES
    qspec = pl.BlockSpec((1, tq, LANES), lambda b, h, i: (b, i, h))
    kvspec = pl.BlockSpec((1, S, LANES), lambda b, h, i: (b, 0, h))
    return pl.pallas_call(
        functools.partial(_diff_attn_kernel, lam_init=lam_init, tk=tk),
        grid=(B, H, S // tq),
        in_specs=[qspec, kvspec, kvspec, _const_spec(da_lambda.shape), _const_spec((1, LANES))],
        out_specs=qspec,
        out_shape=jax.ShapeDtypeStruct((B, S, W), BF16),
        scratch_shapes=[pltpu.VMEM((2 * tq, LANES), BF16),
                        pltpu.VMEM((S // tk, LANES + BF16_ROWS, tk), BF16),
                        pltpu.VMEM((1, 2 * tq), F32),
                        pltpu.VMEM((LANES + BF16_ROWS, 2 * tq), F32)],
        compiler_params=_cparams(("arbitrary",) * 3, 32 << 20),
        name="diff_attn_prompt",
    )(qd, kb, v, da_lambda, onorm)


def _cross_attn_heads(q, mk_head, mv_head):
    dh = q.shape[-1] // MEM_HEADS
    outs = []
    for h in range(MEM_HEADS):
        s = _dot_nt(q[:, h * dh:(h + 1) * dh], mk_head(h))
        m = jnp.max(s, axis=-1, keepdims=True)
        p = jnp.exp(s - m)
        l = jnp.sum(p, axis=-1, keepdims=True)
        outs.append(_dot((p / l).astype(BF16), mv_head(h)))
    return jnp.concatenate(outs, axis=-1)


def _out_cross_kernel(x_ref, mhg_ref, mda_ref, wo_ref, gq_ref, wq_ref, mk_ref, mv_ref,
                      wmo_ref, x2_ref):
    w = mhg_ref.shape[-1]
    x1 = x_ref[0] + _dot(mhg_ref[0], wo_ref[0:w, :]) + _dot(mda_ref[0], wo_ref[w:2 * w, :])
    hq = _rms(x1, gq_ref[...]).astype(BF16)
    dh = wq_ref.shape[1] // MEM_HEADS
    q = (_dot(hq, wq_ref[...]) * (float(dh) ** -0.5)).astype(BF16)
    o = _cross_attn_heads(q, lambda h: mk_ref[0, :, h * dh:(h + 1) * dh],
                          lambda h: mv_ref[0, :, h * dh:(h + 1) * dh])
    x2_ref[0] = x1 + _dot(o.astype(BF16), wmo_ref[...])


def _out_cross_prompt(x, mhg, mda, w_out, gq, w_mq, mkb, mvb, w_mo, ts):
    B, S, D = x.shape
    W = mhg.shape[-1]
    N, MW = mkb.shape[1], mkb.shape[2]
    tok = lambda d: pl.BlockSpec((1, ts, d), lambda b, s: (b, s, 0))
    mem = pl.BlockSpec((1, N, MW), lambda b, s: (b, 0, 0))
    return pl.pallas_call(
        _out_cross_kernel,
        grid=(B, S // ts),
        in_specs=[tok(D), tok(W), tok(W), _const_spec(w_out.shape), _const_spec((1, D)),
                  _const_spec(w_mq.shape), mem, mem, _const_spec(w_mo.shape)],
        out_specs=tok(D),
        out_shape=jax.ShapeDtypeStruct((B, S, D), F32),
        compiler_params=_cparams(("arbitrary", "arbitrary"), 48 << 20),
        name="out_cross_prompt",
    )(x, mhg, mda, w_out, gq, w_mq, mkb, mvb, w_mo)


def _out_cross_sample_kernel(x_ref, mhg_ref, mda_ref, wo_ref, gq_ref, wq_ref, mk_hbm, mv_hbm,
                             wmo_ref, x2_ref, x1_s, q_s, o_s, kbuf, vbuf, sem):
    b = pl.program_id(0)
    n_b = pl.num_programs(0)
    w = mhg_ref.shape[-1]

    def copies(bb, slot):
        out = []
        for h in range(MEM_HEADS):
            out.append(pltpu.make_async_copy(mk_hbm.at[bb, :, h, :], kbuf.at[slot, h], sem.at[slot, 0]))
            out.append(pltpu.make_async_copy(mv_hbm.at[bb, :, h, :], vbuf.at[slot, h], sem.at[slot, 1]))
        return out

    @pl.when(b == 0)
    def _():
        for cp in copies(0, 0):
            cp.start()
        x1 = x_ref[...] + _dot(mhg_ref[...], wo_ref[0:w, :]) + _dot(mda_ref[...], wo_ref[w:2 * w, :])
        x1_s[...] = x1
        hq = _rms(x1, gq_ref[...]).astype(BF16)
        dh = wq_ref.shape[1] // MEM_HEADS
        q_s[...] = _dot(hq, wq_ref[...]) * (float(dh) ** -0.5)

    slot = b % 2

    @pl.when(b + 1 < n_b)
    def _():
        for cp in copies(b + 1, 1 - slot):
            cp.start()

    for cp in copies(b, slot):
        cp.wait()
    q = q_s[pl.ds(b, 1), :].astype(BF16)
    o_s[pl.ds(b, 1), :] = _cross_attn_heads(q, lambda h: kbuf[slot, h].astype(BF16),
                                            lambda h: vbuf[slot, h].astype(BF16))

    @pl.when(b == n_b - 1)
    def _():
        x2_ref[...] = x1_s[...] + _dot(o_s[...].astype(BF16), wmo_ref[...])


def _out_cross_sample(x, mhg, mda, w_out, gq, w_mq, mem_k, mem_v, w_mo):
    T, D = x.shape
    _, N, heads, dh = mem_k.shape
    MW = heads * dh
    hbm = pl.BlockSpec(memory_space=pl.ANY)
    return pl.pallas_call(
        _out_cross_sample_kernel,
        grid=(T,),
        in_specs=[_const_spec(x.shape), _const_spec(mhg.shape), _const_spec(mda.shape),
                  _const_spec(w_out.shape), _const_spec((1, D)), _const_spec(w_mq.shape),
                  hbm, hbm, _const_spec(w_mo.shape)],
        out_specs=pl.BlockSpec((T, D), lambda b: (0, 0)),
        out_shape=jax.ShapeDtypeStruct((T, D), F32),
        scratch_shapes=[pltpu.VMEM((T, D), F32), pltpu.VMEM((T, MW), F32), pltpu.VMEM((T, MW), F32),
                        pltpu.VMEM((2, heads, N, dh), F32), pltpu.VMEM((2, heads, N, dh), F32),
                        pltpu.SemaphoreType.DMA((2, 2))],
        compiler_params=_cparams(("arbitrary",), 32 << 20),
        name="out_cross_sample",
    )(x, mhg, mda, w_out, gq, w_mq, mem_k, mem_v, w_mo)


def _ffn_kernel(x_ref, gf_ref, wg_ref, wu_ref, wd_ref, gfin_ref, y_ref):
    x = x_ref[...]
    h = _rms(x, gf_ref[...]).astype(BF16)
    g = _dot(h, wg_ref[...])
    u = _dot(h, wu_ref[...])
    a = (g * _sigmoid(g) * u).astype(BF16)
    x3 = x + _dot(a, wd_ref[...])
    y_ref[...] = _rms(x3, gfin_ref[...])


def _ffn(x, gf, wg, wu, wd, gfin, ts):
    T, D = x.shape
    tok = pl.BlockSpec((ts, D), lambda i: (i, 0))
    return pl.pallas_call(
        _ffn_kernel,
        grid=(T // ts,),
        in_specs=[tok, _const_spec((1, D)), _const_spec(wg.shape), _const_spec(wu.shape),
                  _const_spec(wd.shape), _const_spec((1, D))],
        out_specs=tok,
        out_shape=jax.ShapeDtypeStruct((T, D), F32),
        compiler_params=_cparams(("arbitrary",), VMEM_LIMIT_CAP),
        name="ffn",
    )(x, gf, wg, wu, wd, gfin)


def _sample_mix_kernel(x_ref, g_ref, w_ref, lb_ref, cos_ref, sin_ref,
                       qd_ref, k_ref, v_ref, hq_ref, kk_ref, f_ref, vh_ref, gate_ref):
    h = _rms(x_ref[...], g_ref[...]).astype(BF16)
    z = _dot(h, w_ref[...])
    lb = _lower_bound(lb_ref)
    hq, k_hg, logf, hi, gate, dq, dk, dv = _mixer_sections(z, lb)
    cos = cos_ref[...]
    sin = sin_ref[...]
    qd_ref[...] = _rope(dq, cos, sin) * (float(LANES // 2) ** -0.5)
    k_ref[...] = _rope(dk, cos, sin)
    v_ref[...] = dv
    hq_ref[...] = hq
    kk_ref[...] = k_hg
    f_ref[...] = jnp.exp(logf)
    vh_ref[...] = hi
    gate_ref[...] = gate


def _sample_mix(x, g, w_in, hg_lb, cos, sin):
    T, D = x.shape
    W = w_in.shape[1] // 7
    sds = jax.ShapeDtypeStruct
    return pl.pallas_call(
        _sample_mix_kernel,
        out_shape=[sds((T, W), F32)] * 8,
        compiler_params=_cparams(None, 32 << 20),
        name="sample_mix",
    )(x, g, w_in, hg_lb, cos, sin)


def _sample_hgrn_kernel(hq_ref, kk_ref, f_ref, vh_ref, gate_ref, onorm_ref, s0_ref,
                        mix_ref, s1_ref):
    nb = s0_ref.shape[0]
    g0 = pl.program_id(0) * nb
    W = hq_ref.shape[-1]

    def columns(ref, h):
        rows = ref[pl.ds(pl.multiple_of(g0, nb), nb), h * LANES:(h + 1) * LANES]
        pad = jnp.zeros((LANES - nb, LANES), F32)
        return jnp.concatenate([rows, pad], axis=0).T

    o_rows = []
    for h in range(HG_HEADS):
        q_t, k_t, f_t = columns(hq_ref, h), columns(kk_ref, h), columns(f_ref, h)
        v_rows = vh_ref[pl.ds(pl.multiple_of(g0, nb), nb), h * LANES:(h + 1) * LANES]
        o_h = []
        for j in range(nb):
            bcast = lambda tile: jnp.broadcast_to(tile[:, j:j + 1], (LANES, LANES))
            s_new = bcast(f_t) * s0_ref[j, h] + bcast(k_t) * v_rows[j:j + 1, :]
            s1_ref[j, h] = s_new
            o_h.append(jnp.sum(bcast(q_t) * s_new, axis=0, keepdims=True))
        o_rows.append(jnp.concatenate(o_h, axis=0))
    o = jnp.concatenate(o_rows, axis=-1)
    gate = gate_ref[pl.ds(pl.multiple_of(g0, nb), nb), :]
    mix_ref[...] = (_head_rms(o, onorm_ref[...]) * gate).astype(BF16)


def _sample_hgrn(hq, kk, f, vh, gate, onorm, s0, nb=8):
    T, W = hq.shape
    st = pl.BlockSpec((nb, HG_HEADS, LANES, LANES), lambda i: (i, 0, 0, 0))
    full = _const_spec((T, W))
    return pl.pallas_call(
        _sample_hgrn_kernel,
        grid=(T // nb,),
        in_specs=[full] * 5 + [_const_spec((1, LANES)), st],
        out_specs=[pl.BlockSpec((nb, W), lambda i: (i, 0)), st],
        out_shape=[jax.ShapeDtypeStruct((T, W), BF16), jax.ShapeDtypeStruct(s0.shape, F32)],
        compiler_params=_cparams(("arbitrary",), 32 << 20),
        name="sample_hgrn",
    )(hq, kk, f, vh, gate, onorm, s0)


def _decode_seed(q, k_new, v_new, state):
    q8_s, m_s, l_s, acc_s = state
    H = DA_HEADS

    def head_rows(x):
        return jnp.concatenate([x[:, h * LANES:(h + 1) * LANES] for h in range(H)] * 2, axis=0)

    row = lax.broadcasted_iota(jnp.int32, (2 * H, LANES), 0)
    lane = lax.broadcasted_iota(jnp.int32, (2 * H, LANES), 1)
    q8 = jnp.where((row // H) == (lane // (LANES // 2)), head_rows(q), 0.0)
    q8_s[...] = q8.astype(BF16)
    m_s[...] = jnp.sum(q8 * head_rows(k_new), axis=-1, keepdims=True)
    l_s[...] = jnp.ones(l_s.shape, F32)
    acc_s[...] = head_rows(v_new)


def _decode_pages(k_pages, v_pages, state):
    q8_s, m_s, l_s, acc_s = state
    H = DA_HEADS
    q8 = q8_s[...]
    s = jnp.concatenate([_dot_nt(q8, k.astype(BF16)) for k in k_pages], axis=-1)
    row = lax.broadcasted_iota(jnp.int32, s.shape, 0)
    col = lax.broadcasted_iota(jnp.int32, s.shape, 1)
    s = jnp.where((col % H) == (row % H), s, -jnp.inf)
    m_old = m_s[...]
    m_new = jnp.maximum(m_old, jnp.max(s, axis=-1, keepdims=True))
    alpha = jnp.exp(m_old - m_new)
    pr = jnp.exp(s - m_new)
    l_s[...] = alpha * l_s[...] + jnp.sum(pr, axis=-1, keepdims=True)
    pr = pr.astype(BF16)
    rows = k_pages[0].shape[0]
    pv = _dot(pr[:, 0:rows], v_pages[0].astype(BF16))
    for p in range(1, len(v_pages)):
        pv = pv + _dot(pr[:, p * rows:(p + 1) * rows], v_pages[p].astype(BF16))
    acc_s[...] = alpha * acc_s[...] + pv
    m_s[...] = m_new


def _decode_finish(lam_ref, onorm_ref, lam_init, state):
    _, _, l_s, acc_s = state
    H = DA_HEADS
    lam = _lambda(lam_ref, lam_init)
    o = acc_s[...] / l_s[...]
    y = _rms(o[0:H] - lam * o[H:2 * H], onorm_ref[...]) * (1.0 - lam_init)
    return jnp.concatenate([y[h:h + 1, :] for h in range(H)], axis=-1).astype(BF16)


def _paged_attn_kernel(pt_ref, q_ref, kn_ref, vn_ref, lam_ref, onorm_ref, *refs,
                       lam_init, n_pg):
    k_refs = refs[:n_pg]
    v_refs = refs[n_pg:2 * n_pg]
    o_ref = refs[2 * n_pg]
    state = refs[2 * n_pg + 1:]
    j = pl.program_id(1)

    @pl.when(j == 0)
    def _():
        _decode_seed(q_ref[0], kn_ref[0], vn_ref[0], state)

    _decode_pages([r[0] for r in k_refs], [r[0] for r in v_refs], state)

    @pl.when(j == pl.num_programs(1) - 1)
    def _():
        o_ref[0] = _decode_finish(lam_ref, onorm_ref, lam_init, state)


def _paged_attn(page_table, qd, k_new, v_new, da_lambda, onorm, cache_k, cache_v, lam_init, n_pg):
    T, n_pages = page_table.shape
    n_phys, rows, dk = cache_k.shape
    W = qd.shape[-1]
    row = pl.BlockSpec((1, 1, W), lambda b, j, pt: (b, 0, 0))

    def page_spec(p):
        return pl.BlockSpec((1, rows, dk), lambda b, j, pt: (pt[b, j * n_pg + p], 0, 0))

    const = lambda shape: pl.BlockSpec(shape, lambda b, j, pt: (0,) * len(shape))
    grid_spec = pltpu.PrefetchScalarGridSpec(
        num_scalar_prefetch=1,
        grid=(T, n_pages // n_pg),
        in_specs=[row, row, row, const(da_lambda.shape), const((1, LANES))]
        + [page_spec(p) for p in range(n_pg)] * 2,
        out_specs=row,
        scratch_shapes=[pltpu.VMEM((2 * DA_HEADS, LANES), BF16),
                        pltpu.VMEM((2 * DA_HEADS, 1), F32),
                        pltpu.VMEM((2 * DA_HEADS, 1), F32),
                        pltpu.VMEM((2 * DA_HEADS, LANES), F32)],
    )
    out = pl.pallas_call(
        functools.partial(_paged_attn_kernel, lam_init=lam_init, n_pg=n_pg),
        grid_spec=grid_spec,
        out_shape=jax.ShapeDtypeStruct((T, 1, W), BF16),
        compiler_params=_cparams(("arbitrary", "arbitrary"), 48 << 20),
        name="paged_diff_attn",
    )(page_table, qd.reshape(T, 1, W), k_new.reshape(T, 1, W), v_new.reshape(T, 1, W),
      da_lambda, onorm, *([cache_k] * n_pg), *([cache_v] * n_pg))
    return out.reshape(T, W)


def _ffn_decode_kernel(pt_ref, x_ref, gf_ref, wg_ref, wu_ref, wd_ref, gfin_ref,
                       q_ref, kn_ref, vn_ref, lam_ref, onorm_ref, ck_hbm, cv_hbm,
                       y_ref, o_ref, kbuf, vbuf, sem, q8_s, m_s, l_s, acc_s,
                       *, lam_init, n_pg, n_slot, sps, col_bounds):
    i = pl.program_id(0)
    n_steps = pl.num_programs(0)
    state = (q8_s, m_s, l_s, acc_s)
    rows = kbuf.shape[1] // n_pg
    gps = pt_ref.shape[1] // n_pg
    groups = [(ls, g) for ls in range(sps) for g in range(gps)]
    n_grp = len(groups)

    def copies(b, g, slot):
        out = []
        for p in range(n_pg):
            page = pt_ref[b, g * n_pg + p]
            dst = pl.ds(p * rows, rows)
            out.append(pltpu.make_async_copy(ck_hbm.at[page], kbuf.at[slot, dst], sem.at[slot, 0]))
            out.append(pltpu.make_async_copy(cv_hbm.at[page], vbuf.at[slot, dst], sem.at[slot, 1]))
        return out

    def start(b, g, slot):
        for cp in copies(b, g, slot):
            cp.start()

    ahead = n_slot - 1

    @pl.when(i == 0)
    def _():
        for n in range(ahead):
            start(groups[n][0], groups[n][1], n % n_slot)

    def fold_group(n):
        ls, g = groups[n]
        b = i * sps + ls
        slot = n % n_slot
        nxt = n + ahead
        if nxt < n_grp:
            start(i * sps + groups[nxt][0], groups[nxt][1], nxt % n_slot)
        else:
            @pl.when(i + 1 < n_steps)
            def _():
                ls2, g2 = groups[nxt - n_grp]
                start((i + 1) * sps + ls2, g2, nxt % n_slot)
        for cp in copies(b, g, slot):
            cp.wait()
        if g == 0:
            _decode_seed(q_ref[b], kn_ref[b], vn_ref[b], state)
        _decode_pages([kbuf[slot, p * rows:(p + 1) * rows, :] for p in range(n_pg)],
                      [vbuf[slot, p * rows:(p + 1) * rows, :] for p in range(n_pg)], state)
        if g == gps - 1:
            o_ref[b] = _decode_finish(lam_ref, onorm_ref, lam_init, state)

    x = x_ref[...]
    h = _rms(x, gf_ref[...]).astype(BF16)
    n_half = 2 * (len(col_bounds) - 1)
    folded = 0
    acc = None
    for k in range(len(col_bounds) - 1):
        lo, hi = col_bounds[k], col_bounds[k + 1]
        g = _dot(h, wg_ref[:, lo:hi])
        u = _dot(h, wu_ref[:, lo:hi])
        a = (g * _sigmoid(g) * u).astype(BF16)
        for half in (2 * k, 2 * k + 1):
            if half % 2 == 1:
                part = _dot(a, wd_ref[lo:hi, :])
                acc = part if acc is None else acc + part
            due = (half + 1) * n_grp // n_half
            for n in range(folded, due):
                fold_group(n)
            folded = due
    y_ref[...] = _rms(x + acc, gfin_ref[...])


def _ffn_decode(x, gf, wg, wu, wd, gfin, ts, page_table, qd, k_new, v_new, da_lambda, onorm,
                cache_k, cache_v, lam_init, n_pg, n_slot):
    T, D = x.shape
    F = wg.shape[1]
    Ts, n_pages = page_table.shape
    n_phys, rows, dk = cache_k.shape
    W = qd.shape[-1]
    n_tiles = T // ts
    sps = Ts // n_tiles
    mxu_n = 2 * LANES
    col_bounds = tuple(range(0, F, mxu_n)) + (F,)
    tok = pl.BlockSpec((ts, D), lambda i, pt: (i, 0))
    const = lambda shape: pl.BlockSpec(shape, lambda i, pt: (0,) * len(shape),
                                       pipeline_mode=pl.Buffered(1))
    sample = const((Ts, 1, W))
    hbm = pl.BlockSpec(memory_space=pl.ANY)
    grid_spec = pltpu.PrefetchScalarGridSpec(
        num_scalar_prefetch=1,
        grid=(n_tiles,),
        in_specs=[tok, const((1, D)), const(wg.shape), const(wu.shape), const(wd.shape),
                  const((1, D)), sample, sample, sample, const(da_lambda.shape),
                  const((1, LANES)), hbm, hbm],
        out_specs=[tok, pl.BlockSpec((Ts, 1, W), lambda i, pt: (0, 0, 0))],
        scratch_shapes=[pltpu.VMEM((n_slot, n_pg * rows, dk), F32),
                        pltpu.VMEM((n_slot, n_pg * rows, dk), F32),
                        pltpu.SemaphoreType.DMA((n_slot, 2)),
                        pltpu.VMEM((2 * DA_HEADS, LANES), BF16),
                        pltpu.VMEM((2 * DA_HEADS, 1), F32),
                        pltpu.VMEM((2 * DA_HEADS, 1), F32),
                        pltpu.VMEM((2 * DA_HEADS, LANES), F32)],
    )
    y, o = pl.pallas_call(
        functools.partial(_ffn_decode_kernel, lam_init=lam_init, n_pg=n_pg, n_slot=n_slot, sps=sps,
                          col_bounds=col_bounds),
        grid_spec=grid_spec,
        out_shape=[jax.ShapeDtypeStruct((T, D), F32), jax.ShapeDtypeStruct((Ts, 1, W), BF16)],
        compiler_params=_cparams(("arbitrary",), VMEM_LIMIT_CAP),
        name="ffn_decode",
    )(page_table, x, gf, wg, wu, wd, gfin, qd.reshape(Ts, 1, W), k_new.reshape(Ts, 1, W),
      v_new.reshape(Ts, 1, W), da_lambda, onorm, cache_k, cache_v)
    return y, o.reshape(Ts, W)


def _rope_tables(pos, dh, n_maps):
    inv = ROPE_THETA ** (-jnp.arange(0, dh, 2, dtype=F32) / dh)
    ang = pos.astype(F32)[:, None] * inv[None, :]
    cos = jnp.cos(ang)
    sin = jnp.sin(ang)
    cos = jnp.tile(jnp.concatenate([cos, cos], axis=-1), (1, n_maps))
    sin = jnp.tile(jnp.concatenate([-sin, sin], axis=-1), (1, n_maps))
    return cos, sin


def _pick_tile(n, pref):
    t = min(n, pref)
    while n % t:
        t //= 2
    return t


def kernel(x_prompt, x_sample, mem_prompt, cache_k, cache_v, cache_mem_k, cache_mem_v, state_hgrn, page_table, norm_mix, w_in, hg_lb, hg_onorm, da_lambda, da_onorm, w_out, norm_mem_q, norm_mem_kv, w_mq, w_mk, w_mv, w_mo, norm_ffn, w_gate, w_up, w_down, norm_final):
    B, S, D = x_prompt.shape
    T = x_sample.shape[0]
    depth = w_in.shape[0]
    assert depth == 1 and x_sample.shape[1] == 1
    l = 0
    lam_init = 0.8 - 0.6 * math.exp(-0.3 * l)
    n_phys, page = cache_k.shape[1], cache_k.shape[2]
    past_len = page_table.shape[1] * page
    W = w_in.shape[2] // 7

    bf = lambda w: w.astype(BF16)
    w_in_b, w_out_b = bf(w_in[l]), bf(w_out[l])
    w_mq_b, w_mk_b, w_mv_b, w_mo_b = bf(w_mq[l]), bf(w_mk[l]), bf(w_mv[l]), bf(w_mo[l])
    w_gate_b, w_up_b, w_down_b = bf(w_gate[l]), bf(w_up[l]), bf(w_down[l])
    row = lambda g: g.reshape(1, -1)
    lb2 = hg_lb[l:l + 2]
    lam_p = da_lambda[l]

    cos_p, sin_p = _rope_tables(jnp.arange(S), LANES // 2, 2 * DA_HEADS)
    cos_s, sin_s = _rope_tables(past_len + jnp.arange(1), LANES // 2, 2 * DA_HEADS)
    cos_s = jnp.broadcast_to(cos_s, (T, W))
    sin_s = jnp.broadcast_to(sin_s, (T, W))

    mk, mv, mkb, mvb = _mem_kv(mem_prompt, row(norm_mem_kv[l]), w_mk_b, w_mv_b)
    qd, k_p, v_p, kb, v_att, mix_hg, hs_p = _prompt_mix(
        x_prompt, row(norm_mix[l]), w_in_b, lb2, cos_p, sin_p, row(hg_onorm[l]),
        ts=_pick_tile(S, 256))
    mix_da = _diff_attn_prompt(qd, kb, v_att, lam_p, row(da_onorm[l]), lam_init,
                               tq=_pick_tile(S, 1024), tk=_pick_tile(S, 256))
    x2 = _out_cross_prompt(x_prompt, mix_hg, mix_da, w_out_b, row(norm_mem_q[l]), w_mq_b,
                           mkb, mvb, w_mo_b, ts=_pick_tile(S, 512))

    xs = x_sample.reshape(T, D)
    qd_s, k_s, v_s, hq_s, kk_s, f_s, vh_s, gate_s = _sample_mix(
        xs, row(norm_mix[l]), w_in_b, lb2, cos_s, sin_s)
    mix_hg_s, hs_s = _sample_hgrn(hq_s, kk_s, f_s, vh_s, gate_s, row(hg_onorm[l]), state_hgrn[l])
    ck = cache_k.reshape(depth * n_phys, page * DA_HEADS, LANES)
    cv = cache_v.reshape(depth * n_phys, page * DA_HEADS, LANES)
    n_pages = page_table.shape[1]
    ffn_ts = _pick_tile(B * S, 512)
    n_tiles = (B * S) // ffn_ts
    n_pg = _pick_tile(n_pages, 8)
    n_slot = 4
    ffn_args = (row(norm_ffn[l]), w_gate_b, w_up_b, w_down_b, row(norm_final))
    if T % n_tiles == 0 and ((T // n_tiles) * (n_pages // n_pg)) % n_slot == 0:
        y_p, mix_da_s = _ffn_decode(x2.reshape(B * S, D), *ffn_args, ffn_ts, page_table, qd_s, k_s,
                                    v_s, lam_p, row(da_onorm[l]), ck, cv, lam_init, n_pg, n_slot)
    else:
        y_p = _ffn(x2.reshape(B * S, D), *ffn_args, ts=ffn_ts)
        mix_da_s = _paged_attn(page_table, qd_s, k_s, v_s, lam_p, row(da_onorm[l]), ck, cv,
                               lam_init, n_pg=_pick_tile(n_pages, 16))
    y_p = y_p.reshape(B, S, D)
    MW = w_mk.shape[2]
    x2_s = _out_cross_sample(xs, mix_hg_s, mix_da_s, w_out_b, row(norm_mem_q[l]), w_mq_b,
                             cache_mem_k[l], cache_mem_v[l],
                             w_mo_b)
    y_s = _ffn(x2_s, row(norm_ffn[l]), w_gate_b, w_up_b, w_down_b, row(norm_final), ts=T)

    dk = LANES
    return (y_p, y_s.reshape(T, 1, D),
            hs_p[None],
            k_p.reshape(1, B, S, DA_HEADS, dk), v_p.reshape(1, B, S, DA_HEADS, dk),
            mk[None], mv[None],
            hs_s[None],
            k_s.reshape(1, T, 1, DA_HEADS, dk), v_s.reshape(1, T, 1, DA_HEADS, dk))
```

```python
import functools
import math

import jax
import jax.numpy as jnp
from jax import lax
from jax.experimental import pallas as pl
from jax.experimental.pallas import tpu as pltpu

F32 = jnp.float32
BF16 = jnp.bfloat16
EPS = 1e-6
ROPE_THETA = 10000.0

HG_HEADS = 4
DA_HEADS = 4
MEM_HEADS = 4
HG_CHUNK = 64
LANES = 128
BF16_ROWS = 16
VMEM_LIMIT_CAP = 56 << 20

_NT = (((1,), (1,)), ((), ()))
_TN = (((0,), (0,)), ((), ()))


def _dot(a, b):
    return jnp.dot(a, b, preferred_element_type=F32)


def _dot_nt(a, b):
    return lax.dot_general(a, b, _NT, preferred_element_type=F32)


def _dot_tn(a, b):
    return lax.dot_general(a, b, _TN, preferred_element_type=F32)


def _rms(x, g):
    ms = jnp.mean(x * x, axis=-1, keepdims=True)
    return x * lax.rsqrt(ms + EPS) * g


def _sigmoid(x):
    return 1.0 / (1.0 + jnp.exp(-x))


def _cparams(semantics, vmem_bytes):
    return pltpu.CompilerParams(
        dimension_semantics=semantics,
        vmem_limit_bytes=int(min(max(vmem_bytes, 16 << 20), VMEM_LIMIT_CAP)))


def _const_spec(shape):
    nd = len(shape)
    return pl.BlockSpec(shape, lambda *_: (0,) * nd, pipeline_mode=pl.Buffered(1))


def _lower_bound(lb_ref):
    a0 = lb_ref[0:1, :]
    a1 = lb_ref[1:2, :]
    m = jnp.maximum(a0, a1)
    e0 = jnp.exp(a0 - m)
    e1 = jnp.exp(a1 - m)
    return e0 / (e0 + e1)


def _rope(x, cos, sin_signed):
    n = x.shape[-1]
    lane = lax.broadcasted_iota(jnp.int32, x.shape, x.ndim - 1)
    swapped = jnp.where((lane & 63) < 32,
                        pltpu.roll(x, n - 32, x.ndim - 1),
                        pltpu.roll(x, 32, x.ndim - 1))
    return x * cos + swapped * sin_signed


def _lambda(lam_ref, lam_init):
    lp = lam_ref[...]
    s01 = jnp.sum(lp[0:1, :] * lp[1:2, :], axis=-1, keepdims=True)
    s23 = jnp.sum(lp[2:3, :] * lp[3:4, :], axis=-1, keepdims=True)
    return jnp.exp(s01) - jnp.exp(s23) + lam_init


def _mixer_sections(z, lb):
    w = z.shape[-1] // 7
    hq, zf, hi, hg, dq, dk, dv = (z[:, i * w:(i + 1) * w] for i in range(7))
    sig = _sigmoid(zf)
    logf = jnp.log(lb + (1.0 - lb) * sig)
    k_hg = (1.0 - lb) * (1.0 - sig)
    gate = hg * _sigmoid(hg)
    return hq, k_hg, logf, hi, gate, dq, dk, dv


def _head_rms(o, g):
    outs = []
    for h in range(o.shape[-1] // LANES):
        oh = o[:, h * LANES:(h + 1) * LANES]
        outs.append(_rms(oh, g))
    return jnp.concatenate(outs, axis=-1)


def _memkv_kernel(mem_ref, g_ref, wk_ref, wv_ref, k_ref, v_ref, kb_ref, vb_ref):
    m = _rms(mem_ref[0], g_ref[...]).astype(BF16)
    k = _dot(m, wk_ref[...])
    v = _dot(m, wv_ref[...])
    dh = k_ref.shape[-1]
    for h in range(MEM_HEADS):
        k_ref[0, :, h, :] = k[:, h * dh:(h + 1) * dh]
        v_ref[0, :, h, :] = v[:, h * dh:(h + 1) * dh]
    kb_ref[0] = k.astype(BF16)
    vb_ref[0] = v.astype(BF16)


def _mem_kv(mem, g, wk, wv):
    B, N, D = mem.shape
    W = wk.shape[1]
    dh = W // MEM_HEADS
    blk = lambda d: pl.BlockSpec((1, N, d), lambda b: (b, 0, 0))
    blk4 = pl.BlockSpec((1, N, MEM_HEADS, dh), lambda b: (b, 0, 0, 0))
    return pl.pallas_call(
        _memkv_kernel,
        grid=(B,),
        in_specs=[blk(D), _const_spec((1, D)), _const_spec((D, W)), _const_spec((D, W))],
        out_specs=[blk4, blk4, blk(W), blk(W)],
        out_shape=[jax.ShapeDtypeStruct((B, N, MEM_HEADS, dh), F32)] * 2
        + [jax.ShapeDtypeStruct((B, N, W), BF16)] * 2,
        compiler_params=_cparams(("arbitrary",), 40 << 20),
        name="mem_kv",
    )(mem, g, wk, wv)


def _hgrn_levels(q, kk, lf, v):
    C, W = q.shape
    t = lax.broadcasted_iota(jnp.int32, (C, W), 0)
    n_lvl = int(math.log2(C))

    c = lf
    lvls = []
    for l in range(n_lvl):
        m = 1 << l
        upper = (t & m) != 0
        if m < 8:
            y = c
            for i in range(l):
                y = jnp.where((t & (1 << i)) == 0, pltpu.roll(y, C - (1 << i), 0), y)
            bc = jnp.where(upper, pltpu.roll(y, m, 0), y)
        else:
            pieces = []
            for j in range(C // (2 * m)):
                r = 2 * m * j + m - 1
                pieces.append(jnp.broadcast_to(c[r:r + 1, :], (2 * m, W)))
            bc = pieces[0] if len(pieces) == 1 else jnp.concatenate(pieces, axis=0)
        e = jnp.exp(jnp.where(upper, c, bc - c))
        lvls.append((jnp.where(upper, q, kk) * e).astype(BF16))
        c = c + jnp.where(upper, bc, 0.0)
    b = c
    b_last = b[C - 1:C, :]
    return dict(lvls=lvls, q=q.astype(BF16), k=kk.astype(BF16), v=v.astype(BF16),
                q_state=(q * jnp.exp(b)).astype(BF16),
                k_state=(kk * jnp.exp(b_last - b)).astype(BF16),
                decay=jnp.exp(b_last))


def _hgrn_products(lv, st_ref):
    out = []
    for h in range(lv["q"].shape[1] // LANES):
        sl = slice(h * LANES, (h + 1) * LANES)
        pairs = [_dot_nt(lv["q"][:, sl], lv["k"][:, sl])]
        pairs += [_dot_nt(r[:, sl], r[:, sl]) for r in lv["lvls"]]
        o_state = _dot_nt(lv["q_state"][:, sl], st_ref[h].astype(BF16))
        st_term = _dot_tn(lv["v"][:, sl], lv["k_state"][:, sl])
        out.append((pairs, o_state, st_term))
    return out


def _hgrn_finish(lv, prods, st_ref):
    C = lv["q"].shape[0]
    ti = lax.broadcasted_iota(jnp.int32, (C, C), 0)
    si = lax.broadcasted_iota(jnp.int32, (C, C), 1)
    diff_bits = jnp.where(ti > si, ti ^ si, 0)
    outs = []
    for h, (pairs, o_state, st_term) in enumerate(prods):
        sl = slice(h * LANES, (h + 1) * LANES)
        a = jnp.where(ti == si, pairs[0], 0.0)
        for l, pr in enumerate(pairs[1:]):
            a = a + jnp.where((diff_bits >> l) == 1, pr, 0.0)
        outs.append(_dot(a.astype(BF16), lv["v"][:, sl]) + o_state)
        st_ref[h] = st_ref[h] * lv["decay"][:, sl] + st_term
    return jnp.concatenate(outs, axis=-1)


def _prompt_mix_kernel(x_ref, g_ref, w_ref, lb_ref, cos_ref, sin_ref, onorm_ref,
                       qd_ref, k_ref, v_ref, kb_ref, va_ref, mix_ref, state_ref,
                       o_s, st_s):
    s = pl.program_id(1)
    last = pl.num_programs(1) - 1

    @pl.when(s == 0)
    def _():
        st_s[...] = jnp.zeros_like(st_s)

    h = _rms(x_ref[0], g_ref[...]).astype(BF16)

    ts = x_ref.shape[1]
    sec_w = w_ref.shape[1] // 7
    n_chunk = ts // HG_CHUNK
    stage = {"done": 0, "lv": None}

    def project(sec):
        return _dot(h, w_ref[:, sec * sec_w:(sec + 1) * sec_w])

    def levels(ci):
        rows = slice(ci * HG_CHUNK, (ci + 1) * HG_CHUNK)
        return _hgrn_levels(hq[rows, :], k_hg[rows, :], logf[rows, :], hi[rows, :])

    def recur(upto):
        for ci in range(stage["done"], min(upto, n_chunk)):
            lv = stage["lv"] if stage["lv"] is not None else levels(ci)
            prods = _hgrn_products(lv, st_s)
            stage["lv"] = levels(ci + 1) if ci + 1 < n_chunk else None
            o_s[ci * HG_CHUNK:(ci + 1) * HG_CHUNK, :] = _hgrn_finish(lv, prods, st_s)
        stage["done"] = max(stage["done"], min(upto, n_chunk))

    def head_rows_out(ref, val):
        for hh in range(DA_HEADS):
            ref[0, pl.ds(hh, ts, stride=DA_HEADS), :] = val[:, hh * LANES:(hh + 1) * LANES]

    lb = _lower_bound(lb_ref)
    cos = cos_ref[...]
    sin = sin_ref[...]
    hq = project(0)
    sig = _sigmoid(project(1))
    logf = jnp.log(lb + (1.0 - lb) * sig)
    k_hg = (1.0 - lb) * (1.0 - sig)
    hi = project(2)
    hg = project(3)
    recur(n_chunk // 4)
    q_da = _rope(project(4), cos, sin)
    qd_ref[0] = (q_da * (math.log2(math.e) * float(LANES // 2) ** -0.5)).astype(BF16)
    recur(n_chunk // 2)
    k_da = _rope(project(5), cos, sin)
    head_rows_out(k_ref, k_da)
    kb_ref[0] = k_da.astype(BF16)
    recur(3 * n_chunk // 4)
    dv = project(6)
    head_rows_out(v_ref, dv)
    va_ref[0] = dv
    recur(n_chunk)
    gate = hg * _sigmoid(hg)
    mix_ref[0] = (_head_rms(o_s[...], onorm_ref[...]) * gate).astype(BF16)

    @pl.when(s == last)
    def _():
        for hh in range(HG_HEADS):
            state_ref[0, hh] = st_s[hh].T


def _prompt_mix(x, g, w_in, hg_lb, cos, sin, onorm, ts):
    B, S, D = x.shape
    W = w_in.shape[1] // 7
    tok = lambda d: pl.BlockSpec((1, ts, d), lambda b, s: (b, s, 0))
    tab = pl.BlockSpec((ts, W), lambda b, s: (s, 0))
    rows = pl.BlockSpec((1, ts * DA_HEADS, LANES), lambda b, s: (b, s, 0))
    state = pl.BlockSpec((1, HG_HEADS, LANES, LANES), lambda b, s: (b, 0, 0, 0))
    sds = jax.ShapeDtypeStruct
    return pl.pallas_call(
        _prompt_mix_kernel,
        grid=(B, S // ts),
        in_specs=[tok(D), _const_spec((1, D)), _const_spec(w_in.shape),
                  _const_spec(hg_lb.shape), tab, tab, _const_spec((1, LANES))],
        out_specs=[tok(W), rows, rows, tok(W), tok(W), tok(W), state],
        out_shape=[sds((B, S, W), BF16), sds((B, S * DA_HEADS, LANES), F32),
                   sds((B, S * DA_HEADS, LANES), F32), sds((B, S, W), BF16),
                   sds((B, S, W), F32), sds((B, S, W), BF16),
                   sds((B, HG_HEADS, LANES, LANES), F32)],
        scratch_shapes=[pltpu.VMEM((ts, W), F32), pltpu.VMEM((HG_HEADS, LANES, LANES), F32)],
        compiler_params=_cparams(("arbitrary", "arbitrary"), 48 << 20),
        name="prompt_mix",
    )(x, g, w_in, hg_lb, cos, sin, onorm)


def _diff_attn_kernel(q_ref, k_ref, v_ref, lam_ref, onorm_ref, o_ref,
                      qq_s, vt_s, m_s, acc_s, *, lam_init, tk):
    i = pl.program_id(2)
    tq = q_ref.shape[1]
    n_ck = tq // tk
    n_kv = v_ref.shape[1] // tk

    @pl.when(i == 0)
    def _():
        ones = jnp.ones((vt_s.shape[1] - LANES, tk), BF16)
        for jj in range(n_kv):
            vt_s[jj, 0:LANES, :] = v_ref[0, jj * tk:(jj + 1) * tk, :].T.astype(BF16)
            vt_s[jj, LANES:, :] = ones

    q = q_ref[0]
    lane = lax.broadcasted_iota(jnp.int32, q.shape, 1)
    zero = jnp.zeros_like(q)
    qq_s[0:tq, :] = jnp.where(lane < LANES // 2, q, zero)
    qq_s[tq:2 * tq, :] = jnp.where(lane >= LANES // 2, q, zero)
    m_s[...] = jnp.full(m_s.shape, -jnp.inf, F32)
    acc_s[...] = jnp.zeros(acc_s.shape, F32)

    def scores(j, c):
        k = k_ref[0, pl.ds(pl.multiple_of(j * tk, tk), tk), :]
        return _dot_nt(k, qq_s[c * tk:(c + 1) * tk, :])

    def softmax(j, c, tri, s):
        cols = slice(c * tk, (c + 1) * tk)
        if tri:
            key = lax.broadcasted_iota(jnp.int32, s.shape, 0)
            qry = lax.broadcasted_iota(jnp.int32, s.shape, 1)
            s = jnp.where(key <= qry, s, -jnp.inf)
        m_old = m_s[:, cols]
        m_new = jnp.maximum(m_old, jnp.max(s, axis=0, keepdims=True))
        alpha = jnp.exp2(m_old - m_new)
        p = jnp.exp2(s - m_new)
        m_s[:, cols] = m_new
        return cols, alpha, _dot(vt_s[j], p.astype(BF16))

    def accumulate(cols, alpha, pv):
        acc_s[:, cols] = alpha * acc_s[:, cols] + pv

    ahead = 4

    def run(work):
        s = {n: scores(*work[n][:2]) for n in range(min(ahead, len(work)))}
        pending = None
        for n, (j, c, tri) in enumerate(work):
            if n + ahead < len(work):
                s[n + ahead] = scores(*work[n + ahead][:2])
            done = softmax(j, c, tri, s.pop(n))
            if pending is not None:
                accumulate(*pending)
            pending = done
        accumulate(*pending)

    per_iter = 2 if n_ck % 2 == 0 else 1

    def full_blocks(jj, carry):
        run([(jj * per_iter + r, c, False) for r in range(per_iter) for c in range(2 * n_ck)])
        return carry

    lax.fori_loop(0, (i * n_ck) // per_iter, full_blocks, 0)
    run([(i * n_ck + d, mp * n_ck + cq, cq == d)
         for d in range(n_ck) for mp in range(2) for cq in range(d, n_ck)])

    lam = _lambda(lam_ref, lam_init)
    a = acc_s[0:LANES, :] * (1.0 / acc_s[LANES:LANES + 1, :])
    o_t = a[:, 0:tq] - lam * a[:, tq:2 * tq]
    ms = jnp.mean(o_t * o_t, axis=0, keepdims=True)
    y = (o_t * lax.rsqrt(ms + EPS)).T
    o_ref[0] = (y * onorm_ref[...] * (1.0 - lam_init)).astype(BF16)


def _diff_attn_prompt(qd, kb, v, da_lambda, onorm, lam_init, tq, tk):
    B, S, W = qd.shape
    H = W // LANES
    qspec = pl.BlockSpec((1, tq, LANES), lambda b, h, i: (b, i, h))
    kvspec = pl.BlockSpec((1, S, LANES), lambda b, h, i: (b, 0, h))
    return pl.pallas_call(
        functools.partial(_diff_attn_kernel, lam_init=lam_init, tk=tk),
        grid=(B, H, S // tq),
        in_specs=[qspec, kvspec, kvspec, _const_spec(da_lambda.shape), _const_spec((1, LANES))],
        out_specs=qspec,
        out_shape=jax.ShapeDtypeStruct((B, S, W), BF16),
        scratch_shapes=[pltpu.VMEM((2 * tq, LANES), BF16),
                        pltpu.VMEM((S // tk, LANES + BF16_ROWS, tk), BF16),
                        pltpu.VMEM((1, 2 * tq), F32),
                        pltpu.VMEM((LANES + BF16_ROWS, 2 * tq), F32)],
        compiler_params=_cparams(("arbitrary",) * 3, 32 << 20),
        name="diff_attn_prompt",
    )(qd, kb, v, da_lambda, onorm)


def _cross_attn_heads(q, mk_head, mv_head):
    dh = q.shape[-1] // MEM_HEADS
    scores = [_dot_nt(q[:, h * dh:(h + 1) * dh], mk_head(h)) for h in range(MEM_HEADS)]
    probs = []
    for s in scores:
        m = jnp.max(s, axis=-1, keepdims=True)
        p = jnp.exp(s - m)
        l = jnp.sum(p, axis=-1, keepdims=True)
        probs.append((p / l).astype(BF16))
    return jnp.concatenate([_dot(probs[h], mv_head(h)) for h in range(MEM_HEADS)], axis=-1)


def _out_cross_kernel(x_ref, mhg_ref, mda_ref, wo_ref, gq_ref, wq_ref, mk_ref, mv_ref,
                      wmo_ref, x2_ref):
    w = mhg_ref.shape[-1]
    dh = wq_ref.shape[1] // MEM_HEADS
    ts = x_ref.shape[1]
    halves = [slice(0, ts // 2), slice(ts // 2, ts)] if ts % 16 == 0 else [slice(0, ts)]
    x1 = [x_ref[0, r, :] + _dot(mhg_ref[0, r, :], wo_ref[0:w, :])
          + _dot(mda_ref[0, r, :], wo_ref[w:2 * w, :]) for r in halves]
    q = [(_dot(_rms(v, gq_ref[...]).astype(BF16), wq_ref[...]) * (float(dh) ** -0.5)).astype(BF16)
         for v in x1]
    mk = lambda h: mk_ref[0, :, h * dh:(h + 1) * dh]
    mv = lambda h: mv_ref[0, :, h * dh:(h + 1) * dh]
    o = [_cross_attn_heads(v, mk, mv) for v in q]
    for r, v1, vo in zip(halves, x1, o):
        x2_ref[0, r, :] = v1 + _dot(vo.astype(BF16), wmo_ref[...])


def _out_cross_prompt(x, mhg, mda, w_out, gq, w_mq, mkb, mvb, w_mo, ts):
    B, S, D = x.shape
    W = mhg.shape[-1]
    N, MW = mkb.shape[1], mkb.shape[2]
    tok = lambda d: pl.BlockSpec((1, ts, d), lambda b, s: (b, s, 0))
    mem = pl.BlockSpec((1, N, MW), lambda b, s: (b, 0, 0))
    return pl.pallas_call(
        _out_cross_kernel,
        grid=(B, S // ts),
        in_specs=[tok(D), tok(W), tok(W), _const_spec(w_out.shape), _const_spec((1, D)),
                  _const_spec(w_mq.shape), mem, mem, _const_spec(w_mo.shape)],
        out_specs=tok(D),
        out_shape=jax.ShapeDtypeStruct((B, S, D), F32),
        compiler_params=_cparams(("arbitrary", "arbitrary"), 48 << 20),
        name="out_cross_prompt",
    )(x, mhg, mda, w_out, gq, w_mq, mkb, mvb, w_mo)


def _out_cross_sample_kernel(x_ref, mhg_ref, mda_ref, wo_ref, gq_ref, wq_ref, mk_hbm, mv_hbm,
                             wmo_ref, x2_ref, x1_s, q_s, o_s, kbuf, vbuf, sem):
    b = pl.program_id(0)
    n_b = pl.num_programs(0)
    w = mhg_ref.shape[-1]

    def copies(bb, slot):
        out = []
        for h in range(MEM_HEADS):
            out.append(pltpu.make_async_copy(mk_hbm.at[bb, :, h, :], kbuf.at[slot, h], sem.at[slot, 0]))
            out.append(pltpu.make_async_copy(mv_hbm.at[bb, :, h, :], vbuf.at[slot, h], sem.at[slot, 1]))
        return out

    @pl.when(b == 0)
    def _():
        for cp in copies(0, 0):
            cp.start()
        x1 = x_ref[...] + _dot(mhg_ref[...], wo_ref[0:w, :]) + _dot(mda_ref[...], wo_ref[w:2 * w, :])
        x1_s[...] = x1
        hq = _rms(x1, gq_ref[...]).astype(BF16)
        dh = wq_ref.shape[1] // MEM_HEADS
        q_s[...] = _dot(hq, wq_ref[...]) * (float(dh) ** -0.5)

    slot = b % 2

    @pl.when(b + 1 < n_b)
    def _():
        for cp in copies(b + 1, 1 - slot):
            cp.start()

    for cp in copies(b, slot):
        cp.wait()
    q = q_s[pl.ds(b, 1), :].astype(BF16)
    o_s[pl.ds(b, 1), :] = _cross_attn_heads(q, lambda h: kbuf[slot, h].astype(BF16),
                                            lambda h: vbuf[slot, h].astype(BF16))

    @pl.when(b == n_b - 1)
    def _():
        x2_ref[...] = x1_s[...] + _dot(o_s[...].astype(BF16), wmo_ref[...])


def _out_cross_sample(x, mhg, mda, w_out, gq, w_mq, mem_k, mem_v, w_mo):
    T, D = x.shape
    _, N, heads, dh = mem_k.shape
    MW = heads * dh
    hbm = pl.BlockSpec(memory_space=pl.ANY)
    return pl.pallas_call(
        _out_cross_sample_kernel,
        grid=(T,),
        in_specs=[_const_spec(x.shape), _const_spec(mhg.shape), _const_spec(mda.shape),
                  _const_spec(w_out.shape), _const_spec((1, D)), _const_spec(w_mq.shape),
                  hbm, hbm, _const_spec(w_mo.shape)],
        out_specs=pl.BlockSpec((T, D), lambda b: (0, 0)),
        out_shape=jax.ShapeDtypeStruct((T, D), F32),
        scratch_shapes=[pltpu.VMEM((T, D), F32), pltpu.VMEM((T, MW), F32), pltpu.VMEM((T, MW), F32),
                        pltpu.VMEM((2, heads, N, dh), F32), pltpu.VMEM((2, heads, N, dh), F32),
                        pltpu.SemaphoreType.DMA((2, 2))],
        compiler_params=_cparams(("arbitrary",), 32 << 20),
        name="out_cross_sample",
    )(x, mhg, mda, w_out, gq, w_mq, mem_k, mem_v, w_mo)


def _ffn_kernel(x_ref, gf_ref, wg_ref, wu_ref, wd_ref, gfin_ref, y_ref):
    x = x_ref[...]
    h = _rms(x, gf_ref[...]).astype(BF16)
    g = _dot(h, wg_ref[...])
    u = _dot(h, wu_ref[...])
    a = (g * _sigmoid(g) * u).astype(BF16)
    x3 = x + _dot(a, wd_ref[...])
    y_ref[...] = _rms(x3, gfin_ref[...])


def _ffn(x, gf, wg, wu, wd, gfin, ts):
    T, D = x.shape
    tok = pl.BlockSpec((ts, D), lambda i: (i, 0))
    return pl.pallas_call(
        _ffn_kernel,
        grid=(T // ts,),
        in_specs=[tok, _const_spec((1, D)), _const_spec(wg.shape), _const_spec(wu.shape),
                  _const_spec(wd.shape), _const_spec((1, D))],
        out_specs=tok,
        out_shape=jax.ShapeDtypeStruct((T, D), F32),
        compiler_params=_cparams(("arbitrary",), VMEM_LIMIT_CAP),
        name="ffn",
    )(x, gf, wg, wu, wd, gfin)


def _sample_mix_kernel(x_ref, g_ref, w_ref, lb_ref, cos_ref, sin_ref,
                       qd_ref, k_ref, v_ref, hq_ref, kk_ref, f_ref, vh_ref, gate_ref):
    h = _rms(x_ref[...], g_ref[...]).astype(BF16)
    z = _dot(h, w_ref[...])
    lb = _lower_bound(lb_ref)
    hq, k_hg, logf, hi, gate, dq, dk, dv = _mixer_sections(z, lb)
    cos = cos_ref[...]
    sin = sin_ref[...]
    qd_ref[...] = _rope(dq, cos, sin) * (float(LANES // 2) ** -0.5)
    k_ref[...] = _rope(dk, cos, sin)
    v_ref[...] = dv
    hq_ref[...] = hq
    kk_ref[...] = k_hg
    f_ref[...] = jnp.exp(logf)
    vh_ref[...] = hi
    gate_ref[...] = gate


def _sample_mix(x, g, w_in, hg_lb, cos, sin):
    T, D = x.shape
    W = w_in.shape[1] // 7
    sds = jax.ShapeDtypeStruct
    return pl.pallas_call(
        _sample_mix_kernel,
        out_shape=[sds((T, W), F32)] * 8,
        compiler_params=_cparams(None, 32 << 20),
        name="sample_mix",
    )(x, g, w_in, hg_lb, cos, sin)


def _sample_hgrn_kernel(hq_ref, kk_ref, f_ref, vh_ref, gate_ref, onorm_ref, s0_ref,
                        mix_ref, s1_ref):
    nb = s0_ref.shape[0]
    g0 = pl.program_id(0) * nb
    W = hq_ref.shape[-1]

    def columns(ref, h):
        rows = ref[pl.ds(pl.multiple_of(g0, nb), nb), h * LANES:(h + 1) * LANES]
        pad = jnp.zeros((LANES - nb, LANES), F32)
        return jnp.concatenate([rows, pad], axis=0).T

    o_rows = []
    for h in range(HG_HEADS):
        q_t, k_t, f_t = columns(hq_ref, h), columns(kk_ref, h), columns(f_ref, h)
        v_rows = vh_ref[pl.ds(pl.multiple_of(g0, nb), nb), h * LANES:(h + 1) * LANES]
        o_h = []
        for j in range(nb):
            bcast = lambda tile: jnp.broadcast_to(tile[:, j:j + 1], (LANES, LANES))
            s_new = bcast(f_t) * s0_ref[j, h] + bcast(k_t) * v_rows[j:j + 1, :]
            s1_ref[j, h] = s_new
            o_h.append(jnp.sum(bcast(q_t) * s_new, axis=0, keepdims=True))
        o_rows.append(jnp.concatenate(o_h, axis=0))
    o = jnp.concatenate(o_rows, axis=-1)
    gate = gate_ref[pl.ds(pl.multiple_of(g0, nb), nb), :]
    mix_ref[...] = (_head_rms(o, onorm_ref[...]) * gate).astype(BF16)


def _sample_hgrn(hq, kk, f, vh, gate, onorm, s0, nb=8):
    T, W = hq.shape
    st = pl.BlockSpec((nb, HG_HEADS, LANES, LANES), lambda i: (i, 0, 0, 0))
    full = _const_spec((T, W))
    return pl.pallas_call(
        _sample_hgrn_kernel,
        grid=(T // nb,),
        in_specs=[full] * 5 + [_const_spec((1, LANES)), st],
        out_specs=[pl.BlockSpec((nb, W), lambda i: (i, 0)), st],
        out_shape=[jax.ShapeDtypeStruct((T, W), BF16), jax.ShapeDtypeStruct(s0.shape, F32)],
        compiler_params=_cparams(("arbitrary",), 32 << 20),
        name="sample_hgrn",
    )(hq, kk, f, vh, gate, onorm, s0)


def _decode_seed(q, k_new, v_new, state):
    q8_s, m_s, l_s, acc_s = state
    H = DA_HEADS

    def head_rows(x):
        return jnp.concatenate([x[:, h * LANES:(h + 1) * LANES] for h in range(H)] * 2, axis=0)

    row = lax.broadcasted_iota(jnp.int32, (2 * H, LANES), 0)
    lane = lax.broadcasted_iota(jnp.int32, (2 * H, LANES), 1)
    q8 = jnp.where((row // H) == (lane // (LANES // 2)), head_rows(q), 0.0)
    q8_s[...] = q8.astype(BF16)
    m_s[...] = jnp.sum(q8 * head_rows(k_new), axis=-1, keepdims=True)
    l_s[...] = jnp.ones(l_s.shape, F32)
    acc_s[...] = head_rows(v_new)


def _decode_scores(k_pages, state):
    H = DA_HEADS
    q8 = state[0][...]
    s = jnp.concatenate([_dot_nt(q8, k.astype(BF16)) for k in k_pages], axis=-1)
    row = lax.broadcasted_iota(jnp.int32, s.shape, 0)
    col = lax.broadcasted_iota(jnp.int32, s.shape, 1)
    return jnp.where((col % H) == (row % H), s, -jnp.inf)


def _decode_update(s, v_pages, state):
    _, m_s, l_s, acc_s = state
    m_old = m_s[...]
    m_new = jnp.maximum(m_old, jnp.max(s, axis=-1, keepdims=True))
    alpha = jnp.exp(m_old - m_new)
    pr = jnp.exp(s - m_new)
    l_s[...] = alpha * l_s[...] + jnp.sum(pr, axis=-1, keepdims=True)
    pr = pr.astype(BF16)
    rows = v_pages[0].shape[0]
    pv = _dot(pr[:, 0:rows], v_pages[0].astype(BF16))
    for p in range(1, len(v_pages)):
        pv = pv + _dot(pr[:, p * rows:(p + 1) * rows], v_pages[p].astype(BF16))
    acc_s[...] = alpha * acc_s[...] + pv
    m_s[...] = m_new


def _decode_finish(lam_ref, onorm_ref, lam_init, state):
    _, _, l_s, acc_s = state
    H = DA_HEADS
    lam = _lambda(lam_ref, lam_init)
    o = acc_s[...] / l_s[...]
    y = _rms(o[0:H] - lam * o[H:2 * H], onorm_ref[...]) * (1.0 - lam_init)
    return jnp.concatenate([y[h:h + 1, :] for h in range(H)], axis=-1).astype(BF16)


def _paged_attn_kernel(pt_ref, q_ref, kn_ref, vn_ref, lam_ref, onorm_ref, *refs,
                       lam_init, n_pg):
    k_refs = refs[:n_pg]
    v_refs = refs[n_pg:2 * n_pg]
    o_ref = refs[2 * n_pg]
    state = refs[2 * n_pg + 1:]
    j = pl.program_id(1)

    @pl.when(j == 0)
    def _():
        _decode_seed(q_ref[0], kn_ref[0], vn_ref[0], state)

    _decode_update(_decode_scores([r[0] for r in k_refs], state), [r[0] for r in v_refs], state)

    @pl.when(j == pl.num_programs(1) - 1)
    def _():
        o_ref[0] = _decode_finish(lam_ref, onorm_ref, lam_init, state)


def _paged_attn(page_table, qd, k_new, v_new, da_lambda, onorm, cache_k, cache_v, lam_init, n_pg):
    T, n_pages = page_table.shape
    n_phys, rows, dk = cache_k.shape
    W = qd.shape[-1]
    row = pl.BlockSpec((1, 1, W), lambda b, j, pt: (b, 0, 0))

    def page_spec(p):
        return pl.BlockSpec((1, rows, dk), lambda b, j, pt: (pt[b, j * n_pg + p], 0, 0))

    const = lambda shape: pl.BlockSpec(shape, lambda b, j, pt: (0,) * len(shape))
    grid_spec = pltpu.PrefetchScalarGridSpec(
        num_scalar_prefetch=1,
        grid=(T, n_pages // n_pg),
        in_specs=[row, row, row, const(da_lambda.shape), const((1, LANES))]
        + [page_spec(p) for p in range(n_pg)] * 2,
        out_specs=row,
        scratch_shapes=[pltpu.VMEM((2 * DA_HEADS, LANES), BF16),
                        pltpu.VMEM((2 * DA_HEADS, 1), F32),
                        pltpu.VMEM((2 * DA_HEADS, 1), F32),
                        pltpu.VMEM((2 * DA_HEADS, LANES), F32)],
    )
    out = pl.pallas_call(
        functools.partial(_paged_attn_kernel, lam_init=lam_init, n_pg=n_pg),
        grid_spec=grid_spec,
        out_shape=jax.ShapeDtypeStruct((T, 1, W), BF16),
        compiler_params=_cparams(("arbitrary", "arbitrary"), 48 << 20),
        name="paged_diff_attn",
    )(page_table, qd.reshape(T, 1, W), k_new.reshape(T, 1, W), v_new.reshape(T, 1, W),
      da_lambda, onorm, *([cache_k] * n_pg), *([cache_v] * n_pg))
    return out.reshape(T, W)


def _ffn_decode_kernel(pt_ref, x_ref, gf_ref, wg_ref, wu_ref, wd_ref, gfin_ref,
                       q_ref, kn_ref, vn_ref, lam_ref, onorm_ref, ck_hbm, cv_hbm,
                       y_ref, o_ref, kbuf, vbuf, sem, q8_s, m_s, l_s, acc_s,
                       *, lam_init, n_pg, n_slot, sps, col_bounds):
    i = pl.program_id(0)
    n_steps = pl.num_programs(0)
    state = (q8_s, m_s, l_s, acc_s)
    rows = kbuf.shape[1] // n_pg
    gps = pt_ref.shape[1] // n_pg
    groups = [(ls, g) for ls in range(sps) for g in range(gps)]
    n_grp = len(groups)

    def copies(b, g, slot):
        out = []
        for p in range(n_pg):
            page = pt_ref[b, g * n_pg + p]
            dst = pl.ds(p * rows, rows)
            out.append(pltpu.make_async_copy(ck_hbm.at[page], kbuf.at[slot, dst], sem.at[slot, 0]))
            out.append(pltpu.make_async_copy(cv_hbm.at[page], vbuf.at[slot, dst], sem.at[slot, 1]))
        return out

    def start(b, g, slot):
        for cp in copies(b, g, slot):
            cp.start()

    ahead = n_slot - 1

    @pl.when(i == 0)
    def _():
        for n in range(ahead):
            start(groups[n][0], groups[n][1], n % n_slot)

    def fold_group(n):
        ls, g = groups[n]
        b = i * sps + ls
        slot = n % n_slot
        nxt = n + ahead
        if nxt < n_grp:
            start(i * sps + groups[nxt][0], groups[nxt][1], nxt % n_slot)
        else:
            @pl.when(i + 1 < n_steps)
            def _():
                ls2, g2 = groups[nxt - n_grp]
                start((i + 1) * sps + ls2, g2, nxt % n_slot)
        if n in scored:
            s = scored.pop(n)
        else:
            s = arrive_and_score(n)
        if n + 1 < n_grp and groups[n + 1][0] == ls:
            scored[n + 1] = arrive_and_score(n + 1)
        _decode_update(s, [vbuf[slot, p * rows:(p + 1) * rows, :] for p in range(n_pg)], state)
        if g == gps - 1:
            o_ref[b] = _decode_finish(lam_ref, onorm_ref, lam_init, state)

    scored = {}

    def arrive_and_score(n):
        ls, g = groups[n]
        b = i * sps + ls
        slot = n % n_slot
        for cp in copies(b, g, slot):
            cp.wait()
        if g == 0:
            _decode_seed(q_ref[b], kn_ref[b], vn_ref[b], state)
        return _decode_scores([kbuf[slot, p * rows:(p + 1) * rows, :] for p in range(n_pg)], state)

    x = x_ref[...]
    h = _rms(x, gf_ref[...]).astype(BF16)
    n_col = len(col_bounds) - 1

    def up(k):
        lo, hi = col_bounds[k], col_bounds[k + 1]
        return _dot(h, wg_ref[:, lo:hi]), _dot(h, wu_ref[:, lo:hi])

    folded = 0
    acc = None
    gu = up(0)
    for k in range(n_col):
        g, u = gu
        part = _dot((g * _sigmoid(g) * u).astype(BF16), wd_ref[col_bounds[k]:col_bounds[k + 1], :])
        if k + 1 < n_col:
            gu = up(k + 1)
        due = (k + 1) * n_grp // n_col
        for n in range(folded, due):
            fold_group(n)
        folded = due
        acc = part if acc is None else acc + part
    y_ref[...] = _rms(x + acc, gfin_ref[...])


def _ffn_decode(x, gf, wg, wu, wd, gfin, ts, page_table, qd, k_new, v_new, da_lambda, onorm,
                cache_k, cache_v, lam_init, n_pg, n_slot):
    T, D = x.shape
    F = wg.shape[1]
    Ts, n_pages = page_table.shape
    n_phys, rows, dk = cache_k.shape
    W = qd.shape[-1]
    n_tiles = T // ts
    sps = Ts // n_tiles
    mxu_n = 2 * LANES
    col_bounds = tuple(range(0, F, mxu_n)) + (F,)
    tok = pl.BlockSpec((ts, D), lambda i, pt: (i, 0))
    const = lambda shape: pl.BlockSpec(shape, lambda i, pt: (0,) * len(shape),
                                       pipeline_mode=pl.Buffered(1))
    sample = const((Ts, 1, W))
    hbm = pl.BlockSpec(memory_space=pl.ANY)
    grid_spec = pltpu.PrefetchScalarGridSpec(
        num_scalar_prefetch=1,
        grid=(n_tiles,),
        in_specs=[tok, const((1, D)), const(wg.shape), const(wu.shape), const(wd.shape),
                  const((1, D)), sample, sample, sample, const(da_lambda.shape),
                  const((1, LANES)), hbm, hbm],
        out_specs=[tok, pl.BlockSpec((Ts, 1, W), lambda i, pt: (0, 0, 0))],
        scratch_shapes=[pltpu.VMEM((n_slot, n_pg * rows, dk), F32),
                        pltpu.VMEM((n_slot, n_pg * rows, dk), F32),
                        pltpu.SemaphoreType.DMA((n_slot, 2)),
                        pltpu.VMEM((2 * DA_HEADS, LANES), BF16),
                        pltpu.VMEM((2 * DA_HEADS, 1), F32),
                        pltpu.VMEM((2 * DA_HEADS, 1), F32),
                        pltpu.VMEM((2 * DA_HEADS, LANES), F32)],
    )
    y, o = pl.pallas_call(
        functools.partial(_ffn_decode_kernel, lam_init=lam_init, n_pg=n_pg, n_slot=n_slot, sps=sps,
                          col_bounds=col_bounds),
        grid_spec=grid_spec,
        out_shape=[jax.ShapeDtypeStruct((T, D), F32), jax.ShapeDtypeStruct((Ts, 1, W), BF16)],
        compiler_params=_cparams(("arbitrary",), VMEM_LIMIT_CAP),
        name="ffn_decode",
    )(page_table, x, gf, wg, wu, wd, gfin, qd.reshape(Ts, 1, W), k_new.reshape(Ts, 1, W),
      v_new.reshape(Ts, 1, W), da_lambda, onorm, cache_k, cache_v)
    return y, o.reshape(Ts, W)


def _rope_tables(pos, dh, n_maps):
    inv = ROPE_THETA ** (-jnp.arange(0, dh, 2, dtype=F32) / dh)
    ang = pos.astype(F32)[:, None] * inv[None, :]
    cos = jnp.cos(ang)
    sin = jnp.sin(ang)
    cos = jnp.tile(jnp.concatenate([cos, cos], axis=-1), (1, n_maps))
    sin = jnp.tile(jnp.concatenate([-sin, sin], axis=-1), (1, n_maps))
    return cos, sin


def _pick_tile(n, pref):
    t = min(n, pref)
    while n % t:
        t //= 2
    return t


def kernel(x_prompt, x_sample, mem_prompt, cache_k, cache_v, cache_mem_k, cache_mem_v, state_hgrn, page_table, norm_mix, w_in, hg_lb, hg_onorm, da_lambda, da_onorm, w_out, norm_mem_q, norm_mem_kv, w_mq, w_mk, w_mv, w_mo, norm_ffn, w_gate, w_up, w_down, norm_final):
    B, S, D = x_prompt.shape
    T = x_sample.shape[0]
    depth = w_in.shape[0]
    assert depth == 1 and x_sample.shape[1] == 1
    l = 0
    lam_init = 0.8 - 0.6 * math.exp(-0.3 * l)
    n_phys, page = cache_k.shape[1], cache_k.shape[2]
    past_len = page_table.shape[1] * page
    W = w_in.shape[2] // 7

    bf = lambda w: w.astype(BF16)
    w_in_b, w_out_b = bf(w_in[l]), bf(w_out[l])
    w_mq_b, w_mk_b, w_mv_b, w_mo_b = bf(w_mq[l]), bf(w_mk[l]), bf(w_mv[l]), bf(w_mo[l])
    w_gate_b, w_up_b, w_down_b = bf(w_gate[l]), bf(w_up[l]), bf(w_down[l])
    row = lambda g: g.reshape(1, -1)
    lb2 = hg_lb[l:l + 2]
    lam_p = da_lambda[l]

    cos_p, sin_p = _rope_tables(jnp.arange(S), LANES // 2, 2 * DA_HEADS)
    cos_s, sin_s = _rope_tables(past_len + jnp.arange(1), LANES // 2, 2 * DA_HEADS)
    cos_s = jnp.broadcast_to(cos_s, (T, W))
    sin_s = jnp.broadcast_to(sin_s, (T, W))

    mk, mv, mkb, mvb = _mem_kv(mem_prompt, row(norm_mem_kv[l]), w_mk_b, w_mv_b)
    qd, k_p, v_p, kb, v_att, mix_hg, hs_p = _prompt_mix(
        x_prompt, row(norm_mix[l]), w_in_b, lb2, cos_p, sin_p, row(hg_onorm[l]),
        ts=_pick_tile(S, 256))
    mix_da = _diff_attn_prompt(qd, kb, v_att, lam_p, row(da_onorm[l]), lam_init,
                               tq=_pick_tile(S, 1024), tk=_pick_tile(S, 256))
    x2 = _out_cross_prompt(x_prompt, mix_hg, mix_da, w_out_b, row(norm_mem_q[l]), w_mq_b,
                           mkb, mvb, w_mo_b, ts=_pick_tile(S, 512))

    xs = x_sample.reshape(T, D)
    qd_s, k_s, v_s, hq_s, kk_s, f_s, vh_s, gate_s = _sample_mix(
        xs, row(norm_mix[l]), w_in_b, lb2, cos_s, sin_s)
    mix_hg_s, hs_s = _sample_hgrn(hq_s, kk_s, f_s, vh_s, gate_s, row(hg_onorm[l]), state_hgrn[l])
    ck = cache_k.reshape(depth * n_phys, page * DA_HEADS, LANES)
    cv = cache_v.reshape(depth * n_phys, page * DA_HEADS, LANES)
    n_pages = page_table.shape[1]
    ffn_ts = _pick_tile(B * S, 512)
    n_tiles = (B * S) // ffn_ts
    n_pg = _pick_tile(n_pages, 8)
    n_slot = 4
    ffn_args = (row(norm_ffn[l]), w_gate_b, w_up_b, w_down_b, row(norm_final))
    if T % n_tiles == 0 and ((T // n_tiles) * (n_pages // n_pg)) % n_slot == 0:
        y_p, mix_da_s = _ffn_decode(x2.reshape(B * S, D), *ffn_args, ffn_ts, page_table, qd_s, k_s,
                                    v_s, lam_p, row(da_onorm[l]), ck, cv, lam_init, n_pg, n_slot)
    else:
        y_p = _ffn(x2.reshape(B * S, D), *ffn_args, ts=ffn_ts)
        mix_da_s = _paged_attn(page_table, qd_s, k_s, v_s, lam_p, row(da_onorm[l]), ck, cv,
                               lam_init, n_pg=_pick_tile(n_pages, 16))
    y_p = y_p.reshape(B, S, D)
    MW = w_mk.shape[2]
    x2_s = _out_cross_sample(xs, mix_hg_s, mix_da_s, w_out_b, row(norm_mem_q[l]), w_mq_b,
                             cache_mem_k[l], cache_mem_v[l],
                             w_mo_b)
    y_s = _ffn(x2_s, row(norm_ffn[l]), w_gate_b, w_up_b, w_down_b, row(norm_final), ts=T)

    dk = LANES
    return (y_p, y_s.reshape(T, 1, D),
            hs_p[None],
            k_p.reshape(1, B, S, DA_HEADS, dk), v_p.reshape(1, B, S, DA_HEADS, dk),
            mk[None], mv[None],
            hs_s[None],
            k_s.reshape(1, T, 1, DA_HEADS, dk), v_s.reshape(1, T, 1, DA_HEADS, dk))
```

```python
import functools
import math

import jax
import jax.numpy as jnp
from jax import lax
from jax.experimental import pallas as pl
from jax.experimental.pallas import tpu as pltpu

F32 = jnp.float32
BF16 = jnp.bfloat16
EPS = 1e-6
ROPE_THETA = 10000.0

HG_HEADS = 4
DA_HEADS = 4
MEM_HEADS = 4
HG_CHUNK = 64
LANES = 128
BF16_ROWS = 16
VMEM_LIMIT_CAP = 56 << 20

_NT = (((1,), (1,)), ((), ()))
_TN = (((0,), (0,)), ((), ()))


def _dot(a, b):
    return jnp.dot(a, b, preferred_element_type=F32)


def _dot_nt(a, b):
    return lax.dot_general(a, b, _NT, preferred_element_type=F32)


def _dot_tn(a, b):
    return lax.dot_general(a, b, _TN, preferred_element_type=F32)


def _rms(x, g):
    ms = jnp.mean(x * x, axis=-1, keepdims=True)
    return x * lax.rsqrt(ms + EPS) * g


def _sigmoid(x):
    return 1.0 / (1.0 + jnp.exp(-x))


def _cparams(semantics, vmem_bytes):
    return pltpu.CompilerParams(
        dimension_semantics=semantics,
        vmem_limit_bytes=int(min(max(vmem_bytes, 16 << 20), VMEM_LIMIT_CAP)))


def _const_spec(shape):
    nd = len(shape)
    return pl.BlockSpec(shape, lambda *_: (0,) * nd, pipeline_mode=pl.Buffered(1))


def _lower_bound(lb_ref):
    a0 = lb_ref[0:1, :]
    a1 = lb_ref[1:2, :]
    m = jnp.maximum(a0, a1)
    e0 = jnp.exp(a0 - m)
    e1 = jnp.exp(a1 - m)
    return e0 / (e0 + e1)


def _rope(x, cos, sin_signed):
    n = x.shape[-1]
    lane = lax.broadcasted_iota(jnp.int32, x.shape, x.ndim - 1)
    swapped = jnp.where((lane & 63) < 32,
                        pltpu.roll(x, n - 32, x.ndim - 1),
                        pltpu.roll(x, 32, x.ndim - 1))
    return x * cos + swapped * sin_signed


def _lambda(lam_ref, lam_init):
    lp = lam_ref[...]
    s01 = jnp.sum(lp[0:1, :] * lp[1:2, :], axis=-1, keepdims=True)
    s23 = jnp.sum(lp[2:3, :] * lp[3:4, :], axis=-1, keepdims=True)
    return jnp.exp(s01) - jnp.exp(s23) + lam_init


def _mixer_sections(z, lb):
    w = z.shape[-1] // 7
    hq, zf, hi, hg, dq, dk, dv = (z[:, i * w:(i + 1) * w] for i in range(7))
    sig = _sigmoid(zf)
    logf = jnp.log(lb + (1.0 - lb) * sig)
    k_hg = (1.0 - lb) * (1.0 - sig)
    gate = hg * _sigmoid(hg)
    return hq, k_hg, logf, hi, gate, dq, dk, dv


def _head_rms(o, g):
    outs = []
    for h in range(o.shape[-1] // LANES):
        oh = o[:, h * LANES:(h + 1) * LANES]
        outs.append(_rms(oh, g))
    return jnp.concatenate(outs, axis=-1)


def _memkv_kernel(mem_ref, g_ref, wk_ref, wv_ref, k_ref, v_ref, kb_ref, vb_ref):
    m = _rms(mem_ref[0], g_ref[...]).astype(BF16)
    k = _dot(m, wk_ref[...])
    v = _dot(m, wv_ref[...])
    dh = k_ref.shape[-1]
    for h in range(MEM_HEADS):
        k_ref[0, :, h, :] = k[:, h * dh:(h + 1) * dh]
        v_ref[0, :, h, :] = v[:, h * dh:(h + 1) * dh]
    kb_ref[0] = k.astype(BF16)
    vb_ref[0] = v.astype(BF16)


def _mem_kv(mem, g, wk, wv):
    B, N, D = mem.shape
    W = wk.shape[1]
    dh = W // MEM_HEADS
    blk = lambda d: pl.BlockSpec((1, N, d), lambda b: (b, 0, 0))
    blk4 = pl.BlockSpec((1, N, MEM_HEADS, dh), lambda b: (b, 0, 0, 0))
    return pl.pallas_call(
        _memkv_kernel,
        grid=(B,),
        in_specs=[blk(D), _const_spec((1, D)), _const_spec((D, W)), _const_spec((D, W))],
        out_specs=[blk4, blk4, blk(W), blk(W)],
        out_shape=[jax.ShapeDtypeStruct((B, N, MEM_HEADS, dh), F32)] * 2
        + [jax.ShapeDtypeStruct((B, N, W), BF16)] * 2,
        compiler_params=_cparams(("arbitrary",), 40 << 20),
        name="mem_kv",
    )(mem, g, wk, wv)


def _hgrn_levels(q, kk, lf, v):
    C, W = q.shape
    t = lax.broadcasted_iota(jnp.int32, (C, W), 0)
    n_lvl = int(math.log2(C))

    c = lf
    lvls = []
    for l in range(n_lvl):
        m = 1 << l
        upper = (t & m) != 0
        if m < 8:
            y = c
            for i in range(l):
                y = jnp.where((t & (1 << i)) == 0, pltpu.roll(y, C - (1 << i), 0), y)
            bc = jnp.where(upper, pltpu.roll(y, m, 0), y)
        else:
            pieces = []
            for j in range(C // (2 * m)):
                r = 2 * m * j + m - 1
                pieces.append(jnp.broadcast_to(c[r:r + 1, :], (2 * m, W)))
            bc = pieces[0] if len(pieces) == 1 else jnp.concatenate(pieces, axis=0)
        e = jnp.exp(jnp.where(upper, c, bc - c))
        lvls.append((jnp.where(upper, q, kk) * e).astype(BF16))
        c = c + jnp.where(upper, bc, 0.0)
    b = c
    b_last = b[C - 1:C, :]
    return dict(lvls=lvls, q=q.astype(BF16), k=kk.astype(BF16), v=v.astype(BF16),
                q_state=(q * jnp.exp(b)).astype(BF16),
                k_state=(kk * jnp.exp(b_last - b)).astype(BF16),
                decay=jnp.exp(b_last))


def _hgrn_products(lv, st_ref):
    out = []
    for h in range(lv["q"].shape[1] // LANES):
        sl = slice(h * LANES, (h + 1) * LANES)
        pairs = [_dot_nt(lv["q"][:, sl], lv["k"][:, sl])]
        pairs += [_dot_nt(r[:, sl], r[:, sl]) for r in lv["lvls"]]
        o_state = _dot_nt(lv["q_state"][:, sl], st_ref[h].astype(BF16))
        st_term = _dot_tn(lv["v"][:, sl], lv["k_state"][:, sl])
        out.append((pairs, o_state, st_term))
    return out


def _hgrn_finish(lv, prods, st_ref):
    C = lv["q"].shape[0]
    ti = lax.broadcasted_iota(jnp.int32, (C, C), 0)
    si = lax.broadcasted_iota(jnp.int32, (C, C), 1)
    diff_bits = jnp.where(ti > si, ti ^ si, 0)
    outs = []
    for h, (pairs, o_state, st_term) in enumerate(prods):
        sl = slice(h * LANES, (h + 1) * LANES)
        a = jnp.where(ti == si, pairs[0], 0.0)
        for l, pr in enumerate(pairs[1:]):
            a = a + jnp.where((diff_bits >> l) == 1, pr, 0.0)
        outs.append(_dot(a.astype(BF16), lv["v"][:, sl]) + o_state)
        st_ref[h] = st_ref[h] * lv["decay"][:, sl] + st_term
    return jnp.concatenate(outs, axis=-1)


def _prompt_mix_kernel(x_ref, g_ref, w_ref, lb_ref, cos_ref, sin_ref, onorm_ref,
                       qd_ref, k_ref, v_ref, kb_ref, va_ref, mix_ref, state_ref,
                       o_s, st_s):
    s = pl.program_id(1)
    last = pl.num_programs(1) - 1

    @pl.when(s == 0)
    def _():
        st_s[...] = jnp.zeros_like(st_s)

    h = _rms(x_ref[0], g_ref[...]).astype(BF16)

    ts = x_ref.shape[1]
    sec_w = w_ref.shape[1] // 7
    n_chunk = ts // HG_CHUNK
    stage = {"done": 0, "lv": None}

    def project(sec):
        return _dot(h, w_ref[:, sec * sec_w:(sec + 1) * sec_w])

    def levels(ci):
        rows = slice(ci * HG_CHUNK, (ci + 1) * HG_CHUNK)
        return _hgrn_levels(hq[rows, :], k_hg[rows, :], logf[rows, :], hi[rows, :])

    def recur(upto):
        for ci in range(stage["done"], min(upto, n_chunk)):
            lv = stage["lv"] if stage["lv"] is not None else levels(ci)
            prods = _hgrn_products(lv, st_s)
            stage["lv"] = levels(ci + 1) if ci + 1 < n_chunk else None
            o_s[ci * HG_CHUNK:(ci + 1) * HG_CHUNK, :] = _hgrn_finish(lv, prods, st_s)
        stage["done"] = max(stage["done"], min(upto, n_chunk))

    def head_rows_out(ref, val):
        for hh in range(DA_HEADS):
            ref[0, pl.ds(hh, ts, stride=DA_HEADS), :] = val[:, hh * LANES:(hh + 1) * LANES]

    lb = _lower_bound(lb_ref)
    cos = cos_ref[...]
    sin = sin_ref[...]
    hq = project(0)
    sig = _sigmoid(project(1))
    logf = jnp.log(lb + (1.0 - lb) * sig)
    k_hg = (1.0 - lb) * (1.0 - sig)
    hi = project(2)
    hg = project(3)
    recur(n_chunk // 4)
    q_da = _rope(project(4), cos, sin)
    qd_ref[0] = (q_da * (math.log2(math.e) * float(LANES // 2) ** -0.5)).astype(BF16)
    recur(n_chunk // 2)
    k_da = _rope(project(5), cos, sin)
    head_rows_out(k_ref, k_da)
    kb_ref[0] = k_da.astype(BF16)
    recur(3 * n_chunk // 4)
    dv = project(6)
    head_rows_out(v_ref, dv)
    va_ref[0] = dv
    recur(n_chunk)
    gate = hg * _sigmoid(hg)
    mix_ref[0] = (_head_rms(o_s[...], onorm_ref[...]) * gate).astype(BF16)

    @pl.when(s == last)
    def _():
        for hh in range(HG_HEADS):
            state_ref[0, hh] = st_s[hh].T


def _prompt_mix(x, g, w_in, hg_lb, cos, sin, onorm, ts):
    B, S, D = x.shape
    W = w_in.shape[1] // 7
    tok = lambda d: pl.BlockSpec((1, ts, d), lambda b, s: (b, s, 0))
    tab = pl.BlockSpec((ts, W), lambda b, s: (s, 0))
    rows = pl.BlockSpec((1, ts * DA_HEADS, LANES), lambda b, s: (b, s, 0))
    state = pl.BlockSpec((1, HG_HEADS, LANES, LANES), lambda b, s: (b, 0, 0, 0))
    sds = jax.ShapeDtypeStruct
    return pl.pallas_call(
        _prompt_mix_kernel,
        grid=(B, S // ts),
        in_specs=[tok(D), _const_spec((1, D)), _const_spec(w_in.shape),
                  _const_spec(hg_lb.shape), tab, tab, _const_spec((1, LANES))],
        out_specs=[tok(W), rows, rows, tok(W), tok(W), tok(W), state],
        out_shape=[sds((B, S, W), BF16), sds((B, S * DA_HEADS, LANES), F32),
                   sds((B, S * DA_HEADS, LANES), F32), sds((B, S, W), BF16),
                   sds((B, S, W), F32), sds((B, S, W), BF16),
                   sds((B, HG_HEADS, LANES, LANES), F32)],
        scratch_shapes=[pltpu.VMEM((ts, W), F32), pltpu.VMEM((HG_HEADS, LANES, LANES), F32)],
        compiler_params=_cparams(("arbitrary", "arbitrary"), 48 << 20),
        name="prompt_mix",
    )(x, g, w_in, hg_lb, cos, sin, onorm)


def _diff_attn_kernel(q_ref, k_ref, v_ref, lam_ref, onorm_ref, o_ref,
                      qq_s, vt_s, m_s, acc_s, *, lam_init, tk):
    i = pl.program_id(2)
    tq = q_ref.shape[1]
    n_ck = tq // tk
    n_kv = v_ref.shape[1] // tk

    @pl.when(i == 0)
    def _():
        ones = jnp.ones((vt_s.shape[1] - LANES, tk), BF16)
        for jj in range(n_kv):
            vt_s[jj, 0:LANES, :] = v_ref[0, jj * tk:(jj + 1) * tk, :].T.astype(BF16)
            vt_s[jj, LANES:, :] = ones

    q = q_ref[0]
    lane = lax.broadcasted_iota(jnp.int32, q.shape, 1)
    zero = jnp.zeros_like(q)
    qq_s[0:tq, :] = jnp.where(lane < LANES // 2, q, zero)
    qq_s[tq:2 * tq, :] = jnp.where(lane >= LANES // 2, q, zero)
    m_s[...] = jnp.full(m_s.shape, -jnp.inf, F32)
    acc_s[...] = jnp.zeros(acc_s.shape, F32)

    def scores(j, c):
        k = k_ref[0, pl.ds(pl.multiple_of(j * tk, tk), tk), :]
        return _dot_nt(k, qq_s[c * tk:(c + 1) * tk, :])

    def softmax(j, c, tri, s):
        cols = slice(c * tk, (c + 1) * tk)
        if tri:
            key = lax.broadcasted_iota(jnp.int32, s.shape, 0)
            qry = lax.broadcasted_iota(jnp.int32, s.shape, 1)
            s = jnp.where(key <= qry, s, -jnp.inf)
        m_old = m_s[:, cols]
        m_new = jnp.maximum(m_old, jnp.max(s, axis=0, keepdims=True))
        alpha = jnp.exp2(m_old - m_new)
        p = jnp.exp2(s - m_new)
        m_s[:, cols] = m_new
        return cols, alpha, _dot(vt_s[j], p.astype(BF16))

    def accumulate(cols, alpha, pv):
        acc_s[:, cols] = alpha * acc_s[:, cols] + pv

    ahead = 4

    def run(work):
        s = {n: scores(*work[n][:2]) for n in range(min(ahead, len(work)))}
        pending = None
        for n, (j, c, tri) in enumerate(work):
            if n + ahead < len(work):
                s[n + ahead] = scores(*work[n + ahead][:2])
            done = softmax(j, c, tri, s.pop(n))
            if pending is not None:
                accumulate(*pending)
            pending = done
        accumulate(*pending)

    per_iter = 2 if n_ck % 2 == 0 else 1

    def full_blocks(jj, carry):
        run([(jj * per_iter + r, c, False) for r in range(per_iter) for c in range(2 * n_ck)])
        return carry

    lax.fori_loop(0, (i * n_ck) // per_iter, full_blocks, 0)
    run([(i * n_ck + d, mp * n_ck + cq, cq == d)
         for d in range(n_ck) for mp in range(2) for cq in range(d, n_ck)])

    lam = _lambda(lam_ref, lam_init)
    a = acc_s[0:LANES, :] * (1.0 / acc_s[LANES:LANES + 1, :])
    o_t = a[:, 0:tq] - lam * a[:, tq:2 * tq]
    ms = jnp.mean(o_t * o_t, axis=0, keepdims=True)
    y = (o_t * lax.rsqrt(ms + EPS)).T
    o_ref[0] = (y * onorm_ref[...] * (1.0 - lam_init)).astype(BF16)


def _diff_attn_prompt(qd, kb, v, da_lambda, onorm, lam_init, tq, tk):
    B, S, W = qd.shape
    H = W // LANES
    qspec = pl.BlockSpec((1, tq, LANES), lambda b, h, i: (b, i, h))
    kvspec = pl.BlockSpec((1, S, LANES), lambda b, h, i: (b, 0, h))
    return pl.pallas_call(
        functools.partial(_diff_attn_kernel, lam_init=lam_init, tk=tk),
        grid=(B, H, S // tq),
        in_specs=[qspec, kvspec, kvspec, _const_spec(da_lambda.shape), _const_spec((1, LANES))],
        out_specs=qspec,
        out_shape=jax.ShapeDtypeStruct((B, S, W), BF16),
        scratch_shapes=[pltpu.VMEM((2 * tq, LANES), BF16),
                        pltpu.VMEM((S // tk, LANES + BF16_ROWS, tk), BF16),
                        pltpu.VMEM((1, 2 * tq), F32),
                        pltpu.VMEM((LANES + BF16_ROWS, 2 * tq), F32)],
        compiler_params=_cparams(("arbitrary",) * 3, 32 << 20),
        name="diff_attn_prompt",
    )(qd, kb, v, da_lambda, onorm)


def _cross_attn_heads(q, mk_head, mv_head):
    dh = q.shape[-1] // MEM_HEADS
    scores = [_dot_nt(q[:, h * dh:(h + 1) * dh], mk_head(h)) for h in range(MEM_HEADS)]
    probs = []
    for s in scores:
        m = jnp.max(s, axis=-1, keepdims=True)
        p = jnp.exp(s - m)
        l = jnp.sum(p, axis=-1, keepdims=True)
        probs.append((p / l).astype(BF16))
    return jnp.concatenate([_dot(probs[h], mv_head(h)) for h in range(MEM_HEADS)], axis=-1)


def _out_cross_kernel(x_ref, mhg_ref, mda_ref, wo_ref, gq_ref, wq_ref, mk_ref, mv_ref,
                      wmo_ref, x2_ref):
    w = mhg_ref.shape[-1]
    dh = wq_ref.shape[1] // MEM_HEADS
    ts = x_ref.shape[1]
    halves = [slice(0, ts // 2), slice(ts // 2, ts)] if ts % 16 == 0 else [slice(0, ts)]
    x1 = [x_ref[0, r, :] + _dot(mhg_ref[0, r, :], wo_ref[0:w, :])
          + _dot(mda_ref[0, r, :], wo_ref[w:2 * w, :]) for r in halves]
    q = [(_dot(_rms(v, gq_ref[...]).astype(BF16), wq_ref[...]) * (float(dh) ** -0.5)).astype(BF16)
         for v in x1]
    mk = lambda h: mk_ref[0, :, h * dh:(h + 1) * dh]
    mv = lambda h: mv_ref[0, :, h * dh:(h + 1) * dh]
    o = [_cross_attn_heads(v, mk, mv) for v in q]
    for r, v1, vo in zip(halves, x1, o):
        x2_ref[0, r, :] = v1 + _dot(vo.astype(BF16), wmo_ref[...])


def _out_cross_prompt(x, mhg, mda, w_out, gq, w_mq, mkb, mvb, w_mo, ts):
    B, S, D = x.shape
    W = mhg.shape[-1]
    N, MW = mkb.shape[1], mkb.shape[2]
    tok = lambda d: pl.BlockSpec((1, ts, d), lambda b, s: (b, s, 0))
    mem = pl.BlockSpec((1, N, MW), lambda b, s: (b, 0, 0))
    return pl.pallas_call(
        _out_cross_kernel,
        grid=(B, S // ts),
        in_specs=[tok(D), tok(W), tok(W), _const_spec(w_out.shape), _const_spec((1, D)),
                  _const_spec(w_mq.shape), mem, mem, _const_spec(w_mo.shape)],
        out_specs=tok(D),
        out_shape=jax.ShapeDtypeStruct((B, S, D), F32),
        compiler_params=_cparams(("arbitrary", "arbitrary"), 48 << 20),
        name="out_cross_prompt",
    )(x, mhg, mda, w_out, gq, w_mq, mkb, mvb, w_mo)


def _out_cross_decode_kernel(pt_ref, x_ref, mhg_ref, mda_ref, wo_ref, gq_ref, wq_ref, mk_ref, mv_ref,
                             wmo_ref, q_ref, kn_ref, vn_ref, ck_hbm, cv_hbm,
                             x2_ref, m_ref, l_ref, a_ref, kbuf, vbuf, sem, q8_s, m_s, l_s, acc_s,
                             *, n_pg, sps, g_hi):
    i = pl.program_id(0) * pl.num_programs(1) + pl.program_id(1)
    n_steps = pl.num_programs(0) * pl.num_programs(1)
    state = (q8_s, m_s, l_s, acc_s)

    def begin(b):
        _decode_seed(q_ref[b], kn_ref[b], vn_ref[b], state)

    def end(b):
        m_ref[b], l_ref[b], a_ref[b] = _decode_save(state)

    fold_group, n_grp = _page_ring(i, n_steps, pt_ref, ck_hbm, cv_hbm, kbuf, vbuf, sem, state,
                                   n_pg=n_pg, sps=sps, g_lo=0, g_hi=g_hi, begin=begin, end=end)
    folded = [0]

    def fold_until(frac):
        due = int(round(frac * n_grp))
        for n in range(folded[0], due):
            fold_group(n)
        folded[0] = max(folded[0], due)

    w = mhg_ref.shape[-1]
    dh = wq_ref.shape[1] // MEM_HEADS
    ts = x_ref.shape[1]
    halves = [slice(0, ts // 2), slice(ts // 2, ts)] if ts % 16 == 0 else [slice(0, ts)]
    x1 = [x_ref[0, r, :] + _dot(mhg_ref[0, r, :], wo_ref[0:w, :])
          + _dot(mda_ref[0, r, :], wo_ref[w:2 * w, :]) for r in halves]
    fold_until(0.25)
    q = [(_dot(_rms(v, gq_ref[...]).astype(BF16), wq_ref[...]) * (float(dh) ** -0.5)).astype(BF16)
         for v in x1]
    fold_until(0.5)
    mk = lambda h: mk_ref[0, :, h * dh:(h + 1) * dh]
    mv = lambda h: mv_ref[0, :, h * dh:(h + 1) * dh]
    o = [_cross_attn_heads(v, mk, mv) for v in q]
    fold_until(0.75)
    out = [v1 + _dot(vo.astype(BF16), wmo_ref[...]) for v1, vo in zip(x1, o)]
    fold_until(1.0)
    for r, v in zip(halves, out):
        x2_ref[0, r, :] = v


def _out_cross_decode(x, mhg, mda, w_out, gq, w_mq, mkb, mvb, w_mo, ts, page_table, qd, k_new, v_new,
                      cache_k, cache_v, n_pg, n_slot, g_hi):
    B, S, D = x.shape
    W = mhg.shape[-1]
    N, MW = mkb.shape[1], mkb.shape[2]
    Ts = page_table.shape[0]
    n_phys, rows, dk = cache_k.shape
    Wq = qd.shape[-1]
    sps = Ts // (B * (S // ts))
    tok = lambda d: pl.BlockSpec((1, ts, d), lambda b, s, pt: (b, s, 0))
    mem = pl.BlockSpec((1, N, MW), lambda b, s, pt: (b, 0, 0))
    const = lambda shape: pl.BlockSpec(shape, lambda b, s, pt: (0,) * len(shape),
                                       pipeline_mode=pl.Buffered(1))
    sample = const((Ts, 1, Wq))
    st = pl.BlockSpec((Ts, 2 * DA_HEADS, LANES), lambda b, s, pt: (0, 0, 0))
    hbm = pl.BlockSpec(memory_space=pl.ANY)
    grid_spec = pltpu.PrefetchScalarGridSpec(
        num_scalar_prefetch=1,
        grid=(B, S // ts),
        in_specs=[tok(D), tok(W), tok(W), const(w_out.shape), const((1, D)), const(w_mq.shape),
                  mem, mem, const(w_mo.shape), sample, sample, sample, hbm, hbm],
        out_specs=[tok(D), st, st, st],
        scratch_shapes=_ring_scratch(n_slot, n_pg, rows, dk),
    )
    st_shape = jax.ShapeDtypeStruct((Ts, 2 * DA_HEADS, LANES), F32)
    x2, m, l, a = pl.pallas_call(
        functools.partial(_out_cross_decode_kernel, n_pg=n_pg, sps=sps, g_hi=g_hi),
        grid_spec=grid_spec,
        out_shape=[jax.ShapeDtypeStruct((B, S, D), F32), st_shape, st_shape, st_shape],
        compiler_params=_cparams(("arbitrary", "arbitrary"), VMEM_LIMIT_CAP),
        name="out_cross_decode",
    )(page_table, x, mhg, mda, w_out, gq, w_mq, mkb, mvb, w_mo, qd.reshape(Ts, 1, Wq),
      k_new.reshape(Ts, 1, Wq), v_new.reshape(Ts, 1, Wq), cache_k, cache_v)
    return x2, (m, l, a)


def _out_cross_sample_kernel(x_ref, mhg_ref, mda_ref, wo_ref, gq_ref, wq_ref, mk_hbm, mv_hbm,
                             wmo_ref, x2_ref, x1_s, q_s, o_s, kbuf, vbuf, sem):
    b = pl.program_id(0)
    n_b = pl.num_programs(0)
    w = mhg_ref.shape[-1]

    def copies(bb, slot):
        out = []
        for h in range(MEM_HEADS):
            out.append(pltpu.make_async_copy(mk_hbm.at[bb, :, h, :], kbuf.at[slot, h], sem.at[slot, 0]))
            out.append(pltpu.make_async_copy(mv_hbm.at[bb, :, h, :], vbuf.at[slot, h], sem.at[slot, 1]))
        return out

    @pl.when(b == 0)
    def _():
        for cp in copies(0, 0):
            cp.start()
        x1 = x_ref[...] + _dot(mhg_ref[...], wo_ref[0:w, :]) + _dot(mda_ref[...], wo_ref[w:2 * w, :])
        x1_s[...] = x1
        hq = _rms(x1, gq_ref[...]).astype(BF16)
        dh = wq_ref.shape[1] // MEM_HEADS
        q_s[...] = _dot(hq, wq_ref[...]) * (float(dh) ** -0.5)

    slot = b % 2

    @pl.when(b + 1 < n_b)
    def _():
        for cp in copies(b + 1, 1 - slot):
            cp.start()

    for cp in copies(b, slot):
        cp.wait()
    q = q_s[pl.ds(b, 1), :].astype(BF16)
    o_s[pl.ds(b, 1), :] = _cross_attn_heads(q, lambda h: kbuf[slot, h].astype(BF16),
                                            lambda h: vbuf[slot, h].astype(BF16))

    @pl.when(b == n_b - 1)
    def _():
        x2_ref[...] = x1_s[...] + _dot(o_s[...].astype(BF16), wmo_ref[...])


def _out_cross_sample(x, mhg, mda, w_out, gq, w_mq, mem_k, mem_v, w_mo):
    T, D = x.shape
    _, N, heads, dh = mem_k.shape
    MW = heads * dh
    hbm = pl.BlockSpec(memory_space=pl.ANY)
    return pl.pallas_call(
        _out_cross_sample_kernel,
        grid=(T,),
        in_specs=[_const_spec(x.shape), _const_spec(mhg.shape), _const_spec(mda.shape),
                  _const_spec(w_out.shape), _const_spec((1, D)), _const_spec(w_mq.shape),
                  hbm, hbm, _const_spec(w_mo.shape)],
        out_specs=pl.BlockSpec((T, D), lambda b: (0, 0)),
        out_shape=jax.ShapeDtypeStruct((T, D), F32),
        scratch_shapes=[pltpu.VMEM((T, D), F32), pltpu.VMEM((T, MW), F32), pltpu.VMEM((T, MW), F32),
                        pltpu.VMEM((2, heads, N, dh), F32), pltpu.VMEM((2, heads, N, dh), F32),
                        pltpu.SemaphoreType.DMA((2, 2))],
        compiler_params=_cparams(("arbitrary",), 32 << 20),
        name="out_cross_sample",
    )(x, mhg, mda, w_out, gq, w_mq, mem_k, mem_v, w_mo)


def _ffn_kernel(x_ref, gf_ref, wg_ref, wu_ref, wd_ref, gfin_ref, y_ref):
    x = x_ref[...]
    h = _rms(x, gf_ref[...]).astype(BF16)
    g = _dot(h, wg_ref[...])
    u = _dot(h, wu_ref[...])
    a = (g * _sigmoid(g) * u).astype(BF16)
    x3 = x + _dot(a, wd_ref[...])
    y_ref[...] = _rms(x3, gfin_ref[...])


def _ffn(x, gf, wg, wu, wd, gfin, ts):
    T, D = x.shape
    tok = pl.BlockSpec((ts, D), lambda i: (i, 0))
    return pl.pallas_call(
        _ffn_kernel,
        grid=(T // ts,),
        in_specs=[tok, _const_spec((1, D)), _const_spec(wg.shape), _const_spec(wu.shape),
                  _const_spec(wd.shape), _const_spec((1, D))],
        out_specs=tok,
        out_shape=jax.ShapeDtypeStruct((T, D), F32),
        compiler_params=_cparams(("arbitrary",), VMEM_LIMIT_CAP),
        name="ffn",
    )(x, gf, wg, wu, wd, gfin)


def _sample_mix_kernel(x_ref, g_ref, w_ref, lb_ref, cos_ref, sin_ref,
                       qd_ref, k_ref, v_ref, hq_ref, kk_ref, f_ref, vh_ref, gate_ref):
    h = _rms(x_ref[...], g_ref[...]).astype(BF16)
    z = _dot(h, w_ref[...])
    lb = _lower_bound(lb_ref)
    hq, k_hg, logf, hi, gate, dq, dk, dv = _mixer_sections(z, lb)
    cos = cos_ref[...]
    sin = sin_ref[...]
    qd_ref[...] = _rope(dq, cos, sin) * (float(LANES // 2) ** -0.5)
    k_ref[...] = _rope(dk, cos, sin)
    v_ref[...] = dv
    hq_ref[...] = hq
    kk_ref[...] = k_hg
    f_ref[...] = jnp.exp(logf)
    vh_ref[...] = hi
    gate_ref[...] = gate


def _sample_mix(x, g, w_in, hg_lb, cos, sin):
    T, D = x.shape
    W = w_in.shape[1] // 7
    sds = jax.ShapeDtypeStruct
    return pl.pallas_call(
        _sample_mix_kernel,
        out_shape=[sds((T, W), F32)] * 8,
        compiler_params=_cparams(None, 32 << 20),
        name="sample_mix",
    )(x, g, w_in, hg_lb, cos, sin)


def _sample_hgrn_kernel(hq_ref, kk_ref, f_ref, vh_ref, gate_ref, onorm_ref, s0_ref,
                        mix_ref, s1_ref):
    nb = s0_ref.shape[0]
    g0 = pl.program_id(0) * nb
    W = hq_ref.shape[-1]

    def columns(ref, h):
        rows = ref[pl.ds(pl.multiple_of(g0, nb), nb), h * LANES:(h + 1) * LANES]
        pad = jnp.zeros((LANES - nb, LANES), F32)
        return jnp.concatenate([rows, pad], axis=0).T

    o_rows = []
    for h in range(HG_HEADS):
        q_t, k_t, f_t = columns(hq_ref, h), columns(kk_ref, h), columns(f_ref, h)
        v_rows = vh_ref[pl.ds(pl.multiple_of(g0, nb), nb), h * LANES:(h + 1) * LANES]
        o_h = []
        for j in range(nb):
            bcast = lambda tile: jnp.broadcast_to(tile[:, j:j + 1], (LANES, LANES))
            s_new = bcast(f_t) * s0_ref[j, h] + bcast(k_t) * v_rows[j:j + 1, :]
            s1_ref[j, h] = s_new
            o_h.append(jnp.sum(bcast(q_t) * s_new, axis=0, keepdims=True))
        o_rows.append(jnp.concatenate(o_h, axis=0))
    o = jnp.concatenate(o_rows, axis=-1)
    gate = gate_ref[pl.ds(pl.multiple_of(g0, nb), nb), :]
    mix_ref[...] = (_head_rms(o, onorm_ref[...]) * gate).astype(BF16)


def _sample_hgrn(hq, kk, f, vh, gate, onorm, s0, nb=8):
    T, W = hq.shape
    st = pl.BlockSpec((nb, HG_HEADS, LANES, LANES), lambda i: (i, 0, 0, 0))
    full = _const_spec((T, W))
    return pl.pallas_call(
        _sample_hgrn_kernel,
        grid=(T // nb,),
        in_specs=[full] * 5 + [_const_spec((1, LANES)), st],
        out_specs=[pl.BlockSpec((nb, W), lambda i: (i, 0)), st],
        out_shape=[jax.ShapeDtypeStruct((T, W), BF16), jax.ShapeDtypeStruct(s0.shape, F32)],
        compiler_params=_cparams(("arbitrary",), 32 << 20),
        name="sample_hgrn",
    )(hq, kk, f, vh, gate, onorm, s0)


def _head_rows(x):
    return jnp.concatenate([x[:, h * LANES:(h + 1) * LANES] for h in range(DA_HEADS)] * 2, axis=0)


def _decode_queries(q):
    row = lax.broadcasted_iota(jnp.int32, (2 * DA_HEADS, LANES), 0)
    lane = lax.broadcasted_iota(jnp.int32, (2 * DA_HEADS, LANES), 1)
    return jnp.where((row // DA_HEADS) == (lane // (LANES // 2)), _head_rows(q), 0.0)


def _decode_seed(q, k_new, v_new, state):
    q8_s, m_s, l_s, acc_s = state
    q8 = _decode_queries(q)
    q8_s[...] = q8.astype(BF16)
    m_s[...] = jnp.sum(q8 * _head_rows(k_new), axis=-1, keepdims=True)
    l_s[...] = jnp.ones(l_s.shape, F32)
    acc_s[...] = _head_rows(v_new)


def _decode_resume(q, m, l, acc, state):
    q8_s, m_s, l_s, acc_s = state
    q8_s[...] = _decode_queries(q).astype(BF16)
    m_s[...] = m[:, 0:1]
    l_s[...] = l[:, 0:1]
    acc_s[...] = acc


def _decode_save(state):
    _, m_s, l_s, acc_s = state
    shape = acc_s.shape
    return jnp.broadcast_to(m_s[...], shape), jnp.broadcast_to(l_s[...], shape), acc_s[...]


def _decode_scores(k_pages, state):
    H = DA_HEADS
    q8 = state[0][...]
    s = jnp.concatenate([_dot_nt(q8, k.astype(BF16)) for k in k_pages], axis=-1)
    row = lax.broadcasted_iota(jnp.int32, s.shape, 0)
    col = lax.broadcasted_iota(jnp.int32, s.shape, 1)
    return jnp.where((col % H) == (row % H), s, -jnp.inf)


def _decode_update(s, v_pages, state):
    _, m_s, l_s, acc_s = state
    m_old = m_s[...]
    m_new = jnp.maximum(m_old, jnp.max(s, axis=-1, keepdims=True))
    alpha = jnp.exp(m_old - m_new)
    pr = jnp.exp(s - m_new)
    l_s[...] = alpha * l_s[...] + jnp.sum(pr, axis=-1, keepdims=True)
    pr = pr.astype(BF16)
    rows = v_pages[0].shape[0]
    pv = _dot(pr[:, 0:rows], v_pages[0].astype(BF16))
    for p in range(1, len(v_pages)):
        pv = pv + _dot(pr[:, p * rows:(p + 1) * rows], v_pages[p].astype(BF16))
    acc_s[...] = alpha * acc_s[...] + pv
    m_s[...] = m_new


def _decode_finish(lam_ref, onorm_ref, lam_init, state):
    _, _, l_s, acc_s = state
    H = DA_HEADS
    lam = _lambda(lam_ref, lam_init)
    o = acc_s[...] / l_s[...]
    y = _rms(o[0:H] - lam * o[H:2 * H], onorm_ref[...]) * (1.0 - lam_init)
    return jnp.concatenate([y[h:h + 1, :] for h in range(H)], axis=-1).astype(BF16)


def _paged_attn_kernel(pt_ref, q_ref, kn_ref, vn_ref, lam_ref, onorm_ref, *refs,
                       lam_init, n_pg):
    k_refs = refs[:n_pg]
    v_refs = refs[n_pg:2 * n_pg]
    o_ref = refs[2 * n_pg]
    state = refs[2 * n_pg + 1:]
    j = pl.program_id(1)

    @pl.when(j == 0)
    def _():
        _decode_seed(q_ref[0], kn_ref[0], vn_ref[0], state)

    _decode_update(_decode_scores([r[0] for r in k_refs], state), [r[0] for r in v_refs], state)

    @pl.when(j == pl.num_programs(1) - 1)
    def _():
        o_ref[0] = _decode_finish(lam_ref, onorm_ref, lam_init, state)


def _paged_attn(page_table, qd, k_new, v_new, da_lambda, onorm, cache_k, cache_v, lam_init, n_pg):
    T, n_pages = page_table.shape
    n_phys, rows, dk = cache_k.shape
    W = qd.shape[-1]
    row = pl.BlockSpec((1, 1, W), lambda b, j, pt: (b, 0, 0))

    def page_spec(p):
        return pl.BlockSpec((1, rows, dk), lambda b, j, pt: (pt[b, j * n_pg + p], 0, 0))

    const = lambda shape: pl.BlockSpec(shape, lambda b, j, pt: (0,) * len(shape))
    grid_spec = pltpu.PrefetchScalarGridSpec(
        num_scalar_prefetch=1,
        grid=(T, n_pages // n_pg),
        in_specs=[row, row, row, const(da_lambda.shape), const((1, LANES))]
        + [page_spec(p) for p in range(n_pg)] * 2,
        out_specs=row,
        scratch_shapes=[pltpu.VMEM((2 * DA_HEADS, LANES), BF16),
                        pltpu.VMEM((2 * DA_HEADS, 1), F32),
                        pltpu.VMEM((2 * DA_HEADS, 1), F32),
                        pltpu.VMEM((2 * DA_HEADS, LANES), F32)],
    )
    out = pl.pallas_call(
        functools.partial(_paged_attn_kernel, lam_init=lam_init, n_pg=n_pg),
        grid_spec=grid_spec,
        out_shape=jax.ShapeDtypeStruct((T, 1, W), BF16),
        compiler_params=_cparams(("arbitrary", "arbitrary"), 48 << 20),
        name="paged_diff_attn",
    )(page_table, qd.reshape(T, 1, W), k_new.reshape(T, 1, W), v_new.reshape(T, 1, W),
      da_lambda, onorm, *([cache_k] * n_pg), *([cache_v] * n_pg))
    return out.reshape(T, W)


def _page_ring(i, n_steps, pt_ref, ck_hbm, cv_hbm, kbuf, vbuf, sem, state, *,
               n_pg, sps, g_lo, g_hi, begin, end):
    n_slot = kbuf.shape[0]
    rows = kbuf.shape[1] // n_pg
    groups = [(ls, g) for ls in range(sps) for g in range(g_lo, g_hi)]
    n_grp = len(groups)

    def copies(b, g, slot):
        out = []
        for p in range(n_pg):
            page = pt_ref[b, g * n_pg + p]
            dst = pl.ds(p * rows, rows)
            out.append(pltpu.make_async_copy(ck_hbm.at[page], kbuf.at[slot, dst], sem.at[slot, 0]))
            out.append(pltpu.make_async_copy(cv_hbm.at[page], vbuf.at[slot, dst], sem.at[slot, 1]))
        return out

    def start(b, g, slot):
        for cp in copies(b, g, slot):
            cp.start()

    ahead = n_slot - 1

    @pl.when(i == 0)
    def _():
        for n in range(ahead):
            start(groups[n][0], groups[n][1], n % n_slot)

    def fold_group(n):
        ls, g = groups[n]
        b = i * sps + ls
        slot = n % n_slot
        nxt = n + ahead
        if nxt < n_grp:
            start(i * sps + groups[nxt][0], groups[nxt][1], nxt % n_slot)
        else:
            @pl.when(i + 1 < n_steps)
            def _():
                ls2, g2 = groups[nxt - n_grp]
                start((i + 1) * sps + ls2, g2, nxt % n_slot)
        if n in scored:
            s = scored.pop(n)
        else:
            s = arrive_and_score(n)
        if n + 1 < n_grp and groups[n + 1][0] == ls:
            scored[n + 1] = arrive_and_score(n + 1)
        _decode_update(s, [vbuf[slot, p * rows:(p + 1) * rows, :] for p in range(n_pg)], state)
        if g == g_hi - 1:
            end(b)

    scored = {}

    def arrive_and_score(n):
        ls, g = groups[n]
        b = i * sps + ls
        slot = n % n_slot
        for cp in copies(b, g, slot):
            cp.wait()
        if g == g_lo:
            begin(b)
        return _decode_scores([kbuf[slot, p * rows:(p + 1) * rows, :] for p in range(n_pg)], state)

    return fold_group, n_grp


def _ffn_decode_kernel(pt_ref, x_ref, gf_ref, wg_ref, wu_ref, wd_ref, gfin_ref,
                       q_ref, m_ref, l_ref, a_ref, lam_ref, onorm_ref, ck_hbm, cv_hbm,
                       y_ref, o_ref, kbuf, vbuf, sem, q8_s, m_s, l_s, acc_s,
                       *, lam_init, n_pg, sps, g_lo, col_bounds):
    i = pl.program_id(0)
    state = (q8_s, m_s, l_s, acc_s)

    def begin(b):
        _decode_resume(q_ref[b], m_ref[b], l_ref[b], a_ref[b], state)

    def end(b):
        o_ref[b] = _decode_finish(lam_ref, onorm_ref, lam_init, state)

    fold_group, n_grp = _page_ring(i, pl.num_programs(0), pt_ref, ck_hbm, cv_hbm, kbuf, vbuf, sem,
                                   state, n_pg=n_pg, sps=sps, g_lo=g_lo,
                                   g_hi=pt_ref.shape[1] // n_pg, begin=begin, end=end)

    x = x_ref[...]
    h = _rms(x, gf_ref[...]).astype(BF16)
    n_col = len(col_bounds) - 1

    def up(k):
        lo, hi = col_bounds[k], col_bounds[k + 1]
        return _dot(h, wg_ref[:, lo:hi]), _dot(h, wu_ref[:, lo:hi])

    folded = 0
    acc = None
    gu = up(0)
    for k in range(n_col):
        g, u = gu
        part = _dot((g * _sigmoid(g) * u).astype(BF16), wd_ref[col_bounds[k]:col_bounds[k + 1], :])
        if k + 1 < n_col:
            gu = up(k + 1)
        due = (k + 1) * n_grp // n_col
        for n in range(folded, due):
            fold_group(n)
        folded = due
        acc = part if acc is None else acc + part
    y_ref[...] = _rms(x + acc, gfin_ref[...])


def _ring_scratch(n_slot, n_pg, rows, dk):
    return [pltpu.VMEM((n_slot, n_pg * rows, dk), F32),
            pltpu.VMEM((n_slot, n_pg * rows, dk), F32),
            pltpu.SemaphoreType.DMA((n_slot, 2)),
            pltpu.VMEM((2 * DA_HEADS, LANES), BF16),
            pltpu.VMEM((2 * DA_HEADS, 1), F32),
            pltpu.VMEM((2 * DA_HEADS, 1), F32),
            pltpu.VMEM((2 * DA_HEADS, LANES), F32)]


def _ffn_decode(x, gf, wg, wu, wd, gfin, ts, page_table, qd, softmax_state, da_lambda, onorm,
                cache_k, cache_v, lam_init, n_pg, n_slot, g_lo):
    T, D = x.shape
    F = wg.shape[1]
    Ts, n_pages = page_table.shape
    n_phys, rows, dk = cache_k.shape
    W = qd.shape[-1]
    n_tiles = T // ts
    sps = Ts // n_tiles
    mxu_n = 2 * LANES
    col_bounds = tuple(range(0, F, mxu_n)) + (F,)
    tok = pl.BlockSpec((ts, D), lambda i, pt: (i, 0))
    const = lambda shape: pl.BlockSpec(shape, lambda i, pt: (0,) * len(shape),
                                       pipeline_mode=pl.Buffered(1))
    st = const((Ts, 2 * DA_HEADS, LANES))
    hbm = pl.BlockSpec(memory_space=pl.ANY)
    grid_spec = pltpu.PrefetchScalarGridSpec(
        num_scalar_prefetch=1,
        grid=(n_tiles,),
        in_specs=[tok, const((1, D)), const(wg.shape), const(wu.shape), const(wd.shape),
                  const((1, D)), const((Ts, 1, W)), st, st, st, const(da_lambda.shape),
                  const((1, LANES)), hbm, hbm],
        out_specs=[tok, pl.BlockSpec((Ts, 1, W), lambda i, pt: (0, 0, 0))],
        scratch_shapes=_ring_scratch(n_slot, n_pg, rows, dk),
    )
    y, o = pl.pallas_call(
        functools.partial(_ffn_decode_kernel, lam_init=lam_init, n_pg=n_pg, sps=sps, g_lo=g_lo,
                          col_bounds=col_bounds),
        grid_spec=grid_spec,
        out_shape=[jax.ShapeDtypeStruct((T, D), F32), jax.ShapeDtypeStruct((Ts, 1, W), BF16)],
        compiler_params=_cparams(("arbitrary",), VMEM_LIMIT_CAP),
        name="ffn_decode",
    )(page_table, x, gf, wg, wu, wd, gfin, qd.reshape(Ts, 1, W), *softmax_state,
      da_lambda, onorm, cache_k, cache_v)
    return y, o.reshape(Ts, W)


def _rope_tables(pos, dh, n_maps):
    inv = ROPE_THETA ** (-jnp.arange(0, dh, 2, dtype=F32) / dh)
    ang = pos.astype(F32)[:, None] * inv[None, :]
    cos = jnp.cos(ang)
    sin = jnp.sin(ang)
    cos = jnp.tile(jnp.concatenate([cos, cos], axis=-1), (1, n_maps))
    sin = jnp.tile(jnp.concatenate([-sin, sin], axis=-1), (1, n_maps))
    return cos, sin


def _pick_tile(n, pref):
    t = min(n, pref)
    while n % t:
        t //= 2
    return t


def kernel(x_prompt, x_sample, mem_prompt, cache_k, cache_v, cache_mem_k, cache_mem_v, state_hgrn, page_table, norm_mix, w_in, hg_lb, hg_onorm, da_lambda, da_onorm, w_out, norm_mem_q, norm_mem_kv, w_mq, w_mk, w_mv, w_mo, norm_ffn, w_gate, w_up, w_down, norm_final):
    B, S, D = x_prompt.shape
    T = x_sample.shape[0]
    depth = w_in.shape[0]
    assert depth == 1 and x_sample.shape[1] == 1
    l = 0
    lam_init = 0.8 - 0.6 * math.exp(-0.3 * l)
    n_phys, page = cache_k.shape[1], cache_k.shape[2]
    past_len = page_table.shape[1] * page
    W = w_in.shape[2] // 7

    bf = lambda w: w.astype(BF16)
    w_in_b, w_out_b = bf(w_in[l]), bf(w_out[l])
    w_mq_b, w_mk_b, w_mv_b, w_mo_b = bf(w_mq[l]), bf(w_mk[l]), bf(w_mv[l]), bf(w_mo[l])
    w_gate_b, w_up_b, w_down_b = bf(w_gate[l]), bf(w_up[l]), bf(w_down[l])
    row = lambda g: g.reshape(1, -1)
    lb2 = hg_lb[l:l + 2]
    lam_p = da_lambda[l]

    cos_p, sin_p = _rope_tables(jnp.arange(S), LANES // 2, 2 * DA_HEADS)
    cos_s, sin_s = _rope_tables(past_len + jnp.arange(1), LANES // 2, 2 * DA_HEADS)
    cos_s = jnp.broadcast_to(cos_s, (T, W))
    sin_s = jnp.broadcast_to(sin_s, (T, W))

    mk, mv, mkb, mvb = _mem_kv(mem_prompt, row(norm_mem_kv[l]), w_mk_b, w_mv_b)
    qd, k_p, v_p, kb, v_att, mix_hg, hs_p = _prompt_mix(
        x_prompt, row(norm_mix[l]), w_in_b, lb2, cos_p, sin_p, row(hg_onorm[l]),
        ts=_pick_tile(S, 256))
    mix_da = _diff_attn_prompt(qd, kb, v_att, lam_p, row(da_onorm[l]), lam_init,
                               tq=_pick_tile(S, 1024), tk=_pick_tile(S, 256))

    xs = x_sample.reshape(T, D)
    qd_s, k_s, v_s, hq_s, kk_s, f_s, vh_s, gate_s = _sample_mix(
        xs, row(norm_mix[l]), w_in_b, lb2, cos_s, sin_s)
    mix_hg_s, hs_s = _sample_hgrn(hq_s, kk_s, f_s, vh_s, gate_s, row(hg_onorm[l]), state_hgrn[l])
    ck = cache_k.reshape(depth * n_phys, page * DA_HEADS, LANES)
    cv = cache_v.reshape(depth * n_phys, page * DA_HEADS, LANES)
    n_pages = page_table.shape[1]
    tok_ts = _pick_tile(S, 512)
    n_tiles = (B * S) // tok_ts
    n_pg = _pick_tile(n_pages, 8)
    n_slot = 4
    gps = n_pages // n_pg
    sps = T // n_tiles if T % n_tiles == 0 else 0
    share = 3 * D / (3 * D + 3 * w_gate.shape[2])
    splits = [g for g in range(1, gps)
              if (sps * g) % n_slot == 0 and (sps * (gps - g)) % n_slot == 0]
    oc_args = (x_prompt, mix_hg, mix_da, w_out_b, row(norm_mem_q[l]), w_mq_b, mkb, mvb, w_mo_b)
    ffn_args = (row(norm_ffn[l]), w_gate_b, w_up_b, w_down_b, row(norm_final))
    if sps and splits:
        g_split = min(splits, key=lambda g: abs(g - share * gps))
        x2, softmax_state = _out_cross_decode(*oc_args, tok_ts, page_table, qd_s, k_s, v_s, ck, cv,
                                              n_pg, n_slot, g_split)
        y_p, mix_da_s = _ffn_decode(x2.reshape(B * S, D), *ffn_args, tok_ts, page_table, qd_s,
                                    softmax_state, lam_p, row(da_onorm[l]), ck, cv, lam_init,
                                    n_pg, n_slot, g_split)
    else:
        x2 = _out_cross_prompt(*oc_args, ts=tok_ts)
        y_p = _ffn(x2.reshape(B * S, D), *ffn_args, ts=tok_ts)
        mix_da_s = _paged_attn(page_table, qd_s, k_s, v_s, lam_p, row(da_onorm[l]), ck, cv,
                               lam_init, n_pg=_pick_tile(n_pages, 16))
    y_p = y_p.reshape(B, S, D)
    MW = w_mk.shape[2]
    x2_s = _out_cross_sample(xs, mix_hg_s, mix_da_s, w_out_b, row(norm_mem_q[l]), w_mq_b,
                             cache_mem_k[l], cache_mem_v[l],
                             w_mo_b)
    y_s = _ffn(x2_s, row(norm_ffn[l]), w_gate_b, w_up_b, w_down_b, row(norm_final), ts=T)

    dk = LANES
    return (y_p, y_s.reshape(T, 1, D),
            hs_p[None],
            k_p.reshape(1, B, S, DA_HEADS, dk), v_p.reshape(1, B, S, DA_HEADS, dk),
            mk[None], mv[None],
            hs_s[None],
            k_s.reshape(1, T, 1, DA_HEADS, dk), v_s.reshape(1, T, 1, DA_HEADS, dk))
```

```python
import functools
import math

import jax
import jax.numpy as jnp
from jax import lax
from jax.experimental import pallas as pl
from jax.experimental.pallas import tpu as pltpu

F32 = jnp.float32
BF16 = jnp.bfloat16
EPS = 1e-6
ROPE_THETA = 10000.0

HG_HEADS = 4
DA_HEADS = 4
MEM_HEADS = 4
HG_CHUNK = 64
LANES = 128
BF16_ROWS = 16
VMEM_LIMIT_CAP = 56 << 20

_NT = (((1,), (1,)), ((), ()))
_TN = (((0,), (0,)), ((), ()))


def _dot(a, b):
    return jnp.dot(a, b, preferred_element_type=F32)


def _dot_nt(a, b):
    return lax.dot_general(a, b, _NT, preferred_element_type=F32)


def _dot_tn(a, b):
    return lax.dot_general(a, b, _TN, preferred_element_type=F32)


def _rms(x, g):
    ms = jnp.mean(x * x, axis=-1, keepdims=True)
    return x * lax.rsqrt(ms + EPS) * g


def _sigmoid(x):
    return 1.0 / (1.0 + jnp.exp(-x))


def _cparams(semantics, vmem_bytes):
    return pltpu.CompilerParams(
        dimension_semantics=semantics,
        vmem_limit_bytes=int(min(max(vmem_bytes, 16 << 20), VMEM_LIMIT_CAP)))


def _const_spec(shape):
    nd = len(shape)
    return pl.BlockSpec(shape, lambda *_: (0,) * nd, pipeline_mode=pl.Buffered(1))


def _lower_bound(lb_ref):
    a0 = lb_ref[0:1, :]
    a1 = lb_ref[1:2, :]
    m = jnp.maximum(a0, a1)
    e0 = jnp.exp(a0 - m)
    e1 = jnp.exp(a1 - m)
    return e0 / (e0 + e1)


def _rope(x, cos, sin_signed):
    n = x.shape[-1]
    lane = lax.broadcasted_iota(jnp.int32, x.shape, x.ndim - 1)
    swapped = jnp.where((lane & 63) < 32,
                        pltpu.roll(x, n - 32, x.ndim - 1),
                        pltpu.roll(x, 32, x.ndim - 1))
    return x * cos + swapped * sin_signed


def _lambda(lam_ref, lam_init):
    lp = lam_ref[...]
    s01 = jnp.sum(lp[0:1, :] * lp[1:2, :], axis=-1, keepdims=True)
    s23 = jnp.sum(lp[2:3, :] * lp[3:4, :], axis=-1, keepdims=True)
    return jnp.exp(s01) - jnp.exp(s23) + lam_init


def _mixer_sections(z, lb):
    w = z.shape[-1] // 7
    hq, zf, hi, hg, dq, dk, dv = (z[:, i * w:(i + 1) * w] for i in range(7))
    sig = _sigmoid(zf)
    logf = jnp.log(lb + (1.0 - lb) * sig)
    k_hg = (1.0 - lb) * (1.0 - sig)
    gate = hg * _sigmoid(hg)
    return hq, k_hg, logf, hi, gate, dq, dk, dv


def _head_rms(o, g):
    outs = []
    for h in range(o.shape[-1] // LANES):
        oh = o[:, h * LANES:(h + 1) * LANES]
        outs.append(_rms(oh, g))
    return jnp.concatenate(outs, axis=-1)


def _memkv_kernel(mem_ref, g_ref, wk_ref, wv_ref, k_ref, v_ref, kb_ref, vb_ref):
    m = _rms(mem_ref[0], g_ref[...]).astype(BF16)
    k = _dot(m, wk_ref[...])
    v = _dot(m, wv_ref[...])
    dh = k_ref.shape[-1]
    for h in range(MEM_HEADS):
        k_ref[0, :, h, :] = k[:, h * dh:(h + 1) * dh]
        v_ref[0, :, h, :] = v[:, h * dh:(h + 1) * dh]
    kb_ref[0] = k.astype(BF16)
    vb_ref[0] = v.astype(BF16)


def _mem_kv(mem, g, wk, wv):
    B, N, D = mem.shape
    W = wk.shape[1]
    dh = W // MEM_HEADS
    blk = lambda d: pl.BlockSpec((1, N, d), lambda b: (b, 0, 0))
    blk4 = pl.BlockSpec((1, N, MEM_HEADS, dh), lambda b: (b, 0, 0, 0))
    return pl.pallas_call(
        _memkv_kernel,
        grid=(B,),
        in_specs=[blk(D), _const_spec((1, D)), _const_spec((D, W)), _const_spec((D, W))],
        out_specs=[blk4, blk4, blk(W), blk(W)],
        out_shape=[jax.ShapeDtypeStruct((B, N, MEM_HEADS, dh), F32)] * 2
        + [jax.ShapeDtypeStruct((B, N, W), BF16)] * 2,
        compiler_params=_cparams(("arbitrary",), 40 << 20),
        name="mem_kv",
    )(mem, g, wk, wv)


def _hgrn_levels(q, kk, lf, v):
    C, W = q.shape
    t = lax.broadcasted_iota(jnp.int32, (C, W), 0)
    n_lvl = int(math.log2(C))

    c = lf
    lvls = []
    for l in range(n_lvl):
        m = 1 << l
        upper = (t & m) != 0
        if m < 8:
            y = c
            for i in range(l):
                y = jnp.where((t & (1 << i)) == 0, pltpu.roll(y, C - (1 << i), 0), y)
            bc = jnp.where(upper, pltpu.roll(y, m, 0), y)
        else:
            pieces = []
            for j in range(C // (2 * m)):
                r = 2 * m * j + m - 1
                pieces.append(jnp.broadcast_to(c[r:r + 1, :], (2 * m, W)))
            bc = pieces[0] if len(pieces) == 1 else jnp.concatenate(pieces, axis=0)
        e = jnp.exp(jnp.where(upper, c, bc - c))
        lvls.append((jnp.where(upper, q, kk) * e).astype(BF16))
        c = c + jnp.where(upper, bc, 0.0)
    b = c
    b_last = b[C - 1:C, :]
    return dict(lvls=lvls, q=q.astype(BF16), k=kk.astype(BF16), v=v.astype(BF16),
                q_state=(q * jnp.exp(b)).astype(BF16),
                k_state=(kk * jnp.exp(b_last - b)).astype(BF16),
                decay=jnp.exp(b_last))


def _hgrn_products(lv, st_ref):
    out = []
    for h in range(lv["q"].shape[1] // LANES):
        sl = slice(h * LANES, (h + 1) * LANES)
        pairs = [_dot_nt(lv["q"][:, sl], lv["k"][:, sl])]
        pairs += [_dot_nt(r[:, sl], r[:, sl]) for r in lv["lvls"]]
        o_state = _dot_nt(lv["q_state"][:, sl], st_ref[h].astype(BF16))
        st_term = _dot_tn(lv["v"][:, sl], lv["k_state"][:, sl])
        out.append((pairs, o_state, st_term))
    return out


def _hgrn_finish(lv, prods, st_ref):
    C = lv["q"].shape[0]
    ti = lax.broadcasted_iota(jnp.int32, (C, C), 0)
    si = lax.broadcasted_iota(jnp.int32, (C, C), 1)
    diff_bits = jnp.where(ti > si, ti ^ si, 0)
    outs = []
    for h, (pairs, o_state, st_term) in enumerate(prods):
        sl = slice(h * LANES, (h + 1) * LANES)
        a = jnp.where(ti == si, pairs[0], 0.0)
        for l, pr in enumerate(pairs[1:]):
            a = a + jnp.where((diff_bits >> l) == 1, pr, 0.0)
        outs.append(_dot(a.astype(BF16), lv["v"][:, sl]) + o_state)
        st_ref[h] = st_ref[h] * lv["decay"][:, sl] + st_term
    return jnp.concatenate(outs, axis=-1)


def _prompt_mix_kernel(x_ref, g_ref, w_ref, lb_ref, cos_ref, sin_ref, onorm_ref,
                       qd_ref, k_ref, v_ref, kb_ref, va_ref, mix_ref, state_ref,
                       o_s, st_s):
    s = pl.program_id(1)
    last = pl.num_programs(1) - 1

    @pl.when(s == 0)
    def _():
        st_s[...] = jnp.zeros_like(st_s)

    h = _rms(x_ref[0], g_ref[...]).astype(BF16)

    ts = x_ref.shape[1]
    sec_w = w_ref.shape[1] // 7
    n_chunk = ts // HG_CHUNK
    stage = {"done": 0, "lv": None}

    def project(sec):
        return _dot(h, w_ref[:, sec * sec_w:(sec + 1) * sec_w])

    def levels(ci):
        rows = slice(ci * HG_CHUNK, (ci + 1) * HG_CHUNK)
        return _hgrn_levels(hq[rows, :], k_hg[rows, :], logf[rows, :], hi[rows, :])

    def recur(upto):
        for ci in range(stage["done"], min(upto, n_chunk)):
            lv = stage["lv"] if stage["lv"] is not None else levels(ci)
            prods = _hgrn_products(lv, st_s)
            stage["lv"] = levels(ci + 1) if ci + 1 < n_chunk else None
            o_s[ci * HG_CHUNK:(ci + 1) * HG_CHUNK, :] = _hgrn_finish(lv, prods, st_s)
        stage["done"] = max(stage["done"], min(upto, n_chunk))

    def head_rows_out(ref, val):
        for hh in range(DA_HEADS):
            ref[0, pl.ds(hh, ts, stride=DA_HEADS), :] = val[:, hh * LANES:(hh + 1) * LANES]

    lb = _lower_bound(lb_ref)
    cos = cos_ref[...]
    sin = sin_ref[...]
    hq = project(0)
    sig = _sigmoid(project(1))
    logf = jnp.log(lb + (1.0 - lb) * sig)
    k_hg = (1.0 - lb) * (1.0 - sig)
    hi = project(2)
    hg = project(3)
    recur(n_chunk // 4)
    q_da = _rope(project(4), cos, sin)
    qd_ref[0] = (q_da * (math.log2(math.e) * float(LANES // 2) ** -0.5)).astype(BF16)
    recur(n_chunk // 2)
    k_da = _rope(project(5), cos, sin)
    head_rows_out(k_ref, k_da)
    kb_ref[0] = k_da.astype(BF16)
    recur(3 * n_chunk // 4)
    dv = project(6)
    head_rows_out(v_ref, dv)
    va_ref[0] = dv
    recur(n_chunk)
    gate = hg * _sigmoid(hg)
    mix_ref[0] = (_head_rms(o_s[...], onorm_ref[...]) * gate).astype(BF16)

    @pl.when(s == last)
    def _():
        for hh in range(HG_HEADS):
            state_ref[0, hh] = st_s[hh].T


def _prompt_mix(x, g, w_in, hg_lb, cos, sin, onorm, ts):
    B, S, D = x.shape
    W = w_in.shape[1] // 7
    tok = lambda d: pl.BlockSpec((1, ts, d), lambda b, s: (b, s, 0))
    tab = pl.BlockSpec((ts, W), lambda b, s: (s, 0))
    rows = pl.BlockSpec((1, ts * DA_HEADS, LANES), lambda b, s: (b, s, 0))
    state = pl.BlockSpec((1, HG_HEADS, LANES, LANES), lambda b, s: (b, 0, 0, 0))
    sds = jax.ShapeDtypeStruct
    return pl.pallas_call(
        _prompt_mix_kernel,
        grid=(B, S // ts),
        in_specs=[tok(D), _const_spec((1, D)), _const_spec(w_in.shape),
                  _const_spec(hg_lb.shape), tab, tab, _const_spec((1, LANES))],
        out_specs=[tok(W), rows, rows, tok(W), tok(W), tok(W), state],
        out_shape=[sds((B, S, W), BF16), sds((B, S * DA_HEADS, LANES), F32),
                   sds((B, S * DA_HEADS, LANES), F32), sds((B, S, W), BF16),
                   sds((B, S, W), F32), sds((B, S, W), BF16),
                   sds((B, HG_HEADS, LANES, LANES), F32)],
        scratch_shapes=[pltpu.VMEM((ts, W), F32), pltpu.VMEM((HG_HEADS, LANES, LANES), F32)],
        compiler_params=_cparams(("arbitrary", "arbitrary"), 48 << 20),
        name="prompt_mix",
    )(x, g, w_in, hg_lb, cos, sin, onorm)


def _diff_attn_kernel(q_ref, k_ref, v_ref, lam_ref, onorm_ref, o_ref,
                      qq_s, vt_s, m_s, acc_s, *, lam_init, tk):
    i = pl.program_id(2)
    tq = q_ref.shape[1]
    n_ck = tq // tk
    n_kv = v_ref.shape[1] // tk

    @pl.when(i == 0)
    def _():
        ones = jnp.ones((vt_s.shape[1] - LANES, tk), BF16)
        for jj in range(n_kv):
            vt_s[jj, 0:LANES, :] = v_ref[0, jj * tk:(jj + 1) * tk, :].T.astype(BF16)
            vt_s[jj, LANES:, :] = ones

    q = q_ref[0]
    lane = lax.broadcasted_iota(jnp.int32, q.shape, 1)
    zero = jnp.zeros_like(q)
    qq_s[0:tq, :] = jnp.where(lane < LANES // 2, q, zero)
    qq_s[tq:2 * tq, :] = jnp.where(lane >= LANES // 2, q, zero)
    m_s[...] = jnp.full(m_s.shape, -jnp.inf, F32)
    acc_s[...] = jnp.zeros(acc_s.shape, F32)

    def scores(j, c):
        k = k_ref[0, pl.ds(pl.multiple_of(j * tk, tk), tk), :]
        return _dot_nt(k, qq_s[c * tk:(c + 1) * tk, :])

    def softmax(j, c, tri, s):
        cols = slice(c * tk, (c + 1) * tk)
        if tri:
            key = lax.broadcasted_iota(jnp.int32, s.shape, 0)
            qry = lax.broadcasted_iota(jnp.int32, s.shape, 1)
            s = jnp.where(key <= qry, s, -jnp.inf)
        m_old = m_s[:, cols]
        m_new = jnp.maximum(m_old, jnp.max(s, axis=0, keepdims=True))
        alpha = jnp.exp2(m_old - m_new)
        p = jnp.exp2(s - m_new)
        m_s[:, cols] = m_new
        return cols, alpha, _dot(vt_s[j], p.astype(BF16))

    def accumulate(cols, alpha, pv):
        acc_s[:, cols] = alpha * acc_s[:, cols] + pv

    ahead = 4

    def run(work):
        s = {n: scores(*work[n][:2]) for n in range(min(ahead, len(work)))}
        pending = None
        for n, (j, c, tri) in enumerate(work):
            if n + ahead < len(work):
                s[n + ahead] = scores(*work[n + ahead][:2])
            done = softmax(j, c, tri, s.pop(n))
            if pending is not None:
                accumulate(*pending)
            pending = done
        accumulate(*pending)

    per_iter = 2 if n_ck % 2 == 0 else 1

    def full_blocks(jj, carry):
        run([(jj * per_iter + r, c, False) for r in range(per_iter) for c in range(2 * n_ck)])
        return carry

    lax.fori_loop(0, (i * n_ck) // per_iter, full_blocks, 0)
    run([(i * n_ck + d, mp * n_ck + cq, cq == d)
         for d in range(n_ck) for mp in range(2) for cq in range(d, n_ck)])

    lam = _lambda(lam_ref, lam_init)
    a = acc_s[0:LANES, :] * (1.0 / acc_s[LANES:LANES + 1, :])
    o_t = a[:, 0:tq] - lam * a[:, tq:2 * tq]
    ms = jnp.mean(o_t * o_t, axis=0, keepdims=True)
    y = (o_t * lax.rsqrt(ms + EPS)).T
    o_ref[0] = (y * onorm_ref[...] * (1.0 - lam_init)).astype(BF16)


def _diff_attn_prompt(qd, kb, v, da_lambda, onorm, lam_init, tq, tk):
    B, S, W = qd.shape
    H = W // LANES
    qspec = pl.BlockSpec((1, tq, LANES), lambda b, h, i: (b, i, h))
    kvspec = pl.BlockSpec((1, S, LANES), lambda b, h, i: (b, 0, h))
    return pl.pallas_call(
        functools.partial(_diff_attn_kernel, lam_init=lam_init, tk=tk),
        grid=(B, H, S // tq),
        in_specs=[qspec, kvspec, kvspec, _const_spec(da_lambda.shape), _const_spec((1, LANES))],
        out_specs=qspec,
        out_shape=jax.ShapeDtypeStruct((B, S, W), BF16),
        scratch_shapes=[pltpu.VMEM((2 * tq, LANES), BF16),
                        pltpu.VMEM((S // tk, LANES + BF16_ROWS, tk), BF16),
                        pltpu.VMEM((1, 2 * tq), F32),
                        pltpu.VMEM((LANES + BF16_ROWS, 2 * tq), F32)],
        compiler_params=_cparams(("arbitrary",) * 3, 32 << 20),
        name="diff_attn_prompt",
    )(qd, kb, v, da_lambda, onorm)


def _cross_attn_heads(q, mk_head, mv_head):
    dh = q.shape[-1] // MEM_HEADS
    scores = [_dot_nt(q[:, h * dh:(h + 1) * dh], mk_head(h)) for h in range(MEM_HEADS)]
    probs = []
    for s in scores:
        m = jnp.max(s, axis=-1, keepdims=True)
        p = jnp.exp(s - m)
        l = jnp.sum(p, axis=-1, keepdims=True)
        probs.append((p / l).astype(BF16))
    return jnp.concatenate([_dot(probs[h], mv_head(h)) for h in range(MEM_HEADS)], axis=-1)


def _out_cross_kernel(x_ref, mhg_ref, mda_ref, wo_ref, gq_ref, wq_ref, mk_ref, mv_ref,
                      wmo_ref, x2_ref):
    w = mhg_ref.shape[-1]
    dh = wq_ref.shape[1] // MEM_HEADS
    ts = x_ref.shape[1]
    halves = [slice(0, ts // 2), slice(ts // 2, ts)] if ts % 16 == 0 else [slice(0, ts)]
    x1 = [x_ref[0, r, :] + _dot(mhg_ref[0, r, :], wo_ref[0:w, :])
          + _dot(mda_ref[0, r, :], wo_ref[w:2 * w, :]) for r in halves]
    q = [(_dot(_rms(v, gq_ref[...]).astype(BF16), wq_ref[...]) * (float(dh) ** -0.5)).astype(BF16)
         for v in x1]
    mk = lambda h: mk_ref[0, :, h * dh:(h + 1) * dh]
    mv = lambda h: mv_ref[0, :, h * dh:(h + 1) * dh]
    o = [_cross_attn_heads(v, mk, mv) for v in q]
    for r, v1, vo in zip(halves, x1, o):
        x2_ref[0, r, :] = v1 + _dot(vo.astype(BF16), wmo_ref[...])


def _out_cross_prompt(x, mhg, mda, w_out, gq, w_mq, mkb, mvb, w_mo, ts):
    B, S, D = x.shape
    W = mhg.shape[-1]
    N, MW = mkb.shape[1], mkb.shape[2]
    tok = lambda d: pl.BlockSpec((1, ts, d), lambda b, s: (b, s, 0))
    mem = pl.BlockSpec((1, N, MW), lambda b, s: (b, 0, 0))
    return pl.pallas_call(
        _out_cross_kernel,
        grid=(B, S // ts),
        in_specs=[tok(D), tok(W), tok(W), _const_spec(w_out.shape), _const_spec((1, D)),
                  _const_spec(w_mq.shape), mem, mem, _const_spec(w_mo.shape)],
        out_specs=tok(D),
        out_shape=jax.ShapeDtypeStruct((B, S, D), F32),
        compiler_params=_cparams(("arbitrary", "arbitrary"), 48 << 20),
        name="out_cross_prompt",
    )(x, mhg, mda, w_out, gq, w_mq, mkb, mvb, w_mo)


def _out_cross_decode_kernel(pt_ref, x_ref, mhg_ref, mda_ref, wo_ref, gq_ref, wq_ref, mk_ref, mv_ref,
                             wmo_ref, q_ref, kn_ref, vn_ref, ck_hbm, cv_hbm,
                             x2_ref, m_ref, l_ref, a_ref, kbuf, vbuf, sem, q8_s, m_s, l_s, acc_s,
                             *, n_pg, sps, g_hi):
    i = pl.program_id(0) * pl.num_programs(1) + pl.program_id(1)
    n_steps = pl.num_programs(0) * pl.num_programs(1)
    state = (q8_s, m_s, l_s, acc_s)

    def begin(b):
        _decode_seed(q_ref[b], kn_ref[b], vn_ref[b], state)

    def end(b):
        m_ref[b], l_ref[b], a_ref[b] = _decode_save(state)

    fold_group, n_grp = _page_ring(i, n_steps, pt_ref, ck_hbm, cv_hbm, kbuf, vbuf, sem, state,
                                   n_pg=n_pg, sps=sps, g_lo=0, g_hi=g_hi, begin=begin, end=end)
    folded = [0]

    def fold_until(frac):
        due = int(round(frac * n_grp))
        for n in range(folded[0], due):
            fold_group(n)
        folded[0] = max(folded[0], due)

    w = mhg_ref.shape[-1]
    dh = wq_ref.shape[1] // MEM_HEADS
    ts = x_ref.shape[1]
    halves = [slice(0, ts // 2), slice(ts // 2, ts)] if ts % 16 == 0 else [slice(0, ts)]
    x1 = [x_ref[0, r, :] + _dot(mhg_ref[0, r, :], wo_ref[0:w, :])
          + _dot(mda_ref[0, r, :], wo_ref[w:2 * w, :]) for r in halves]
    fold_until(0.25)
    q = [(_dot(_rms(v, gq_ref[...]).astype(BF16), wq_ref[...]) * (float(dh) ** -0.5)).astype(BF16)
         for v in x1]
    fold_until(0.5)
    mk = lambda h: mk_ref[0, :, h * dh:(h + 1) * dh]
    mv = lambda h: mv_ref[0, :, h * dh:(h + 1) * dh]
    o = [_cross_attn_heads(v, mk, mv) for v in q]
    fold_until(0.75)
    out = [v1 + _dot(vo.astype(BF16), wmo_ref[...]) for v1, vo in zip(x1, o)]
    fold_until(1.0)
    for r, v in zip(halves, out):
        x2_ref[0, r, :] = v


def _out_cross_decode(x, mhg, mda, w_out, gq, w_mq, mkb, mvb, w_mo, ts, page_table, qd, k_new, v_new,
                      cache_k, cache_v, n_pg, n_slot, g_hi):
    B, S, D = x.shape
    W = mhg.shape[-1]
    N, MW = mkb.shape[1], mkb.shape[2]
    Ts = page_table.shape[0]
    n_phys, rows, dk = cache_k.shape
    Wq = qd.shape[-1]
    sps = Ts // (B * (S // ts))
    tok = lambda d: pl.BlockSpec((1, ts, d), lambda b, s, pt: (b, s, 0))
    mem = pl.BlockSpec((1, N, MW), lambda b, s, pt: (b, 0, 0))
    const = lambda shape: pl.BlockSpec(shape, lambda b, s, pt: (0,) * len(shape),
                                       pipeline_mode=pl.Buffered(1))
    sample = const((Ts, 1, Wq))
    st = pl.BlockSpec((Ts, 2 * DA_HEADS, LANES), lambda b, s, pt: (0, 0, 0))
    hbm = pl.BlockSpec(memory_space=pl.ANY)
    grid_spec = pltpu.PrefetchScalarGridSpec(
        num_scalar_prefetch=1,
        grid=(B, S // ts),
        in_specs=[tok(D), tok(W), tok(W), const(w_out.shape), const((1, D)), const(w_mq.shape),
                  mem, mem, const(w_mo.shape), sample, sample, sample, hbm, hbm],
        out_specs=[tok(D), st, st, st],
        scratch_shapes=_ring_scratch(n_slot, n_pg, rows, dk),
    )
    st_shape = jax.ShapeDtypeStruct((Ts, 2 * DA_HEADS, LANES), F32)
    x2, m, l, a = pl.pallas_call(
        functools.partial(_out_cross_decode_kernel, n_pg=n_pg, sps=sps, g_hi=g_hi),
        grid_spec=grid_spec,
        out_shape=[jax.ShapeDtypeStruct((B, S, D), F32), st_shape, st_shape, st_shape],
        compiler_params=_cparams(("arbitrary", "arbitrary"), VMEM_LIMIT_CAP),
        name="out_cross_decode",
    )(page_table, x, mhg, mda, w_out, gq, w_mq, mkb, mvb, w_mo, qd.reshape(Ts, 1, Wq),
      k_new.reshape(Ts, 1, Wq), v_new.reshape(Ts, 1, Wq), cache_k, cache_v)
    return x2, (m, l, a)


def _out_cross_sample_kernel(x_ref, mhg_ref, mda_ref, wo_ref, gq_ref, wq_ref, mk_hbm, mv_hbm,
                             wmo_ref, x2_ref, x1_s, q_s, o_s, kbuf, vbuf, sem):
    b = pl.program_id(0)
    n_b = pl.num_programs(0)
    w = mhg_ref.shape[-1]

    def copies(bb, slot):
        out = []
        for h in range(MEM_HEADS):
            out.append(pltpu.make_async_copy(mk_hbm.at[bb, :, h, :], kbuf.at[slot, h], sem.at[slot, 0]))
            out.append(pltpu.make_async_copy(mv_hbm.at[bb, :, h, :], vbuf.at[slot, h], sem.at[slot, 1]))
        return out

    @pl.when(b == 0)
    def _():
        for cp in copies(0, 0):
            cp.start()
        x1 = x_ref[...] + _dot(mhg_ref[...], wo_ref[0:w, :]) + _dot(mda_ref[...], wo_ref[w:2 * w, :])
        x1_s[...] = x1
        hq = _rms(x1, gq_ref[...]).astype(BF16)
        dh = wq_ref.shape[1] // MEM_HEADS
        q_s[...] = _dot(hq, wq_ref[...]) * (float(dh) ** -0.5)

    slot = b % 2

    @pl.when(b + 1 < n_b)
    def _():
        for cp in copies(b + 1, 1 - slot):
            cp.start()

    for cp in copies(b, slot):
        cp.wait()
    q = q_s[pl.ds(b, 1), :].astype(BF16)
    o_s[pl.ds(b, 1), :] = _cross_attn_heads(q, lambda h: kbuf[slot, h].astype(BF16),
                                            lambda h: vbuf[slot, h].astype(BF16))

    @pl.when(b == n_b - 1)
    def _():
        x2_ref[...] = x1_s[...] + _dot(o_s[...].astype(BF16), wmo_ref[...])


def _out_cross_sample(x, mhg, mda, w_out, gq, w_mq, mem_k, mem_v, w_mo):
    T, D = x.shape
    _, N, heads, dh = mem_k.shape
    MW = heads * dh
    hbm = pl.BlockSpec(memory_space=pl.ANY)
    return pl.pallas_call(
        _out_cross_sample_kernel,
        grid=(T,),
        in_specs=[_const_spec(x.shape), _const_spec(mhg.shape), _const_spec(mda.shape),
                  _const_spec(w_out.shape), _const_spec((1, D)), _const_spec(w_mq.shape),
                  hbm, hbm, _const_spec(w_mo.shape)],
        out_specs=pl.BlockSpec((T, D), lambda b: (0, 0)),
        out_shape=jax.ShapeDtypeStruct((T, D), F32),
        scratch_shapes=[pltpu.VMEM((T, D), F32), pltpu.VMEM((T, MW), F32), pltpu.VMEM((T, MW), F32),
                        pltpu.VMEM((2, heads, N, dh), F32), pltpu.VMEM((2, heads, N, dh), F32),
                        pltpu.SemaphoreType.DMA((2, 2))],
        compiler_params=_cparams(("arbitrary",), 32 << 20),
        name="out_cross_sample",
    )(x, mhg, mda, w_out, gq, w_mq, mem_k, mem_v, w_mo)


def _ffn_kernel(x_ref, gf_ref, wg_ref, wu_ref, wd_ref, gfin_ref, y_ref):
    x = x_ref[...]
    h = _rms(x, gf_ref[...]).astype(BF16)
    g = _dot(h, wg_ref[...])
    u = _dot(h, wu_ref[...])
    a = (g * _sigmoid(g) * u).astype(BF16)
    x3 = x + _dot(a, wd_ref[...])
    y_ref[...] = _rms(x3, gfin_ref[...])


def _ffn(x, gf, wg, wu, wd, gfin, ts):
    T, D = x.shape
    tok = pl.BlockSpec((ts, D), lambda i: (i, 0))
    return pl.pallas_call(
        _ffn_kernel,
        grid=(T // ts,),
        in_specs=[tok, _const_spec((1, D)), _const_spec(wg.shape), _const_spec(wu.shape),
                  _const_spec(wd.shape), _const_spec((1, D))],
        out_specs=tok,
        out_shape=jax.ShapeDtypeStruct((T, D), F32),
        compiler_params=_cparams(("arbitrary",), VMEM_LIMIT_CAP),
        name="ffn",
    )(x, gf, wg, wu, wd, gfin)


def _sample_mix_kernel(x_ref, g_ref, w_ref, lb_ref, cos_ref, sin_ref,
                       qd_ref, k_ref, v_ref, hq_ref, kk_ref, f_ref, vh_ref, gate_ref):
    h = _rms(x_ref[...], g_ref[...]).astype(BF16)
    z = _dot(h, w_ref[...])
    lb = _lower_bound(lb_ref)
    hq, k_hg, logf, hi, gate, dq, dk, dv = _mixer_sections(z, lb)
    cos = cos_ref[...]
    sin = sin_ref[...]
    qd_ref[...] = _rope(dq, cos, sin) * (float(LANES // 2) ** -0.5)
    k_ref[...] = _rope(dk, cos, sin)
    v_ref[...] = dv
    hq_ref[...] = hq
    kk_ref[...] = k_hg
    f_ref[...] = jnp.exp(logf)
    vh_ref[...] = hi
    gate_ref[...] = gate


def _sample_mix(x, g, w_in, hg_lb, cos, sin):
    T, D = x.shape
    W = w_in.shape[1] // 7
    sds = jax.ShapeDtypeStruct
    return pl.pallas_call(
        _sample_mix_kernel,
        out_shape=[sds((T, W), F32)] * 8,
        compiler_params=_cparams(None, 32 << 20),
        name="sample_mix",
    )(x, g, w_in, hg_lb, cos, sin)


def _sample_hgrn_kernel(hq_ref, kk_ref, f_ref, vh_ref, gate_ref, onorm_ref, s0_ref,
                        mix_ref, s1_ref):
    nb = s0_ref.shape[0]
    g0 = pl.program_id(0) * nb
    W = hq_ref.shape[-1]

    def columns(ref, h):
        rows = ref[pl.ds(pl.multiple_of(g0, nb), nb), h * LANES:(h + 1) * LANES]
        pad = jnp.zeros((LANES - nb, LANES), F32)
        return jnp.concatenate([rows, pad], axis=0).T

    o_rows = []
    for h in range(HG_HEADS):
        q_t, k_t, f_t = columns(hq_ref, h), columns(kk_ref, h), columns(f_ref, h)
        v_rows = vh_ref[pl.ds(pl.multiple_of(g0, nb), nb), h * LANES:(h + 1) * LANES]
        o_h = []
        for j in range(nb):
            bcast = lambda tile: jnp.broadcast_to(tile[:, j:j + 1], (LANES, LANES))
            s_new = bcast(f_t) * s0_ref[j, h] + bcast(k_t) * v_rows[j:j + 1, :]
            s1_ref[j, h] = s_new
            o_h.append(jnp.sum(bcast(q_t) * s_new, axis=0, keepdims=True))
        o_rows.append(jnp.concatenate(o_h, axis=0))
    o = jnp.concatenate(o_rows, axis=-1)
    gate = gate_ref[pl.ds(pl.multiple_of(g0, nb), nb), :]
    mix_ref[...] = (_head_rms(o, onorm_ref[...]) * gate).astype(BF16)


def _sample_hgrn(hq, kk, f, vh, gate, onorm, s0, nb=8):
    T, W = hq.shape
    st = pl.BlockSpec((nb, HG_HEADS, LANES, LANES), lambda i: (i, 0, 0, 0))
    full = _const_spec((T, W))
    return pl.pallas_call(
        _sample_hgrn_kernel,
        grid=(T // nb,),
        in_specs=[full] * 5 + [_const_spec((1, LANES)), st],
        out_specs=[pl.BlockSpec((nb, W), lambda i: (i, 0)), st],
        out_shape=[jax.ShapeDtypeStruct((T, W), BF16), jax.ShapeDtypeStruct(s0.shape, F32)],
        compiler_params=_cparams(("arbitrary",), 32 << 20),
        name="sample_hgrn",
    )(hq, kk, f, vh, gate, onorm, s0)


def _head_rows(x):
    return jnp.concatenate([x[:, h * LANES:(h + 1) * LANES] for h in range(DA_HEADS)] * 2, axis=0)


def _decode_queries(q):
    row = lax.broadcasted_iota(jnp.int32, (2 * DA_HEADS, LANES), 0)
    lane = lax.broadcasted_iota(jnp.int32, (2 * DA_HEADS, LANES), 1)
    return jnp.where((row // DA_HEADS) == (lane // (LANES // 2)), _head_rows(q), 0.0)


def _decode_seed(q, k_new, v_new, state):
    q8_s, m_s, l_s, acc_s = state
    q8 = _decode_queries(q)
    q8_s[...] = q8.astype(BF16)
    m_s[...] = jnp.sum(q8 * _head_rows(k_new), axis=-1, keepdims=True)
    l_s[...] = jnp.ones(l_s.shape, F32)
    acc_s[...] = _head_rows(v_new)


def _decode_resume(q, m, l, acc, state):
    q8_s, m_s, l_s, acc_s = state
    q8_s[...] = _decode_queries(q).astype(BF16)
    m_s[...] = m[:, 0:1]
    l_s[...] = l[:, 0:1]
    acc_s[...] = acc


def _decode_save(state):
    _, m_s, l_s, acc_s = state
    shape = acc_s.shape
    return jnp.broadcast_to(m_s[...], shape), jnp.broadcast_to(l_s[...], shape), acc_s[...]


def _decode_scores(k_pages, state):
    H = DA_HEADS
    q8 = state[0][...]
    s = jnp.concatenate([_dot_nt(q8, k.astype(BF16)) for k in k_pages], axis=-1)
    row = lax.broadcasted_iota(jnp.int32, s.shape, 0)
    col = lax.broadcasted_iota(jnp.int32, s.shape, 1)
    return jnp.where((col % H) == (row % H), s, -jnp.inf)


def _decode_update(s, v_pages, state):
    _, m_s, l_s, acc_s = state
    m_old = m_s[...]
    m_new = jnp.maximum(m_old, jnp.max(s, axis=-1, keepdims=True))
    alpha = jnp.exp(m_old - m_new)
    pr = jnp.exp(s - m_new)
    l_s[...] = alpha * l_s[...] + jnp.sum(pr, axis=-1, keepdims=True)
    pr = pr.astype(BF16)
    rows = v_pages[0].shape[0]
    pv = _dot(pr[:, 0:rows], v_pages[0].astype(BF16))
    for p in range(1, len(v_pages)):
        pv = pv + _dot(pr[:, p * rows:(p + 1) * rows], v_pages[p].astype(BF16))
    acc_s[...] = alpha * acc_s[...] + pv
    m_s[...] = m_new


def _decode_finish(lam_ref, onorm_ref, lam_init, state):
    _, _, l_s, acc_s = state
    H = DA_HEADS
    lam = _lambda(lam_ref, lam_init)
    o = acc_s[...] / l_s[...]
    y = _rms(o[0:H] - lam * o[H:2 * H], onorm_ref[...]) * (1.0 - lam_init)
    return jnp.concatenate([y[h:h + 1, :] for h in range(H)], axis=-1).astype(BF16)


def _paged_attn_kernel(pt_ref, q_ref, kn_ref, vn_ref, lam_ref, onorm_ref, *refs,
                       lam_init, n_pg):
    k_refs = refs[:n_pg]
    v_refs = refs[n_pg:2 * n_pg]
    o_ref = refs[2 * n_pg]
    state = refs[2 * n_pg + 1:]
    j = pl.program_id(1)

    @pl.when(j == 0)
    def _():
        _decode_seed(q_ref[0], kn_ref[0], vn_ref[0], state)

    _decode_update(_decode_scores([r[0] for r in k_refs], state), [r[0] for r in v_refs], state)

    @pl.when(j == pl.num_programs(1) - 1)
    def _():
        o_ref[0] = _decode_finish(lam_ref, onorm_ref, lam_init, state)


def _paged_attn(page_table, qd, k_new, v_new, da_lambda, onorm, cache_k, cache_v, lam_init, n_pg):
    T, n_pages = page_table.shape
    n_phys, rows, dk = cache_k.shape
    W = qd.shape[-1]
    row = pl.BlockSpec((1, 1, W), lambda b, j, pt: (b, 0, 0))

    def page_spec(p):
        return pl.BlockSpec((1, rows, dk), lambda b, j, pt: (pt[b, j * n_pg + p], 0, 0))

    const = lambda shape: pl.BlockSpec(shape, lambda b, j, pt: (0,) * len(shape))
    grid_spec = pltpu.PrefetchScalarGridSpec(
        num_scalar_prefetch=1,
        grid=(T, n_pages // n_pg),
        in_specs=[row, row, row, const(da_lambda.shape), const((1, LANES))]
        + [page_spec(p) for p in range(n_pg)] * 2,
        out_specs=row,
        scratch_shapes=[pltpu.VMEM((2 * DA_HEADS, LANES), BF16),
                        pltpu.VMEM((2 * DA_HEADS, 1), F32),
                        pltpu.VMEM((2 * DA_HEADS, 1), F32),
                        pltpu.VMEM((2 * DA_HEADS, LANES), F32)],
    )
    out = pl.pallas_call(
        functools.partial(_paged_attn_kernel, lam_init=lam_init, n_pg=n_pg),
        grid_spec=grid_spec,
        out_shape=jax.ShapeDtypeStruct((T, 1, W), BF16),
        compiler_params=_cparams(("arbitrary", "arbitrary"), 48 << 20),
        name="paged_diff_attn",
    )(page_table, qd.reshape(T, 1, W), k_new.reshape(T, 1, W), v_new.reshape(T, 1, W),
      da_lambda, onorm, *([cache_k] * n_pg), *([cache_v] * n_pg))
    return out.reshape(T, W)


def _page_ring(i, n_steps, pt_ref, ck_hbm, cv_hbm, kbuf, vbuf, sem, state, *,
               n_pg, sps, g_lo, g_hi, begin, end):
    n_slot = kbuf.shape[0]
    rows = kbuf.shape[1] // n_pg
    groups = [(ls, g) for ls in range(sps) for g in range(g_lo, g_hi)]
    n_grp = len(groups)

    def copies(b, g, slot):
        out = []
        for p in range(n_pg):
            page = pt_ref[b, g * n_pg + p]
            dst = pl.ds(p * rows, rows)
            out.append(pltpu.make_async_copy(ck_hbm.at[page], kbuf.at[slot, dst], sem.at[slot, 0]))
            out.append(pltpu.make_async_copy(cv_hbm.at[page], vbuf.at[slot, dst], sem.at[slot, 1]))
        return out

    def start(b, g, slot):
        for cp in copies(b, g, slot):
            cp.start()

    ahead = n_slot - 1

    @pl.when(i == 0)
    def _():
        for n in range(ahead):
            start(groups[n][0], groups[n][1], n % n_slot)

    def fold_group(n):
        ls, g = groups[n]
        b = i * sps + ls
        slot = n % n_slot
        nxt = n + ahead
        if nxt < n_grp:
            start(i * sps + groups[nxt][0], groups[nxt][1], nxt % n_slot)
        else:
            @pl.when(i + 1 < n_steps)
            def _():
                ls2, g2 = groups[nxt - n_grp]
                start((i + 1) * sps + ls2, g2, nxt % n_slot)
        if n in scored:
            s = scored.pop(n)
        else:
            s = arrive_and_score(n)
        if n + 1 < n_grp and groups[n + 1][0] == ls:
            scored[n + 1] = arrive_and_score(n + 1)
        _decode_update(s, [vbuf[slot, p * rows:(p + 1) * rows, :] for p in range(n_pg)], state)
        if g == g_hi - 1:
            end(b)

    scored = {}

    def arrive_and_score(n):
        ls, g = groups[n]
        b = i * sps + ls
        slot = n % n_slot
        for cp in copies(b, g, slot):
            cp.wait()
        if g == g_lo:
            begin(b)
        return _decode_scores([kbuf[slot, p * rows:(p + 1) * rows, :] for p in range(n_pg)], state)

    return fold_group, n_grp


def _ffn_decode_kernel(pt_ref, x_ref, gf_ref, wg_ref, wu_ref, wd_ref, gfin_ref,
                       q_ref, m_ref, l_ref, a_ref, lam_ref, onorm_ref, ck_hbm, cv_hbm,
                       y_ref, o_ref, kbuf, vbuf, sem, q8_s, m_s, l_s, acc_s,
                       *, lam_init, n_pg, sps, g_lo, col_bounds):
    i = pl.program_id(0)
    state = (q8_s, m_s, l_s, acc_s)

    def begin(b):
        _decode_resume(q_ref[b], m_ref[b], l_ref[b], a_ref[b], state)

    def end(b):
        o_ref[b] = _decode_finish(lam_ref, onorm_ref, lam_init, state)

    fold_group, n_grp = _page_ring(i, pl.num_programs(0), pt_ref, ck_hbm, cv_hbm, kbuf, vbuf, sem,
                                   state, n_pg=n_pg, sps=sps, g_lo=g_lo,
                                   g_hi=pt_ref.shape[1] // n_pg, begin=begin, end=end)

    x = x_ref[...]
    h = _rms(x, gf_ref[...]).astype(BF16)
    n_col = len(col_bounds) - 1

    def up(k):
        lo, hi = col_bounds[k], col_bounds[k + 1]
        return _dot(h, wg_ref[:, lo:hi]), _dot(h, wu_ref[:, lo:hi])

    folded = 0
    acc = None
    gu = up(0)
    for k in range(n_col):
        g, u = gu
        part = _dot((g * _sigmoid(g) * u).astype(BF16), wd_ref[col_bounds[k]:col_bounds[k + 1], :])
        if k + 1 < n_col:
            gu = up(k + 1)
        due = (k + 1) * n_grp // n_col
        for n in range(folded, due):
            fold_group(n)
        folded = due
        acc = part if acc is None else acc + part
    y_ref[...] = _rms(x + acc, gfin_ref[...])


def _ring_scratch(n_slot, n_pg, rows, dk):
    return [pltpu.VMEM((n_slot, n_pg * rows, dk), F32),
            pltpu.VMEM((n_slot, n_pg * rows, dk), F32),
            pltpu.SemaphoreType.DMA((n_slot, 2)),
            pltpu.VMEM((2 * DA_HEADS, LANES), BF16),
            pltpu.VMEM((2 * DA_HEADS, 1), F32),
            pltpu.VMEM((2 * DA_HEADS, 1), F32),
            pltpu.VMEM((2 * DA_HEADS, LANES), F32)]


def _ffn_decode(x, gf, wg, wu, wd, gfin, ts, page_table, qd, softmax_state, da_lambda, onorm,
                cache_k, cache_v, lam_init, n_pg, n_slot, g_lo):
    T, D = x.shape
    F = wg.shape[1]
    Ts, n_pages = page_table.shape
    n_phys, rows, dk = cache_k.shape
    W = qd.shape[-1]
    n_tiles = T // ts
    sps = Ts // n_tiles
    mxu_n = 2 * LANES
    col_bounds = tuple(range(0, F, mxu_n)) + (F,)
    tok = pl.BlockSpec((ts, D), lambda i, pt: (i, 0))
    const = lambda shape: pl.BlockSpec(shape, lambda i, pt: (0,) * len(shape),
                                       pipeline_mode=pl.Buffered(1))
    st = const((Ts, 2 * DA_HEADS, LANES))
    hbm = pl.BlockSpec(memory_space=pl.ANY)
    grid_spec = pltpu.PrefetchScalarGridSpec(
        num_scalar_prefetch=1,
        grid=(n_tiles,),
        in_specs=[tok, const((1, D)), const(wg.shape), const(wu.shape), const(wd.shape),
                  const((1, D)), const((Ts, 1, W)), st, st, st, const(da_lambda.shape),
                  const((1, LANES)), hbm, hbm],
        out_specs=[tok, pl.BlockSpec((Ts, 1, W), lambda i, pt: (0, 0, 0))],
        scratch_shapes=_ring_scratch(n_slot, n_pg, rows, dk),
    )
    y, o = pl.pallas_call(
        functools.partial(_ffn_decode_kernel, lam_init=lam_init, n_pg=n_pg, sps=sps, g_lo=g_lo,
                          col_bounds=col_bounds),
        grid_spec=grid_spec,
        out_shape=[jax.ShapeDtypeStruct((T, D), F32), jax.ShapeDtypeStruct((Ts, 1, W), BF16)],
        compiler_params=_cparams(("arbitrary",), VMEM_LIMIT_CAP),
        name="ffn_decode",
    )(page_table, x, gf, wg, wu, wd, gfin, qd.reshape(Ts, 1, W), *softmax_state,
      da_lambda, onorm, cache_k, cache_v)
    return y, o.reshape(Ts, W)


def _rope_tables(pos, dh, n_maps):
    inv = ROPE_THETA ** (-jnp.arange(0, dh, 2, dtype=F32) / dh)
    ang = pos.astype(F32)[:, None] * inv[None, :]
    cos = jnp.cos(ang)
    sin = jnp.sin(ang)
    cos = jnp.tile(jnp.concatenate([cos, cos], axis=-1), (1, n_maps))
    sin = jnp.tile(jnp.concatenate([-sin, sin], axis=-1), (1, n_maps))
    return cos, sin


def _pick_tile(n, pref):
    t = min(n, pref)
    while n % t:
        t //= 2
    return t


def kernel(x_prompt, x_sample, mem_prompt, cache_k, cache_v, cache_mem_k, cache_mem_v, state_hgrn, page_table, norm_mix, w_in, hg_lb, hg_onorm, da_lambda, da_onorm, w_out, norm_mem_q, norm_mem_kv, w_mq, w_mk, w_mv, w_mo, norm_ffn, w_gate, w_up, w_down, norm_final):
    B, S, D = x_prompt.shape
    T = x_sample.shape[0]
    depth = w_in.shape[0]
    assert depth == 1 and x_sample.shape[1] == 1
    l = 0
    lam_init = 0.8 - 0.6 * math.exp(-0.3 * l)
    n_phys, page = cache_k.shape[1], cache_k.shape[2]
    past_len = page_table.shape[1] * page
    W = w_in.shape[2] // 7

    bf = lambda w: w.astype(BF16)
    w_in_b, w_out_b = bf(w_in[l]), bf(w_out[l])
    w_mq_b, w_mk_b, w_mv_b, w_mo_b = bf(w_mq[l]), bf(w_mk[l]), bf(w_mv[l]), bf(w_mo[l])
    w_gate_b, w_up_b, w_down_b = bf(w_gate[l]), bf(w_up[l]), bf(w_down[l])
    row = lambda g: g.reshape(1, -1)
    lb2 = hg_lb[l:l + 2]
    lam_p = da_lambda[l]

    cos_p, sin_p = _rope_tables(jnp.arange(S), LANES // 2, 2 * DA_HEADS)
    cos_s, sin_s = _rope_tables(past_len + jnp.arange(1), LANES // 2, 2 * DA_HEADS)
    cos_s = jnp.broadcast_to(cos_s, (T, W))
    sin_s = jnp.broadcast_to(sin_s, (T, W))

    mk, mv, mkb, mvb = _mem_kv(mem_prompt, row(norm_mem_kv[l]), w_mk_b, w_mv_b)
    qd, k_p, v_p, kb, v_att, mix_hg, hs_p = _prompt_mix(
        x_prompt, row(norm_mix[l]), w_in_b, lb2, cos_p, sin_p, row(hg_onorm[l]),
        ts=_pick_tile(S, 512))
    mix_da = _diff_attn_prompt(qd, kb, v_att, lam_p, row(da_onorm[l]), lam_init,
                               tq=_pick_tile(S, 2048), tk=_pick_tile(S, 256))

    xs = x_sample.reshape(T, D)
    qd_s, k_s, v_s, hq_s, kk_s, f_s, vh_s, gate_s = _sample_mix(
        xs, row(norm_mix[l]), w_in_b, lb2, cos_s, sin_s)
    mix_hg_s, hs_s = _sample_hgrn(hq_s, kk_s, f_s, vh_s, gate_s, row(hg_onorm[l]), state_hgrn[l])
    ck = cache_k.reshape(depth * n_phys, page * DA_HEADS, LANES)
    cv = cache_v.reshape(depth * n_phys, page * DA_HEADS, LANES)
    n_pages = page_table.shape[1]
    tok_ts = _pick_tile(S, 512)
    n_tiles = (B * S) // tok_ts
    n_pg = _pick_tile(n_pages, 8)
    n_slot = 4
    gps = n_pages // n_pg
    sps = T // n_tiles if T % n_tiles == 0 else 0
    share = 3 * D / (3 * D + 3 * w_gate.shape[2])
    splits = [g for g in range(1, gps)
              if (sps * g) % n_slot == 0 and (sps * (gps - g)) % n_slot == 0]
    oc_args = (x_prompt, mix_hg, mix_da, w_out_b, row(norm_mem_q[l]), w_mq_b, mkb, mvb, w_mo_b)
    ffn_args = (row(norm_ffn[l]), w_gate_b, w_up_b, w_down_b, row(norm_final))
    if sps and splits:
        g_split = min(splits, key=lambda g: abs(g - share * gps))
        x2, softmax_state = _out_cross_decode(*oc_args, tok_ts, page_table, qd_s, k_s, v_s, ck, cv,
                                              n_pg, n_slot, g_split)
        y_p, mix_da_s = _ffn_decode(x2.reshape(B * S, D), *ffn_args, tok_ts, page_table, qd_s,
                                    softmax_state, lam_p, row(da_onorm[l]), ck, cv, lam_init,
                                    n_pg, n_slot, g_split)
    else:
        x2 = _out_cross_prompt(*oc_args, ts=tok_ts)
        y_p = _ffn(x2.reshape(B * S, D), *ffn_args, ts=tok_ts)
        mix_da_s = _paged_attn(page_table, qd_s, k_s, v_s, lam_p, row(da_onorm[l]), ck, cv,
                               lam_init, n_pg=_pick_tile(n_pages, 16))
    y_p = y_p.reshape(B, S, D)
    MW = w_mk.shape[2]
    x2_s = _out_cross_sample(xs, mix_hg_s, mix_da_s, w_out_b, row(norm_mem_q[l]), w_mq_b,
                             cache_mem_k[l], cache_mem_v[l],
                             w_mo_b)
    y_s = _ffn(x2_s, row(norm_ffn[l]), w_gate_b, w_up_b, w_down_b, row(norm_final), ts=T)

    dk = LANES
    return (y_p, y_s.reshape(T, 1, D),
            hs_p[None],
            k_p.reshape(1, B, S, DA_HEADS, dk), v_p.reshape(1, B, S, DA_HEADS, dk),
            mk[None], mv[None],
            hs_s[None],
            k_s.reshape(1, T, 1, DA_HEADS, dk), v_s.reshape(1, T, 1, DA_HEADS, dk))
```

```python
import functools
import math

import jax
import jax.numpy as jnp
from jax import lax
from jax.experimental import pallas as pl
from jax.experimental.pallas import tpu as pltpu

F32 = jnp.float32
BF16 = jnp.bfloat16
EPS = 1e-6
ROPE_THETA = 10000.0

HG_HEADS = 4
DA_HEADS = 4
MEM_HEADS = 4
HG_CHUNK = 64
LANES = 128
BF16_ROWS = 16
ATTN_MXU_DERATE = 2.5
VMEM_LIMIT_CAP = 56 << 20

_NT = (((1,), (1,)), ((), ()))
_TN = (((0,), (0,)), ((), ()))


def _dot(a, b):
    return jnp.dot(a, b, preferred_element_type=F32)


def _dot_nt(a, b):
    return lax.dot_general(a, b, _NT, preferred_element_type=F32)


def _dot_tn(a, b):
    return lax.dot_general(a, b, _TN, preferred_element_type=F32)


def _rms(x, g):
    ms = jnp.mean(x * x, axis=-1, keepdims=True)
    return x * lax.rsqrt(ms + EPS) * g


def _sigmoid(x):
    return 1.0 / (1.0 + jnp.exp(-x))


def _cparams(semantics, vmem_bytes):
    return pltpu.CompilerParams(
        dimension_semantics=semantics,
        vmem_limit_bytes=int(min(max(vmem_bytes, 16 << 20), VMEM_LIMIT_CAP)))


def _const_spec(shape):
    nd = len(shape)
    return pl.BlockSpec(shape, lambda *_: (0,) * nd, pipeline_mode=pl.Buffered(1))


def _lower_bound(lb_ref):
    a0 = lb_ref[0:1, :]
    a1 = lb_ref[1:2, :]
    m = jnp.maximum(a0, a1)
    e0 = jnp.exp(a0 - m)
    e1 = jnp.exp(a1 - m)
    return e0 / (e0 + e1)


def _rope(x, cos, sin_signed):
    n = x.shape[-1]
    lane = lax.broadcasted_iota(jnp.int32, x.shape, x.ndim - 1)
    swapped = jnp.where((lane & 63) < 32,
                        pltpu.roll(x, n - 32, x.ndim - 1),
                        pltpu.roll(x, 32, x.ndim - 1))
    return x * cos + swapped * sin_signed


def _lambda(lam_ref, lam_init):
    lp = lam_ref[...]
    s01 = jnp.sum(lp[0:1, :] * lp[1:2, :], axis=-1, keepdims=True)
    s23 = jnp.sum(lp[2:3, :] * lp[3:4, :], axis=-1, keepdims=True)
    return jnp.exp(s01) - jnp.exp(s23) + lam_init


def _mixer_sections(z, lb):
    w = z.shape[-1] // 7
    hq, zf, hi, hg, dq, dk, dv = (z[:, i * w:(i + 1) * w] for i in range(7))
    sig = _sigmoid(zf)
    logf = jnp.log(lb + (1.0 - lb) * sig)
    k_hg = (1.0 - lb) * (1.0 - sig)
    gate = hg * _sigmoid(hg)
    return hq, k_hg, logf, hi, gate, dq, dk, dv


def _head_rms(o, g):
    outs = []
    for h in range(o.shape[-1] // LANES):
        oh = o[:, h * LANES:(h + 1) * LANES]
        outs.append(_rms(oh, g))
    return jnp.concatenate(outs, axis=-1)


def _memkv_kernel(mem_ref, g_ref, wk_ref, wv_ref, k_ref, v_ref, kb_ref, vb_ref):
    m = _rms(mem_ref[0], g_ref[...]).astype(BF16)
    k = _dot(m, wk_ref[...])
    v = _dot(m, wv_ref[...])
    dh = k_ref.shape[-1]
    for h in range(MEM_HEADS):
        k_ref[0, :, h, :] = k[:, h * dh:(h + 1) * dh]
        v_ref[0, :, h, :] = v[:, h * dh:(h + 1) * dh]
    kb_ref[0] = k.astype(BF16)
    vb_ref[0] = v.astype(BF16)


def _mem_kv(mem, g, wk, wv):
    B, N, D = mem.shape
    W = wk.shape[1]
    dh = W // MEM_HEADS
    blk = lambda d: pl.BlockSpec((1, N, d), lambda b: (b, 0, 0))
    blk4 = pl.BlockSpec((1, N, MEM_HEADS, dh), lambda b: (b, 0, 0, 0))
    return pl.pallas_call(
        _memkv_kernel,
        grid=(B,),
        in_specs=[blk(D), _const_spec((1, D)), _const_spec((D, W)), _const_spec((D, W))],
        out_specs=[blk4, blk4, blk(W), blk(W)],
        out_shape=[jax.ShapeDtypeStruct((B, N, MEM_HEADS, dh), F32)] * 2
        + [jax.ShapeDtypeStruct((B, N, W), BF16)] * 2,
        compiler_params=_cparams(("arbitrary",), 40 << 20),
        name="mem_kv",
    )(mem, g, wk, wv)


def _hgrn_levels(q, kk, lf, v):
    C, W = q.shape
    t = lax.broadcasted_iota(jnp.int32, (C, W), 0)
    n_lvl = int(math.log2(C))

    c = lf
    lvls = []
    for l in range(n_lvl):
        m = 1 << l
        upper = (t & m) != 0
        if m < 8:
            y = c
            for i in range(l):
                y = jnp.where((t & (1 << i)) == 0, pltpu.roll(y, C - (1 << i), 0), y)
            bc = jnp.where(upper, pltpu.roll(y, m, 0), y)
        else:
            pieces = []
            for j in range(C // (2 * m)):
                r = 2 * m * j + m - 1
                pieces.append(jnp.broadcast_to(c[r:r + 1, :], (2 * m, W)))
            bc = pieces[0] if len(pieces) == 1 else jnp.concatenate(pieces, axis=0)
        e = jnp.exp(jnp.where(upper, c, bc - c))
        lvls.append((jnp.where(upper, q, kk) * e).astype(BF16))
        c = c + jnp.where(upper, bc, 0.0)
    b = c
    b_last = b[C - 1:C, :]
    return dict(lvls=lvls, q=q.astype(BF16), k=kk.astype(BF16), v=v.astype(BF16),
                q_state=(q * jnp.exp(b)).astype(BF16),
                k_state=(kk * jnp.exp(b_last - b)).astype(BF16),
                decay=jnp.exp(b_last))


def _hgrn_products(lv, st_ref):
    out = []
    for h in range(lv["q"].shape[1] // LANES):
        sl = slice(h * LANES, (h + 1) * LANES)
        pairs = [_dot_nt(lv["q"][:, sl], lv["k"][:, sl])]
        pairs += [_dot_nt(r[:, sl], r[:, sl]) for r in lv["lvls"]]
        o_state = _dot_nt(lv["q_state"][:, sl], st_ref[h].astype(BF16))
        st_term = _dot_tn(lv["v"][:, sl], lv["k_state"][:, sl])
        out.append((pairs, o_state, st_term))
    return out


def _hgrn_finish(lv, prods, st_ref):
    C = lv["q"].shape[0]
    ti = lax.broadcasted_iota(jnp.int32, (C, C), 0)
    si = lax.broadcasted_iota(jnp.int32, (C, C), 1)
    diff_bits = jnp.where(ti > si, ti ^ si, 0)
    outs = []
    for h, (pairs, o_state, st_term) in enumerate(prods):
        sl = slice(h * LANES, (h + 1) * LANES)
        a = jnp.where(ti == si, pairs[0], 0.0)
        for l, pr in enumerate(pairs[1:]):
            a = a + jnp.where((diff_bits >> l) == 1, pr, 0.0)
        outs.append(_dot(a.astype(BF16), lv["v"][:, sl]) + o_state)
        st_ref[h] = st_ref[h] * lv["decay"][:, sl] + st_term
    return jnp.concatenate(outs, axis=-1)


def _prompt_mix_kernel(x_ref, g_ref, w_ref, lb_ref, cos_ref, sin_ref, onorm_ref,
                       qd_ref, k_ref, v_ref, kb_ref, va_ref, mix_ref, state_ref,
                       o_s, st_s):
    s = pl.program_id(1)
    last = pl.num_programs(1) - 1

    @pl.when(s == 0)
    def _():
        st_s[...] = jnp.zeros_like(st_s)

    h = _rms(x_ref[0], g_ref[...]).astype(BF16)

    ts = x_ref.shape[1]
    sec_w = w_ref.shape[1] // 7
    n_chunk = ts // HG_CHUNK
    stage = {"done": 0, "lv": None}

    def project(sec):
        return _dot(h, w_ref[:, sec * sec_w:(sec + 1) * sec_w])

    def levels(ci):
        rows = slice(ci * HG_CHUNK, (ci + 1) * HG_CHUNK)
        return _hgrn_levels(hq[rows, :], k_hg[rows, :], logf[rows, :], hi[rows, :])

    def recur(upto):
        for ci in range(stage["done"], min(upto, n_chunk)):
            lv = stage["lv"] if stage["lv"] is not None else levels(ci)
            prods = _hgrn_products(lv, st_s)
            stage["lv"] = levels(ci + 1) if ci + 1 < n_chunk else None
            o_s[ci * HG_CHUNK:(ci + 1) * HG_CHUNK, :] = _hgrn_finish(lv, prods, st_s)
        stage["done"] = max(stage["done"], min(upto, n_chunk))

    def head_rows_out(ref, val):
        for hh in range(DA_HEADS):
            ref[0, pl.ds(hh, ts, stride=DA_HEADS), :] = val[:, hh * LANES:(hh + 1) * LANES]

    lb = _lower_bound(lb_ref)
    cos = cos_ref[...]
    sin = sin_ref[...]
    hq = project(0)
    sig = _sigmoid(project(1))
    logf = jnp.log(lb + (1.0 - lb) * sig)
    k_hg = (1.0 - lb) * (1.0 - sig)
    hi = project(2)
    hg = project(3)
    recur(n_chunk // 4)
    q_da = _rope(project(4), cos, sin)
    qd_ref[0] = (q_da * (math.log2(math.e) * float(LANES // 2) ** -0.5)).astype(BF16)
    recur(n_chunk // 2)
    k_da = _rope(project(5), cos, sin)
    head_rows_out(k_ref, k_da)
    kb_ref[0] = k_da.astype(BF16)
    recur(3 * n_chunk // 4)
    dv = project(6)
    head_rows_out(v_ref, dv)
    va_ref[0] = dv
    recur(n_chunk)
    gate = hg * _sigmoid(hg)
    mix_ref[0] = (_head_rms(o_s[...], onorm_ref[...]) * gate).astype(BF16)

    @pl.when(s == last)
    def _():
        for hh in range(HG_HEADS):
            state_ref[0, hh] = st_s[hh].T


def _prompt_mix(x, g, w_in, hg_lb, cos, sin, onorm, ts):
    B, S, D = x.shape
    W = w_in.shape[1] // 7
    tok = lambda d: pl.BlockSpec((1, ts, d), lambda b, s: (b, s, 0))
    tab = pl.BlockSpec((ts, W), lambda b, s: (s, 0))
    rows = pl.BlockSpec((1, ts * DA_HEADS, LANES), lambda b, s: (b, s, 0))
    state = pl.BlockSpec((1, HG_HEADS, LANES, LANES), lambda b, s: (b, 0, 0, 0))
    sds = jax.ShapeDtypeStruct
    return pl.pallas_call(
        _prompt_mix_kernel,
        grid=(B, S // ts),
        in_specs=[tok(D), _const_spec((1, D)), _const_spec(w_in.shape),
                  _const_spec(hg_lb.shape), tab, tab, _const_spec((1, LANES))],
        out_specs=[tok(W), rows, rows, tok(W), tok(W), tok(W), state],
        out_shape=[sds((B, S, W), BF16), sds((B, S * DA_HEADS, LANES), F32),
                   sds((B, S * DA_HEADS, LANES), F32), sds((B, S, W), BF16),
                   sds((B, S, W), F32), sds((B, S, W), BF16),
                   sds((B, HG_HEADS, LANES, LANES), F32)],
        scratch_shapes=[pltpu.VMEM((ts, W), F32), pltpu.VMEM((HG_HEADS, LANES, LANES), F32)],
        compiler_params=_cparams(("arbitrary", "arbitrary"), 48 << 20),
        name="prompt_mix",
    )(x, g, w_in, hg_lb, cos, sin, onorm)


def _diff_attn_kernel(q_ref, k_ref, v_ref, lam_ref, onorm_ref, o_ref,
                      qq_s, vt_s, m_s, acc_s, *, lam_init, tk):
    _diff_attn_body(q_ref, k_ref, v_ref, lam_ref, onorm_ref, o_ref, qq_s, vt_s, m_s, acc_s,
                    lam_init=lam_init, tk=tk)


def _diff_attn_decode_kernel(pt_ref, q_ref, k_ref, v_ref, lam_ref, onorm_ref, qs_ref, kn_ref, vn_ref,
                             ck_hbm, cv_hbm, o_ref, m_ref, l_ref, a_ref,
                             qq_s, vt_s, m_s, acc_s, kbuf, vbuf, sem, q8_s, dm_s, dl_s, dacc_s,
                             *, lam_init, tk, n_pg, sps, g_hi):
    step = ((pl.program_id(0) * pl.num_programs(1) + pl.program_id(1)) * pl.num_programs(2)
            + pl.program_id(2))
    n_steps = pl.num_programs(0) * pl.num_programs(1) * pl.num_programs(2)
    state = (q8_s, dm_s, dl_s, dacc_s)

    def begin(b):
        _decode_seed(qs_ref[b], kn_ref[b], vn_ref[b], state)

    def end(b):
        m_ref[b], l_ref[b], a_ref[b] = _decode_save(state)

    fold_group, n_grp = _page_ring(step, n_steps, pt_ref, ck_hbm, cv_hbm, kbuf, vbuf, sem, state,
                                   n_pg=n_pg, sps=sps, g_lo=0, g_hi=g_hi, begin=begin, end=end)
    folded = [0]

    def tick(frac):
        due = int(frac * n_grp)
        for n in range(folded[0], due):
            fold_group(n)
        folded[0] = max(folded[0], due)

    _diff_attn_body(q_ref, k_ref, v_ref, lam_ref, onorm_ref, o_ref, qq_s, vt_s, m_s, acc_s,
                    lam_init=lam_init, tk=tk, tick=tick)
    tick(1.0)


def _diff_attn_body(q_ref, k_ref, v_ref, lam_ref, onorm_ref, o_ref,
                    qq_s, vt_s, m_s, acc_s, *, lam_init, tk, tick=None):
    i = pl.program_id(2)
    tq = q_ref.shape[1]
    n_ck = tq // tk
    n_kv = v_ref.shape[1] // tk

    @pl.when(i == 0)
    def _():
        ones = jnp.ones((vt_s.shape[1] - LANES, tk), BF16)
        for jj in range(n_kv):
            vt_s[jj, 0:LANES, :] = v_ref[0, jj * tk:(jj + 1) * tk, :].T.astype(BF16)
            vt_s[jj, LANES:, :] = ones

    q = q_ref[0]
    lane = lax.broadcasted_iota(jnp.int32, q.shape, 1)
    zero = jnp.zeros_like(q)
    qq_s[0:tq, :] = jnp.where(lane < LANES // 2, q, zero)
    qq_s[tq:2 * tq, :] = jnp.where(lane >= LANES // 2, q, zero)
    m_s[...] = jnp.full(m_s.shape, -jnp.inf, F32)
    acc_s[...] = jnp.zeros(acc_s.shape, F32)

    def scores(j, c):
        k = k_ref[0, pl.ds(pl.multiple_of(j * tk, tk), tk), :]
        return _dot_nt(k, qq_s[c * tk:(c + 1) * tk, :])

    def softmax(j, c, tri, s):
        cols = slice(c * tk, (c + 1) * tk)
        if tri:
            key = lax.broadcasted_iota(jnp.int32, s.shape, 0)
            qry = lax.broadcasted_iota(jnp.int32, s.shape, 1)
            s = jnp.where(key <= qry, s, -jnp.inf)
        m_old = m_s[:, cols]
        m_new = jnp.maximum(m_old, jnp.max(s, axis=0, keepdims=True))
        alpha = jnp.exp2(m_old - m_new)
        p = jnp.exp2(s - m_new)
        m_s[:, cols] = m_new
        return cols, alpha, _dot(vt_s[j], p.astype(BF16))

    def accumulate(cols, alpha, pv):
        acc_s[:, cols] = alpha * acc_s[:, cols] + pv

    ahead = 4

    def run(work, tick=None):
        s = {n: scores(*work[n][:2]) for n in range(min(ahead, len(work)))}
        pending = None
        for n, (j, c, tri) in enumerate(work):
            if n + ahead < len(work):
                s[n + ahead] = scores(*work[n + ahead][:2])
            done = softmax(j, c, tri, s.pop(n))
            if pending is not None:
                accumulate(*pending)
            pending = done
            if tick is not None:
                tick((n + 1) / len(work))
        accumulate(*pending)

    per_iter = 2 if n_ck % 2 == 0 else 1

    def full_blocks(jj, carry):
        run([(jj * per_iter + r, c, False) for r in range(per_iter) for c in range(2 * n_ck)])
        return carry

    lax.fori_loop(0, (i * n_ck) // per_iter, full_blocks, 0)
    run([(i * n_ck + d, mp * n_ck + cq, cq == d)
         for d in range(n_ck) for mp in range(2) for cq in range(d, n_ck)], tick)

    lam = _lambda(lam_ref, lam_init)
    a = acc_s[0:LANES, :] * (1.0 / acc_s[LANES:LANES + 1, :])
    o_t = a[:, 0:tq] - lam * a[:, tq:2 * tq]
    ms = jnp.mean(o_t * o_t, axis=0, keepdims=True)
    y = (o_t * lax.rsqrt(ms + EPS)).T
    o_ref[0] = (y * onorm_ref[...] * (1.0 - lam_init)).astype(BF16)


def _diff_attn_prompt(qd, kb, v, da_lambda, onorm, lam_init, tq, tk):
    B, S, W = qd.shape
    H = W // LANES
    qspec = pl.BlockSpec((1, tq, LANES), lambda b, h, i: (b, i, h))
    kvspec = pl.BlockSpec((1, S, LANES), lambda b, h, i: (b, 0, h))
    return pl.pallas_call(
        functools.partial(_diff_attn_kernel, lam_init=lam_init, tk=tk),
        grid=(B, H, S // tq),
        in_specs=[qspec, kvspec, kvspec, _const_spec(da_lambda.shape), _const_spec((1, LANES))],
        out_specs=qspec,
        out_shape=jax.ShapeDtypeStruct((B, S, W), BF16),
        scratch_shapes=[pltpu.VMEM((2 * tq, LANES), BF16),
                        pltpu.VMEM((S // tk, LANES + BF16_ROWS, tk), BF16),
                        pltpu.VMEM((1, 2 * tq), F32),
                        pltpu.VMEM((LANES + BF16_ROWS, 2 * tq), F32)],
        compiler_params=_cparams(("arbitrary",) * 3, 32 << 20),
        name="diff_attn_prompt",
    )(qd, kb, v, da_lambda, onorm)


def _diff_attn_decode(qd, kb, v, da_lambda, onorm, lam_init, tq, tk, page_table, qd_s, k_new, v_new,
                      cache_k, cache_v, n_pg, n_slot, g_hi):
    B, S, W = qd.shape
    H = W // LANES
    Ts = page_table.shape[0]
    n_phys, rows, dk = cache_k.shape
    sps = Ts // (B * H * (S // tq))
    qspec = pl.BlockSpec((1, tq, LANES), lambda b, h, i, pt: (b, i, h))
    kvspec = pl.BlockSpec((1, S, LANES), lambda b, h, i, pt: (b, 0, h))
    const = lambda shape: pl.BlockSpec(shape, lambda b, h, i, pt: (0,) * len(shape),
                                       pipeline_mode=pl.Buffered(1))
    sample = const((Ts, 1, W))
    st = pl.BlockSpec((Ts, 2 * DA_HEADS, LANES), lambda b, h, i, pt: (0, 0, 0))
    hbm = pl.BlockSpec(memory_space=pl.ANY)
    grid_spec = pltpu.PrefetchScalarGridSpec(
        num_scalar_prefetch=1,
        grid=(B, H, S // tq),
        in_specs=[qspec, kvspec, kvspec, const(da_lambda.shape), const((1, LANES)),
                  sample, sample, sample, hbm, hbm],
        out_specs=[qspec, st, st, st],
        scratch_shapes=[pltpu.VMEM((2 * tq, LANES), BF16),
                        pltpu.VMEM((S // tk, LANES + BF16_ROWS, tk), BF16),
                        pltpu.VMEM((1, 2 * tq), F32),
                        pltpu.VMEM((LANES + BF16_ROWS, 2 * tq), F32)]
        + _ring_scratch(n_slot, n_pg, rows, dk),
    )
    st_shape = jax.ShapeDtypeStruct((Ts, 2 * DA_HEADS, LANES), F32)
    o, m, l, a = pl.pallas_call(
        functools.partial(_diff_attn_decode_kernel, lam_init=lam_init, tk=tk, n_pg=n_pg, sps=sps,
                          g_hi=g_hi),
        grid_spec=grid_spec,
        out_shape=[jax.ShapeDtypeStruct((B, S, W), BF16), st_shape, st_shape, st_shape],
        compiler_params=_cparams(("arbitrary",) * 3, 48 << 20),
        name="diff_attn_decode",
    )(page_table, qd, kb, v, da_lambda, onorm, qd_s.reshape(Ts, 1, W), k_new.reshape(Ts, 1, W),
      v_new.reshape(Ts, 1, W), cache_k, cache_v)
    return o, (m, l, a)


def _cross_attn_heads(q, mk_head, mv_head):
    dh = q.shape[-1] // MEM_HEADS
    scores = [_dot_nt(q[:, h * dh:(h + 1) * dh], mk_head(h)) for h in range(MEM_HEADS)]
    probs = []
    for s in scores:
        m = jnp.max(s, axis=-1, keepdims=True)
        p = jnp.exp(s - m)
        l = jnp.sum(p, axis=-1, keepdims=True)
        probs.append((p / l).astype(BF16))
    return jnp.concatenate([_dot(probs[h], mv_head(h)) for h in range(MEM_HEADS)], axis=-1)


def _out_cross_kernel(x_ref, mhg_ref, mda_ref, wo_ref, gq_ref, wq_ref, mk_ref, mv_ref,
                      wmo_ref, x2_ref):
    w = mhg_ref.shape[-1]
    dh = wq_ref.shape[1] // MEM_HEADS
    ts = x_ref.shape[1]
    halves = [slice(0, ts // 2), slice(ts // 2, ts)] if ts % 16 == 0 else [slice(0, ts)]
    x1 = [x_ref[0, r, :] + _dot(mhg_ref[0, r, :], wo_ref[0:w, :])
          + _dot(mda_ref[0, r, :], wo_ref[w:2 * w, :]) for r in halves]
    q = [(_dot(_rms(v, gq_ref[...]).astype(BF16), wq_ref[...]) * (float(dh) ** -0.5)).astype(BF16)
         for v in x1]
    mk = lambda h: mk_ref[0, :, h * dh:(h + 1) * dh]
    mv = lambda h: mv_ref[0, :, h * dh:(h + 1) * dh]
    o = [_cross_attn_heads(v, mk, mv) for v in q]
    for r, v1, vo in zip(halves, x1, o):
        x2_ref[0, r, :] = v1 + _dot(vo.astype(BF16), wmo_ref[...])


def _out_cross_prompt(x, mhg, mda, w_out, gq, w_mq, mkb, mvb, w_mo, ts):
    B, S, D = x.shape
    W = mhg.shape[-1]
    N, MW = mkb.shape[1], mkb.shape[2]
    tok = lambda d: pl.BlockSpec((1, ts, d), lambda b, s: (b, s, 0))
    mem = pl.BlockSpec((1, N, MW), lambda b, s: (b, 0, 0))
    return pl.pallas_call(
        _out_cross_kernel,
        grid=(B, S // ts),
        in_specs=[tok(D), tok(W), tok(W), _const_spec(w_out.shape), _const_spec((1, D)),
                  _const_spec(w_mq.shape), mem, mem, _const_spec(w_mo.shape)],
        out_specs=tok(D),
        out_shape=jax.ShapeDtypeStruct((B, S, D), F32),
        compiler_params=_cparams(("arbitrary", "arbitrary"), 48 << 20),
        name="out_cross_prompt",
    )(x, mhg, mda, w_out, gq, w_mq, mkb, mvb, w_mo)


def _out_cross_decode_kernel(pt_ref, x_ref, mhg_ref, mda_ref, wo_ref, gq_ref, wq_ref, mk_ref, mv_ref,
                             wmo_ref, q_ref, kn_ref, vn_ref, mi_ref, li_ref, ai_ref, ck_hbm, cv_hbm,
                             x2_ref, m_ref, l_ref, a_ref, kbuf, vbuf, sem, q8_s, m_s, l_s, acc_s,
                             *, n_pg, sps, g_lo, g_hi):
    i = pl.program_id(0) * pl.num_programs(1) + pl.program_id(1)
    n_steps = pl.num_programs(0) * pl.num_programs(1)
    state = (q8_s, m_s, l_s, acc_s)

    def begin(b):
        if g_lo == 0:
            _decode_seed(q_ref[b], kn_ref[b], vn_ref[b], state)
        else:
            _decode_resume(q_ref[b], mi_ref[b], li_ref[b], ai_ref[b], state)

    def end(b):
        m_ref[b], l_ref[b], a_ref[b] = _decode_save(state)

    fold_group, n_grp = _page_ring(i, n_steps, pt_ref, ck_hbm, cv_hbm, kbuf, vbuf, sem, state,
                                   n_pg=n_pg, sps=sps, g_lo=g_lo, g_hi=g_hi, begin=begin, end=end)
    folded = [0]

    def fold_until(frac):
        due = int(round(frac * n_grp))
        for n in range(folded[0], due):
            fold_group(n)
        folded[0] = max(folded[0], due)

    w = mhg_ref.shape[-1]
    dh = wq_ref.shape[1] // MEM_HEADS
    ts = x_ref.shape[1]
    halves = [slice(0, ts // 2), slice(ts // 2, ts)] if ts % 16 == 0 else [slice(0, ts)]
    x1 = [x_ref[0, r, :] + _dot(mhg_ref[0, r, :], wo_ref[0:w, :])
          + _dot(mda_ref[0, r, :], wo_ref[w:2 * w, :]) for r in halves]
    fold_until(0.25)
    q = [(_dot(_rms(v, gq_ref[...]).astype(BF16), wq_ref[...]) * (float(dh) ** -0.5)).astype(BF16)
         for v in x1]
    fold_until(0.5)
    mk = lambda h: mk_ref[0, :, h * dh:(h + 1) * dh]
    mv = lambda h: mv_ref[0, :, h * dh:(h + 1) * dh]
    o = [_cross_attn_heads(v, mk, mv) for v in q]
    fold_until(0.75)
    out = [v1 + _dot(vo.astype(BF16), wmo_ref[...]) for v1, vo in zip(x1, o)]
    fold_until(1.0)
    for r, v in zip(halves, out):
        x2_ref[0, r, :] = v


def _out_cross_decode(x, mhg, mda, w_out, gq, w_mq, mkb, mvb, w_mo, ts, page_table, qd, k_new, v_new,
                      cache_k, cache_v, n_pg, n_slot, g_lo, g_hi, softmax_state=None):
    B, S, D = x.shape
    W = mhg.shape[-1]
    N, MW = mkb.shape[1], mkb.shape[2]
    Ts = page_table.shape[0]
    n_phys, rows, dk = cache_k.shape
    Wq = qd.shape[-1]
    sps = Ts // (B * (S // ts))
    tok = lambda d: pl.BlockSpec((1, ts, d), lambda b, s, pt: (b, s, 0))
    mem = pl.BlockSpec((1, N, MW), lambda b, s, pt: (b, 0, 0))
    const = lambda shape: pl.BlockSpec(shape, lambda b, s, pt: (0,) * len(shape),
                                       pipeline_mode=pl.Buffered(1))
    sample = const((Ts, 1, Wq))
    st = pl.BlockSpec((Ts, 2 * DA_HEADS, LANES), lambda b, s, pt: (0, 0, 0))
    hbm = pl.BlockSpec(memory_space=pl.ANY)
    grid_spec = pltpu.PrefetchScalarGridSpec(
        num_scalar_prefetch=1,
        grid=(B, S // ts),
        in_specs=[tok(D), tok(W), tok(W), const(w_out.shape), const((1, D)), const(w_mq.shape),
                  mem, mem, const(w_mo.shape), sample, sample, sample,
                  const(st.block_shape), const(st.block_shape), const(st.block_shape), hbm, hbm],
        out_specs=[tok(D), st, st, st],
        scratch_shapes=_ring_scratch(n_slot, n_pg, rows, dk),
    )
    st_shape = jax.ShapeDtypeStruct((Ts, 2 * DA_HEADS, LANES), F32)
    if softmax_state is None:
        softmax_state = (jnp.zeros(st_shape.shape, F32),) * 3
    x2, m, l, a = pl.pallas_call(
        functools.partial(_out_cross_decode_kernel, n_pg=n_pg, sps=sps, g_lo=g_lo, g_hi=g_hi),
        grid_spec=grid_spec,
        out_shape=[jax.ShapeDtypeStruct((B, S, D), F32), st_shape, st_shape, st_shape],
        compiler_params=_cparams(("arbitrary", "arbitrary"), VMEM_LIMIT_CAP),
        name="out_cross_decode",
    )(page_table, x, mhg, mda, w_out, gq, w_mq, mkb, mvb, w_mo, qd.reshape(Ts, 1, Wq),
      k_new.reshape(Ts, 1, Wq), v_new.reshape(Ts, 1, Wq), *softmax_state, cache_k, cache_v)
    return x2, (m, l, a)


def _out_cross_sample_kernel(x_ref, mhg_ref, mda_ref, wo_ref, gq_ref, wq_ref, mk_hbm, mv_hbm,
                             wmo_ref, x2_ref, x1_s, q_s, o_s, kbuf, vbuf, sem):
    b = pl.program_id(0)
    n_b = pl.num_programs(0)
    w = mhg_ref.shape[-1]

    def copies(bb, slot):
        out = []
        for h in range(MEM_HEADS):
            out.append(pltpu.make_async_copy(mk_hbm.at[bb, :, h, :], kbuf.at[slot, h], sem.at[slot, 0]))
            out.append(pltpu.make_async_copy(mv_hbm.at[bb, :, h, :], vbuf.at[slot, h], sem.at[slot, 1]))
        return out

    @pl.when(b == 0)
    def _():
        for cp in copies(0, 0):
            cp.start()
        x1 = x_ref[...] + _dot(mhg_ref[...], wo_ref[0:w, :]) + _dot(mda_ref[...], wo_ref[w:2 * w, :])
        x1_s[...] = x1
        hq = _rms(x1, gq_ref[...]).astype(BF16)
        dh = wq_ref.shape[1] // MEM_HEADS
        q_s[...] = _dot(hq, wq_ref[...]) * (float(dh) ** -0.5)

    slot = b % 2

    @pl.when(b + 1 < n_b)
    def _():
        for cp in copies(b + 1, 1 - slot):
            cp.start()

    for cp in copies(b, slot):
        cp.wait()
    q = q_s[pl.ds(b, 1), :].astype(BF16)
    o_s[pl.ds(b, 1), :] = _cross_attn_heads(q, lambda h: kbuf[slot, h].astype(BF16),
                                            lambda h: vbuf[slot, h].astype(BF16))

    @pl.when(b == n_b - 1)
    def _():
        x2_ref[...] = x1_s[...] + _dot(o_s[...].astype(BF16), wmo_ref[...])


def _out_cross_sample(x, mhg, mda, w_out, gq, w_mq, mem_k, mem_v, w_mo):
    T, D = x.shape
    _, N, heads, dh = mem_k.shape
    MW = heads * dh
    hbm = pl.BlockSpec(memory_space=pl.ANY)
    return pl.pallas_call(
        _out_cross_sample_kernel,
        grid=(T,),
        in_specs=[_const_spec(x.shape), _const_spec(mhg.shape), _const_spec(mda.shape),
                  _const_spec(w_out.shape), _const_spec((1, D)), _const_spec(w_mq.shape),
                  hbm, hbm, _const_spec(w_mo.shape)],
        out_specs=pl.BlockSpec((T, D), lambda b: (0, 0)),
        out_shape=jax.ShapeDtypeStruct((T, D), F32),
        scratch_shapes=[pltpu.VMEM((T, D), F32), pltpu.VMEM((T, MW), F32), pltpu.VMEM((T, MW), F32),
                        pltpu.VMEM((2, heads, N, dh), F32), pltpu.VMEM((2, heads, N, dh), F32),
                        pltpu.SemaphoreType.DMA((2, 2))],
        compiler_params=_cparams(("arbitrary",), 32 << 20),
        name="out_cross_sample",
    )(x, mhg, mda, w_out, gq, w_mq, mem_k, mem_v, w_mo)


def _ffn_kernel(x_ref, gf_ref, wg_ref, wu_ref, wd_ref, gfin_ref, y_ref):
    x = x_ref[...]
    h = _rms(x, gf_ref[...]).astype(BF16)
    g = _dot(h, wg_ref[...])
    u = _dot(h, wu_ref[...])
    a = (g * _sigmoid(g) * u).astype(BF16)
    x3 = x + _dot(a, wd_ref[...])
    y_ref[...] = _rms(x3, gfin_ref[...])


def _ffn(x, gf, wg, wu, wd, gfin, ts):
    T, D = x.shape
    tok = pl.BlockSpec((ts, D), lambda i: (i, 0))
    return pl.pallas_call(
        _ffn_kernel,
        grid=(T // ts,),
        in_specs=[tok, _const_spec((1, D)), _const_spec(wg.shape), _const_spec(wu.shape),
                  _const_spec(wd.shape), _const_spec((1, D))],
        out_specs=tok,
        out_shape=jax.ShapeDtypeStruct((T, D), F32),
        compiler_params=_cparams(("arbitrary",), VMEM_LIMIT_CAP),
        name="ffn",
    )(x, gf, wg, wu, wd, gfin)


def _sample_mix_kernel(x_ref, g_ref, w_ref, lb_ref, cos_ref, sin_ref,
                       qd_ref, k_ref, v_ref, hq_ref, kk_ref, f_ref, vh_ref, gate_ref):
    h = _rms(x_ref[...], g_ref[...]).astype(BF16)
    z = _dot(h, w_ref[...])
    lb = _lower_bound(lb_ref)
    hq, k_hg, logf, hi, gate, dq, dk, dv = _mixer_sections(z, lb)
    cos = cos_ref[...]
    sin = sin_ref[...]
    qd_ref[...] = _rope(dq, cos, sin) * (float(LANES // 2) ** -0.5)
    k_ref[...] = _rope(dk, cos, sin)
    v_ref[...] = dv
    hq_ref[...] = hq
    kk_ref[...] = k_hg
    f_ref[...] = jnp.exp(logf)
    vh_ref[...] = hi
    gate_ref[...] = gate


def _sample_mix(x, g, w_in, hg_lb, cos, sin):
    T, D = x.shape
    W = w_in.shape[1] // 7
    sds = jax.ShapeDtypeStruct
    return pl.pallas_call(
        _sample_mix_kernel,
        out_shape=[sds((T, W), F32)] * 8,
        compiler_params=_cparams(None, 32 << 20),
        name="sample_mix",
    )(x, g, w_in, hg_lb, cos, sin)


def _sample_hgrn_kernel(hq_ref, kk_ref, f_ref, vh_ref, gate_ref, onorm_ref, s0_ref,
                        mix_ref, s1_ref):
    nb = s0_ref.shape[0]
    g0 = pl.program_id(0) * nb
    W = hq_ref.shape[-1]

    def columns(ref, h):
        rows = ref[pl.ds(pl.multiple_of(g0, nb), nb), h * LANES:(h + 1) * LANES]
        pad = jnp.zeros((LANES - nb, LANES), F32)
        return jnp.concatenate([rows, pad], axis=0).T

    o_rows = []
    for h in range(HG_HEADS):
        q_t, k_t, f_t = columns(hq_ref, h), columns(kk_ref, h), columns(f_ref, h)
        v_rows = vh_ref[pl.ds(pl.multiple_of(g0, nb), nb), h * LANES:(h + 1) * LANES]
        o_h = []
        for j in range(nb):
            bcast = lambda tile: jnp.broadcast_to(tile[:, j:j + 1], (LANES, LANES))
            s_new = bcast(f_t) * s0_ref[j, h] + bcast(k_t) * v_rows[j:j + 1, :]
            s1_ref[j, h] = s_new
            o_h.append(jnp.sum(bcast(q_t) * s_new, axis=0, keepdims=True))
        o_rows.append(jnp.concatenate(o_h, axis=0))
    o = jnp.concatenate(o_rows, axis=-1)
    gate = gate_ref[pl.ds(pl.multiple_of(g0, nb), nb), :]
    mix_ref[...] = (_head_rms(o, onorm_ref[...]) * gate).astype(BF16)


def _sample_hgrn(hq, kk, f, vh, gate, onorm, s0, nb=8):
    T, W = hq.shape
    st = pl.BlockSpec((nb, HG_HEADS, LANES, LANES), lambda i: (i, 0, 0, 0))
    full = _const_spec((T, W))
    return pl.pallas_call(
        _sample_hgrn_kernel,
        grid=(T // nb,),
        in_specs=[full] * 5 + [_const_spec((1, LANES)), st],
        out_specs=[pl.BlockSpec((nb, W), lambda i: (i, 0)), st],
        out_shape=[jax.ShapeDtypeStruct((T, W), BF16), jax.ShapeDtypeStruct(s0.shape, F32)],
        compiler_params=_cparams(("arbitrary",), 32 << 20),
        name="sample_hgrn",
    )(hq, kk, f, vh, gate, onorm, s0)


def _head_rows(x):
    return jnp.concatenate([x[:, h * LANES:(h + 1) * LANES] for h in range(DA_HEADS)] * 2, axis=0)


def _decode_queries(q):
    row = lax.broadcasted_iota(jnp.int32, (2 * DA_HEADS, LANES), 0)
    lane = lax.broadcasted_iota(jnp.int32, (2 * DA_HEADS, LANES), 1)
    return jnp.where((row // DA_HEADS) == (lane // (LANES // 2)), _head_rows(q), 0.0)


def _decode_seed(q, k_new, v_new, state):
    q8_s, m_s, l_s, acc_s = state
    q8 = _decode_queries(q)
    q8_s[...] = q8.astype(BF16)
    m_s[...] = jnp.sum(q8 * _head_rows(k_new), axis=-1, keepdims=True)
    l_s[...] = jnp.ones(l_s.shape, F32)
    acc_s[...] = _head_rows(v_new)


def _decode_resume(q, m, l, acc, state):
    q8_s, m_s, l_s, acc_s = state
    q8_s[...] = _decode_queries(q).astype(BF16)
    m_s[...] = m[:, 0:1]
    l_s[...] = l[:, 0:1]
    acc_s[...] = acc


def _decode_save(state):
    _, m_s, l_s, acc_s = state
    shape = acc_s.shape
    return jnp.broadcast_to(m_s[...], shape), jnp.broadcast_to(l_s[...], shape), acc_s[...]


def _decode_scores(k_pages, state):
    H = DA_HEADS
    q8 = state[0][...]
    s = jnp.concatenate([_dot_nt(q8, k.astype(BF16)) for k in k_pages], axis=-1)
    row = lax.broadcasted_iota(jnp.int32, s.shape, 0)
    col = lax.broadcasted_iota(jnp.int32, s.shape, 1)
    return jnp.where((col % H) == (row % H), s, -jnp.inf)


def _decode_update(s, v_pages, state):
    _, m_s, l_s, acc_s = state
    m_old = m_s[...]
    m_new = jnp.maximum(m_old, jnp.max(s, axis=-1, keepdims=True))
    alpha = jnp.exp(m_old - m_new)
    pr = jnp.exp(s - m_new)
    l_s[...] = alpha * l_s[...] + jnp.sum(pr, axis=-1, keepdims=True)
    pr = pr.astype(BF16)
    rows = v_pages[0].shape[0]
    pv = _dot(pr[:, 0:rows], v_pages[0].astype(BF16))
    for p in range(1, len(v_pages)):
        pv = pv + _dot(pr[:, p * rows:(p + 1) * rows], v_pages[p].astype(BF16))
    acc_s[...] = alpha * acc_s[...] + pv
    m_s[...] = m_new


def _decode_finish(lam_ref, onorm_ref, lam_init, state):
    _, _, l_s, acc_s = state
    H = DA_HEADS
    lam = _lambda(lam_ref, lam_init)
    o = acc_s[...] / l_s[...]
    y = _rms(o[0:H] - lam * o[H:2 * H], onorm_ref[...]) * (1.0 - lam_init)
    return jnp.concatenate([y[h:h + 1, :] for h in range(H)], axis=-1).astype(BF16)


def _paged_attn_kernel(pt_ref, q_ref, kn_ref, vn_ref, lam_ref, onorm_ref, *refs,
                       lam_init, n_pg):
    k_refs = refs[:n_pg]
    v_refs = refs[n_pg:2 * n_pg]
    o_ref = refs[2 * n_pg]
    state = refs[2 * n_pg + 1:]
    j = pl.program_id(1)

    @pl.when(j == 0)
    def _():
        _decode_seed(q_ref[0], kn_ref[0], vn_ref[0], state)

    _decode_update(_decode_scores([r[0] for r in k_refs], state), [r[0] for r in v_refs], state)

    @pl.when(j == pl.num_programs(1) - 1)
    def _():
        o_ref[0] = _decode_finish(lam_ref, onorm_ref, lam_init, state)


def _paged_attn(page_table, qd, k_new, v_new, da_lambda, onorm, cache_k, cache_v, lam_init, n_pg):
    T, n_pages = page_table.shape
    n_phys, rows, dk = cache_k.shape
    W = qd.shape[-1]
    row = pl.BlockSpec((1, 1, W), lambda b, j, pt: (b, 0, 0))

    def page_spec(p):
        return pl.BlockSpec((1, rows, dk), lambda b, j, pt: (pt[b, j * n_pg + p], 0, 0))

    const = lambda shape: pl.BlockSpec(shape, lambda b, j, pt: (0,) * len(shape))
    grid_spec = pltpu.PrefetchScalarGridSpec(
        num_scalar_prefetch=1,
        grid=(T, n_pages // n_pg),
        in_specs=[row, row, row, const(da_lambda.shape), const((1, LANES))]
        + [page_spec(p) for p in range(n_pg)] * 2,
        out_specs=row,
        scratch_shapes=[pltpu.VMEM((2 * DA_HEADS, LANES), BF16),
                        pltpu.VMEM((2 * DA_HEADS, 1), F32),
                        pltpu.VMEM((2 * DA_HEADS, 1), F32),
                        pltpu.VMEM((2 * DA_HEADS, LANES), F32)],
    )
    out = pl.pallas_call(
        functools.partial(_paged_attn_kernel, lam_init=lam_init, n_pg=n_pg),
        grid_spec=grid_spec,
        out_shape=jax.ShapeDtypeStruct((T, 1, W), BF16),
        compiler_params=_cparams(("arbitrary", "arbitrary"), 48 << 20),
        name="paged_diff_attn",
    )(page_table, qd.reshape(T, 1, W), k_new.reshape(T, 1, W), v_new.reshape(T, 1, W),
      da_lambda, onorm, *([cache_k] * n_pg), *([cache_v] * n_pg))
    return out.reshape(T, W)


def _page_ring(i, n_steps, pt_ref, ck_hbm, cv_hbm, kbuf, vbuf, sem, state, *,
               n_pg, sps, g_lo, g_hi, begin, end):
    n_slot = kbuf.shape[0]
    rows = kbuf.shape[1] // n_pg
    groups = [(ls, g) for ls in range(sps) for g in range(g_lo, g_hi)]
    n_grp = len(groups)

    def copies(b, g, slot):
        out = []
        for p in range(n_pg):
            page = pt_ref[b, g * n_pg + p]
            dst = pl.ds(p * rows, rows)
            out.append(pltpu.make_async_copy(ck_hbm.at[page], kbuf.at[slot, dst], sem.at[slot, 0]))
            out.append(pltpu.make_async_copy(cv_hbm.at[page], vbuf.at[slot, dst], sem.at[slot, 1]))
        return out

    def start(b, g, slot):
        for cp in copies(b, g, slot):
            cp.start()

    ahead = n_slot - 1

    @pl.when(i == 0)
    def _():
        for n in range(ahead):
            start(groups[n][0], groups[n][1], n % n_slot)

    def fold_group(n):
        ls, g = groups[n]
        b = i * sps + ls
        slot = n % n_slot
        nxt = n + ahead
        if nxt < n_grp:
            start(i * sps + groups[nxt][0], groups[nxt][1], nxt % n_slot)
        else:
            @pl.when(i + 1 < n_steps)
            def _():
                ls2, g2 = groups[nxt - n_grp]
                start((i + 1) * sps + ls2, g2, nxt % n_slot)
        if n in scored:
            s = scored.pop(n)
        else:
            s = arrive_and_score(n)
        if n + 1 < n_grp and groups[n + 1][0] == ls:
            scored[n + 1] = arrive_and_score(n + 1)
        _decode_update(s, [vbuf[slot, p * rows:(p + 1) * rows, :] for p in range(n_pg)], state)
        if g == g_hi - 1:
            end(b)

    scored = {}

    def arrive_and_score(n):
        ls, g = groups[n]
        b = i * sps + ls
        slot = n % n_slot
        for cp in copies(b, g, slot):
            cp.wait()
        if g == g_lo:
            begin(b)
        return _decode_scores([kbuf[slot, p * rows:(p + 1) * rows, :] for p in range(n_pg)], state)

    return fold_group, n_grp


def _ffn_decode_kernel(pt_ref, x_ref, gf_ref, wg_ref, wu_ref, wd_ref, gfin_ref,
                       q_ref, m_ref, l_ref, a_ref, lam_ref, onorm_ref, ck_hbm, cv_hbm,
                       y_ref, o_ref, kbuf, vbuf, sem, q8_s, m_s, l_s, acc_s,
                       *, lam_init, n_pg, sps, g_lo, col_bounds):
    i = pl.program_id(0)
    state = (q8_s, m_s, l_s, acc_s)

    def begin(b):
        _decode_resume(q_ref[b], m_ref[b], l_ref[b], a_ref[b], state)

    def end(b):
        o_ref[b] = _decode_finish(lam_ref, onorm_ref, lam_init, state)

    fold_group, n_grp = _page_ring(i, pl.num_programs(0), pt_ref, ck_hbm, cv_hbm, kbuf, vbuf, sem,
                                   state, n_pg=n_pg, sps=sps, g_lo=g_lo,
                                   g_hi=pt_ref.shape[1] // n_pg, begin=begin, end=end)

    x = x_ref[...]
    h = _rms(x, gf_ref[...]).astype(BF16)
    n_col = len(col_bounds) - 1

    def up(k):
        lo, hi = col_bounds[k], col_bounds[k + 1]
        return _dot(h, wg_ref[:, lo:hi]), _dot(h, wu_ref[:, lo:hi])

    folded = 0
    acc = None
    gu = up(0)
    for k in range(n_col):
        g, u = gu
        part = _dot((g * _sigmoid(g) * u).astype(BF16), wd_ref[col_bounds[k]:col_bounds[k + 1], :])
        if k + 1 < n_col:
            gu = up(k + 1)
        due = (k + 1) * n_grp // n_col
        for n in range(folded, due):
            fold_group(n)
        folded = due
        acc = part if acc is None else acc + part
    y_ref[...] = _rms(x + acc, gfin_ref[...])


def _ring_scratch(n_slot, n_pg, rows, dk):
    return [pltpu.VMEM((n_slot, n_pg * rows, dk), F32),
            pltpu.VMEM((n_slot, n_pg * rows, dk), F32),
            pltpu.SemaphoreType.DMA((n_slot, 2)),
            pltpu.VMEM((2 * DA_HEADS, LANES), BF16),
            pltpu.VMEM((2 * DA_HEADS, 1), F32),
            pltpu.VMEM((2 * DA_HEADS, 1), F32),
            pltpu.VMEM((2 * DA_HEADS, LANES), F32)]


def _ffn_decode(x, gf, wg, wu, wd, gfin, ts, page_table, qd, softmax_state, da_lambda, onorm,
                cache_k, cache_v, lam_init, n_pg, n_slot, g_lo):
    T, D = x.shape
    F = wg.shape[1]
    Ts, n_pages = page_table.shape
    n_phys, rows, dk = cache_k.shape
    W = qd.shape[-1]
    n_tiles = T // ts
    sps = Ts // n_tiles
    mxu_n = 2 * LANES
    col_bounds = tuple(range(0, F, mxu_n)) + (F,)
    tok = pl.BlockSpec((ts, D), lambda i, pt: (i, 0))
    const = lambda shape: pl.BlockSpec(shape, lambda i, pt: (0,) * len(shape),
                                       pipeline_mode=pl.Buffered(1))
    st = const((Ts, 2 * DA_HEADS, LANES))
    hbm = pl.BlockSpec(memory_space=pl.ANY)
    grid_spec = pltpu.PrefetchScalarGridSpec(
        num_scalar_prefetch=1,
        grid=(n_tiles,),
        in_specs=[tok, const((1, D)), const(wg.shape), const(wu.shape), const(wd.shape),
                  const((1, D)), const((Ts, 1, W)), st, st, st, const(da_lambda.shape),
                  const((1, LANES)), hbm, hbm],
        out_specs=[tok, pl.BlockSpec((Ts, 1, W), lambda i, pt: (0, 0, 0))],
        scratch_shapes=_ring_scratch(n_slot, n_pg, rows, dk),
    )
    y, o = pl.pallas_call(
        functools.partial(_ffn_decode_kernel, lam_init=lam_init, n_pg=n_pg, sps=sps, g_lo=g_lo,
                          col_bounds=col_bounds),
        grid_spec=grid_spec,
        out_shape=[jax.ShapeDtypeStruct((T, D), F32), jax.ShapeDtypeStruct((Ts, 1, W), BF16)],
        compiler_params=_cparams(("arbitrary",), VMEM_LIMIT_CAP),
        name="ffn_decode",
    )(page_table, x, gf, wg, wu, wd, gfin, qd.reshape(Ts, 1, W), *softmax_state,
      da_lambda, onorm, cache_k, cache_v)
    return y, o.reshape(Ts, W)


def _rope_tables(pos, dh, n_maps):
    inv = ROPE_THETA ** (-jnp.arange(0, dh, 2, dtype=F32) / dh)
    ang = pos.astype(F32)[:, None] * inv[None, :]
    cos = jnp.cos(ang)
    sin = jnp.sin(ang)
    cos = jnp.tile(jnp.concatenate([cos, cos], axis=-1), (1, n_maps))
    sin = jnp.tile(jnp.concatenate([-sin, sin], axis=-1), (1, n_maps))
    return cos, sin


def _pick_tile(n, pref):
    t = min(n, pref)
    while n % t:
        t //= 2
    return t


def kernel(x_prompt, x_sample, mem_prompt, cache_k, cache_v, cache_mem_k, cache_mem_v, state_hgrn, page_table, norm_mix, w_in, hg_lb, hg_onorm, da_lambda, da_onorm, w_out, norm_mem_q, norm_mem_kv, w_mq, w_mk, w_mv, w_mo, norm_ffn, w_gate, w_up, w_down, norm_final):
    B, S, D = x_prompt.shape
    T = x_sample.shape[0]
    depth = w_in.shape[0]
    assert depth == 1 and x_sample.shape[1] == 1
    l = 0
    lam_init = 0.8 - 0.6 * math.exp(-0.3 * l)
    n_phys, page = cache_k.shape[1], cache_k.shape[2]
    past_len = page_table.shape[1] * page
    W = w_in.shape[2] // 7

    bf = lambda w: w.astype(BF16)
    w_in_b, w_out_b = bf(w_in[l]), bf(w_out[l])
    w_mq_b, w_mk_b, w_mv_b, w_mo_b = bf(w_mq[l]), bf(w_mk[l]), bf(w_mv[l]), bf(w_mo[l])
    w_gate_b, w_up_b, w_down_b = bf(w_gate[l]), bf(w_up[l]), bf(w_down[l])
    row = lambda g: g.reshape(1, -1)
    lb2 = hg_lb[l:l + 2]
    lam_p = da_lambda[l]

    cos_p, sin_p = _rope_tables(jnp.arange(S), LANES // 2, 2 * DA_HEADS)
    cos_s, sin_s = _rope_tables(past_len + jnp.arange(1), LANES // 2, 2 * DA_HEADS)
    cos_s = jnp.broadcast_to(cos_s, (T, W))
    sin_s = jnp.broadcast_to(sin_s, (T, W))

    mk, mv, mkb, mvb = _mem_kv(mem_prompt, row(norm_mem_kv[l]), w_mk_b, w_mv_b)
    qd, k_p, v_p, kb, v_att, mix_hg, hs_p = _prompt_mix(
        x_prompt, row(norm_mix[l]), w_in_b, lb2, cos_p, sin_p, row(hg_onorm[l]),
        ts=_pick_tile(S, 512))

    xs = x_sample.reshape(T, D)
    qd_s, k_s, v_s, hq_s, kk_s, f_s, vh_s, gate_s = _sample_mix(
        xs, row(norm_mix[l]), w_in_b, lb2, cos_s, sin_s)
    mix_hg_s, hs_s = _sample_hgrn(hq_s, kk_s, f_s, vh_s, gate_s, row(hg_onorm[l]), state_hgrn[l])
    ck = cache_k.reshape(depth * n_phys, page * DA_HEADS, LANES)
    cv = cache_v.reshape(depth * n_phys, page * DA_HEADS, LANES)

    n_pages = page_table.shape[1]
    tq, tk = _pick_tile(S, 2048), _pick_tile(S, 256)
    tok_ts = _pick_tile(S, 512)
    n_tiles = (B * S) // tok_ts
    n_pg = _pick_tile(n_pages, 8)
    n_slot = 4
    gps = n_pages // n_pg
    sps = T // n_tiles if T % n_tiles == 0 else 0
    attn_hosts = B * DA_HEADS * (S // tq) == n_tiles
    macs = (ATTN_MXU_DERATE * 3 * LANES // 2 * S * DA_HEADS if attn_hosts else 0,
            3 * D * D + 2 * mem_prompt.shape[1] * D,
            3 * D * w_gate.shape[2])
    ok = lambda g: (sps * g) % n_slot == 0
    splits = [(ga, gb) for ga in range(0, gps) for gb in range(1, gps - ga)
              if ok(ga) and ok(gb) and ok(gps - ga - gb) and (ga > 0) == attn_hosts]
    att_args = (qd, kb, v_att, lam_p, row(da_onorm[l]), lam_init, tq, tk)
    ffn_args = (row(norm_ffn[l]), w_gate_b, w_up_b, w_down_b, row(norm_final))
    if sps and splits:
        want = [m * gps / sum(macs) for m in macs]
        g_att, g_oc = min(splits, key=lambda s: (s[0] - want[0]) ** 2 + (s[1] - want[1]) ** 2
                          + (gps - s[0] - s[1] - want[2]) ** 2)
        ring = (page_table, qd_s, k_s, v_s, ck, cv, n_pg, n_slot)
        if g_att:
            mix_da, softmax_state = _diff_attn_decode(*att_args, *ring, g_att)
        else:
            mix_da, softmax_state = _diff_attn_prompt(*att_args), None
        x2, softmax_state = _out_cross_decode(
            x_prompt, mix_hg, mix_da, w_out_b, row(norm_mem_q[l]), w_mq_b, mkb, mvb, w_mo_b, tok_ts,
            *ring, g_att, g_att + g_oc, softmax_state)
        y_p, mix_da_s = _ffn_decode(x2.reshape(B * S, D), *ffn_args, tok_ts, page_table, qd_s,
                                    softmax_state, lam_p, row(da_onorm[l]), ck, cv, lam_init,
                                    n_pg, n_slot, g_att + g_oc)
    else:
        mix_da = _diff_attn_prompt(*att_args)
        x2 = _out_cross_prompt(x_prompt, mix_hg, mix_da, w_out_b, row(norm_mem_q[l]), w_mq_b,
                               mkb, mvb, w_mo_b, ts=tok_ts)
        y_p = _ffn(x2.reshape(B * S, D), *ffn_args, ts=tok_ts)
        mix_da_s = _paged_attn(page_table, qd_s, k_s, v_s, lam_p, row(da_onorm[l]), ck, cv,
                               lam_init, n_pg=_pick_tile(n_pages, 16))
    y_p = y_p.reshape(B, S, D)
    MW = w_mk.shape[2]
    x2_s = _out_cross_sample(xs, mix_hg_s, mix_da_s, w_out_b, row(norm_mem_q[l]), w_mq_b,
                             cache_mem_k[l], cache_mem_v[l],
                             w_mo_b)
    y_s = _ffn(x2_s, row(norm_ffn[l]), w_gate_b, w_up_b, w_down_b, row(norm_final), ts=T)

    dk = LANES
    return (y_p, y_s.reshape(T, 1, D),
            hs_p[None],
            k_p.reshape(1, B, S, DA_HEADS, dk), v_p.reshape(1, B, S, DA_HEADS, dk),
            mk[None], mv[None],
            hs_s[None],
            k_s.reshape(1, T, 1, DA_HEADS, dk), v_s.reshape(1, T, 1, DA_HEADS, dk))
```
